```python
import jax, jax.numpy as jnp
from jax import lax
import numpy as np


D_MODEL = 1024
BATCH = 8
SEQ = 4096
DEPTH = 4

MIX_WIDTH = D_MODEL
ATT_WIDTH = MIX_WIDTH // 2
RWKV_WIDTH = MIX_WIDTH - ATT_WIDTH
ATT_HEAD_DIM = 64
N_ATT_HEADS = ATT_WIDTH // ATT_HEAD_DIM
RWKV_HEAD_DIM = 64
N_RWKV_HEADS = RWKV_WIDTH // RWKV_HEAD_DIM
DILATED_PATTERNS = ((128, 1), (512, 4), (2048, 16))
ATT_BLOCK = 128
DECAY_LORA = 64
AAA_LORA = 64
VRES_LORA = 32
GATE_LORA = 128
RWKV_SHIFT_BASE = 3 * RWKV_WIDTH + DECAY_LORA + AAA_LORA + GATE_LORA
D_IN_BASE = 3 * ATT_WIDTH + RWKV_SHIFT_BASE
D_FF = 128 * ((8 * D_MODEL // 3 + 127) // 128)
N_EXPERTS = 8
TOP_K = 2
D_FF_EXPERT = 7 * D_MODEL // 2
N_DENSE = (DEPTH + 1) // 2
N_MOE = DEPTH // 2
DEEPNORM_ALPHA = float((2 * DEPTH) ** 0.25)
DEEPNORM_BETA = float((8 * DEPTH) ** -0.25)
LN_EPS = 1e-5
GN_EPS = 64e-5

kernel_name = 'hybrid_dilated_attn_rwkv7_moe_deepnorm'


def layer_norm(x, g, b):
    xf = x.astype(jnp.float32)
    mu = jnp.mean(xf, axis=-1, keepdims=True)
    var = jnp.mean(jnp.square(xf - mu), axis=-1, keepdims=True)
    return ((xf - mu) * lax.rsqrt(var + LN_EPS) * g + b).astype(x.dtype)


def token_shift(p, mu):
    prev = jnp.pad(p, ((0, 0), (1, 0), (0, 0)))[:, :-1]
    return p + (prev - p) * mu


def dilated_branch(q, k, v, window, dilation):
    B, S, H, Dh = q.shape
    steps = window // dilation
    L = S // dilation
    nb = -(-L // ATT_BLOCK)
    Lp = nb * ATT_BLOCK

    def blocks(t):
        t = t.reshape(B, L, dilation, H, Dh)
        t = jnp.pad(t, ((0, 0), (0, Lp - L), (0, 0), (0, 0), (0, 0)))
        return t.reshape(B, nb, ATT_BLOCK, dilation, H, Dh)

    def with_prev(t):
        prev = jnp.pad(t, ((0, 0), (1, 0), (0, 0), (0, 0), (0, 0), (0, 0)))[:, :-1]
        return jnp.concatenate([prev, t], axis=2)

    qb = blocks(q)
    kc = with_prev(blocks(k))
    vc = with_prev(blocks(v))
    s = jnp.einsum('bnqchd,bnkchd->bnchqk', qb, kc, preferred_element_type=jnp.float32)
    blk = jnp.arange(nb)[:, None] * ATT_BLOCK
    qi = blk + jnp.arange(ATT_BLOCK)[None, :]
    ki = blk + jnp.arange(2 * ATT_BLOCK)[None, :] - ATT_BLOCK
    dist = qi[:, :, None] - ki[:, None, :]
    valid = (dist >= 0) & (dist <= steps) & (ki[:, None, :] >= 0)
    s = jnp.where(valid[None, :, None, None], s, -jnp.inf)
    m = jnp.max(s, axis=-1, keepdims=True)
    p = jnp.exp(s - m)
    den = jnp.sum(p, axis=-1, keepdims=True)
    o = jnp.einsum('bnchqk,bnkchd->bnqchd', p / den, vc.astype(jnp.float32))
    lse = jnp.transpose((m + jnp.log(den))[..., 0], (0, 1, 4, 2, 3))
    o = o.reshape(B, Lp, dilation, H, Dh)[:, :L].reshape(B, S, H, Dh)
    lse = lse.reshape(B, Lp, dilation, H)[:, :L].reshape(B, S, H)
    return o, lse


def dilated_attention(q, k, v):
    outs, lses = [], []
    for window, dilation in DILATED_PATTERNS:
        o, lse = dilated_branch(q, k, v, window, dilation)
        outs.append(o)
        lses.append(lse)
    wts = jax.nn.softmax(jnp.stack(lses), axis=0)
    return jnp.sum(jnp.stack(outs) * wts[..., None], axis=0)


def rwkv7_time_mix(cols, v_first, decay_up, decay_base, aaa_up, aaa_base, gate_up,
                   k_k, k_a, r_k, lnx_g, lnx_b, vres_up, vres_base):
    B, S, _ = cols.shape
    H, N = N_RWKV_HEADS, RWKV_HEAD_DIM
    cuts = np.cumsum([RWKV_WIDTH] * 3 + [DECAY_LORA, AAA_LORA, GATE_LORA])
    r, k, v, wd, ad, gd, vd = jnp.split(cols, cuts, axis=-1)
    w_log = -jax.nn.softplus(-(decay_base + jnp.tanh(wd) @ decay_up)) - 0.5
    decay = jnp.exp(-jnp.exp(w_log))
    a = jax.nn.sigmoid(aaa_base + ad @ aaa_up)
    g = jax.nn.sigmoid(gd) @ gate_up
    if v_first is None:
        v_first = v
    else:
        v = v + (v_first - v) * jax.nn.sigmoid(vres_base + vd @ vres_up)
    heads = lambda t: t.reshape(B, S, H, N)
    kk = heads(k * k_k)
    kk = kk / jnp.maximum(jnp.sqrt(jnp.sum(kk * kk, axis=-1, keepdims=True)), 1e-12)
    k = k * (1.0 + (a - 1.0) * k_a)
    r_h, k_h, v_h, w_h, a_h = heads(r), heads(k), heads(v), heads(decay), heads(a)
    tm = lambda t: jnp.moveaxis(t, 1, 0)

    def step(state, inp):
        r_t, w_t, k_t, v_t, kk_t, a_t = inp
        sa = jnp.einsum('bhvk,bhk->bhv', state, -kk_t)
        state = (state * w_t[:, :, None, :] + sa[..., None] * (kk_t * a_t)[:, :, None, :]
                 + v_t[..., None] * k_t[:, :, None, :])
        return state, jnp.einsum('bhvk,bhk->bhv', state, r_t)

    state0 = jnp.zeros((B, H, N, N), jnp.float32)
    _, y = lax.scan(step, state0, (tm(r_h), tm(w_h), tm(k_h), tm(v_h), tm(kk), tm(a_h)))
    y = jnp.moveaxis(y, 0, 1)
    mu = jnp.mean(y, axis=-1, keepdims=True)
    var = jnp.mean(jnp.square(y - mu), axis=-1, keepdims=True)
    y = ((y - mu) * lax.rsqrt(var + GN_EPS)).reshape(B, S, RWKV_WIDTH) * lnx_g + lnx_b
    bonus = jnp.sum(r_h * k_h * r_k.reshape(H, N), axis=-1, keepdims=True) * v_h
    y = (y + bonus.reshape(B, S, RWKV_WIDTH)) * g
    return y, v_first


def hybrid_mixer(h, w_in_l, mu_l, w_out_l, v_first, decay_up, decay_base, aaa_up, aaa_base,
                 gate_up, k_k, k_a, r_k, lnx_g, lnx_b, vres_up, vres_base):
    B, S, _ = h.shape
    proj = h @ w_in_l
    q, k, v = jnp.split(proj[..., :3 * ATT_WIDTH], 3, axis=-1)
    ah = lambda t: t.reshape(B, S, N_ATT_HEADS, ATT_HEAD_DIM)
    att = dilated_attention(ah(q) * (ATT_HEAD_DIM ** -0.5), ah(k), ah(v)).reshape(B, S, ATT_WIDTH)
    cols = token_shift(proj[..., 3 * ATT_WIDTH:], mu_l).astype(jnp.float32)
    rw, v_first = rwkv7_time_mix(cols, v_first, decay_up, decay_base, aaa_up, aaa_base, gate_up,
                                 k_k, k_a, r_k, lnx_g, lnx_b, vres_up, vres_base)
    out = jnp.concatenate([att, rw], axis=-1).astype(h.dtype) @ w_out_l
    return out, v_first


def swiglu(h, wg, wu, wd):
    return (jax.nn.silu(h @ wg) * (h @ wu)) @ wd


def moe_swiglu(h, router_l, wg, wu, wd):
    logits = (h @ router_l).astype(jnp.float32)
    vals, idx = lax.top_k(logits, TOP_K)
    gates = jax.nn.softmax(vals, axis=-1)
    comb = jnp.sum(jax.nn.one_hot(idx, N_EXPERTS, dtype=jnp.float32) * gates[..., None], axis=-2)
    y = jnp.zeros(h.shape, jnp.float32)
    for e in range(N_EXPERTS):
        y = y + comb[..., e:e + 1] * swiglu(h, wg[e], wu[e], wd[e])
    return y.astype(h.dtype)


def setup_inputs(seed: int = 0) -> dict:
    key = jax.random.key(seed)
    ks = iter(jax.random.split(key, 40))
    nrm = lambda shape, scale: jax.random.normal(next(ks), shape, jnp.float32) * scale
    uni = lambda shape, lo, hi: jax.random.uniform(next(ks), shape, jnp.float32, lo, hi)
    L, D, W = DEPTH, D_MODEL, RWKV_WIDTH
    return {
        'x': nrm((BATCH, SEQ, D), 1.0),
        'w_in': nrm((L, D, D_IN_BASE), D ** -0.5),
        'w_in_vres': nrm((L - 1, D, VRES_LORA), D ** -0.5),
        'shift_mu': uni((L, RWKV_SHIFT_BASE), 0.0, 1.0),
        'shift_mu_vres': uni((L - 1, VRES_LORA), 0.0, 1.0),
        'decay_up': nrm((L, DECAY_LORA, W), 0.5 * DECAY_LORA ** -0.5),
        'decay_base': uni((L, W), -4.0, 1.0),
        'aaa_up': nrm((L, AAA_LORA, W), 0.5 * AAA_LORA ** -0.5),
        'aaa_base': nrm((L, W), 0.5),
        'vres_up': nrm((L - 1, VRES_LORA, W), 0.5 * VRES_LORA ** -0.5),
        'vres_base': nrm((L - 1, W), 0.5),
        'gate_up': nrm((L, GATE_LORA, W), GATE_LORA ** -0.5),
        'k_k': 0.85 + nrm((L, W), 0.05),
        'k_a': 1.0 + nrm((L, W), 0.05),
        'r_k': nrm((L, W), 0.1),
        'lnx_g': 1.0 + nrm((L, W), 0.05),
        'lnx_b': nrm((L, W), 0.02),
        'w_out': nrm((L, MIX_WIDTH, D), DEEPNORM_BETA * MIX_WIDTH ** -0.5),
        'ln1_g': 1.0 + nrm((L, D), 0.05),
        'ln1_b': nrm((L, D), 0.02),
        'ln2_g': 1.0 + nrm((L, D), 0.05),
        'ln2_b': nrm((L, D), 0.02),
        'ffn_w_gate': nrm((N_DENSE, D, D_FF), D ** -0.5),
        'ffn_w_up': nrm((N_DENSE, D, D_FF), D ** -0.5),
        'ffn_w_down': nrm((N_DENSE, D_FF, D), DEEPNORM_BETA * D_FF ** -0.5),
        'router': nrm((N_MOE, D, N_EXPERTS), D ** -0.5),
        'moe_w_gate': nrm((N_MOE, N_EXPERTS, D, D_FF_EXPERT), D ** -0.5),
        'moe_w_up': nrm((N_MOE, N_EXPERTS, D, D_FF_EXPERT), D ** -0.5),
        'moe_w_down': nrm((N_MOE, N_EXPERTS, D_FF_EXPERT, D), DEEPNORM_BETA * D_FF_EXPERT ** -0.5),
    }


def reference(x, w_in, w_in_vres, shift_mu, shift_mu_vres, decay_up, decay_base, aaa_up,
              aaa_base, vres_up, vres_base, gate_up, k_k, k_a, r_k, lnx_g, lnx_b, w_out,
              ln1_g, ln1_b, ln2_g, ln2_b, ffn_w_gate, ffn_w_up, ffn_w_down, router,
              moe_w_gate, moe_w_up, moe_w_down):
    v_first = None
    for l in range(DEPTH):
        if l == 0:
            w_in_l, mu_l, vu, vb = w_in[0], shift_mu[0], None, None
        else:
            w_in_l = jnp.concatenate([w_in[l], w_in_vres[l - 1]], axis=1)
            mu_l = jnp.concatenate([shift_mu[l], shift_mu_vres[l - 1]], axis=0)
            vu, vb = vres_up[l - 1], vres_base[l - 1]
        mix, v_first = hybrid_mixer(x, w_in_l, mu_l, w_out[l], v_first, decay_up[l], decay_base[l],
                                    aaa_up[l], aaa_base[l], gate_up[l], k_k[l], k_a[l], r_k[l],
                                    lnx_g[l], lnx_b[l], vu, vb)
        x = layer_norm(DEEPNORM_ALPHA * x + mix, ln1_g[l], ln1_b[l])
        if l % 2 == 0:
            i = l // 2
            f = swiglu(x, ffn_w_gate[i], ffn_w_up[i], ffn_w_down[i])
        else:
            i = l // 2
            f = moe_swiglu(x, router[i], moe_w_gate[i], moe_w_up[i], moe_w_down[i])
        x = layer_norm(DEEPNORM_ALPHA * x + f, ln2_g[l], ln2_b[l])
    return x
```

```python
import functools

import jax
import jax.numpy as jnp
from jax import lax
from jax.experimental import pallas as pl
from jax.experimental.pallas import tpu as pltpu

ATT_HEAD_DIM = 64
RWKV_HEAD_DIM = 64
DILATIONS = (1, 4, 16)
ATT_BLOCK = 128
DECAY_LORA = 64
AAA_LORA = 64
GATE_LORA = 128
VRES_LORA = 32
N_EXPERTS = 8
LN_EPS = 1e-5
GN_EPS = 64e-5

LANES = 128
LORA_PAD = 384
VMEM_LIMIT = 56 * 1024 * 1024
NEG_BIG = -1e30

F32 = jnp.float32
BF16 = jnp.bfloat16


def _params(*sem):
    return pltpu.CompilerParams(dimension_semantics=sem, vmem_limit_bytes=VMEM_LIMIT)


def _inproj_kernel(x_ref, wqkv_ref, wrkv_ref, wlora_ref, mu_rkv_ref, mu_lora_ref,
                   qkv_ref, rkv_ref, lora_ref, carry_rkv, carry_lora, *, tiles_per_seq):
    i = pl.program_id(0)
    xb = x_ref[...].astype(BF16)
    qkv_ref[...] = jnp.dot(xb, wqkv_ref[...], preferred_element_type=F32).astype(BF16)

    first = (i % tiles_per_seq) == 0

    def shifted(w_ref, mu_ref, carry_ref, out_ref):
        p = jnp.dot(xb, w_ref[...], preferred_element_type=F32)
        tm = p.shape[0]
        prev_row = jnp.where(first, 0.0, carry_ref[...])
        rolled = pltpu.roll(p, 1, axis=0)
        row = lax.broadcasted_iota(jnp.int32, p.shape, 0)
        prev = jnp.where(row == 0, prev_row, rolled)
        out_ref[...] = p + (prev - p) * mu_ref[...]
        carry_ref[...] = p[tm - 1:tm, :]

    shifted(wrkv_ref, mu_rkv_ref, carry_rkv, rkv_ref)
    shifted(wlora_ref, mu_lora_ref, carry_lora, lora_ref)


def _inproj(x2, wqkv, wrkv, wlora, mu_rkv, mu_lora, seq, tm=512):
    m, d = x2.shape
    nq, nr, nl = wqkv.shape[1], wrkv.shape[1], wlora.shape[1]
    tm = min(tm, seq)
    full = lambda shape: pl.BlockSpec(shape, lambda i: (0, 0))
    return pl.pallas_call(
        functools.partial(_inproj_kernel, tiles_per_seq=seq // tm),
        grid=(m // tm,),
        in_specs=[pl.BlockSpec((tm, d), lambda i: (i, 0)),
                  full((d, nq)), full((d, nr)), full((d, nl)), full((1, nr)), full((1, nl))],
        out_specs=[pl.BlockSpec((tm, nq), lambda i: (i, 0)),
                   pl.BlockSpec((tm, nr), lambda i: (i, 0)),
                   pl.BlockSpec((tm, nl), lambda i: (i, 0))],
        out_shape=[jax.ShapeDtypeStruct((m, nq), BF16),
                   jax.ShapeDtypeStruct((m, nr), F32),
                   jax.ShapeDtypeStruct((m, nl), F32)],
        scratch_shapes=[pltpu.VMEM((1, nr), F32), pltpu.VMEM((1, nl), F32)],
        compiler_params=_params("arbitrary"),
        name="inproj",
    )(x2, wqkv, wrkv, wlora, mu_rkv, mu_lora)


def _attn_kernel(q_ref, kp_ref, kc_ref, vp_ref, vc_ref, o_ref, lse_ref):
    n = pl.program_id(3)
    q = q_ref[0]
    kp, kc, vp, vc = kp_ref[0], kc_ref[0], vp_ref[0], vc_ref[0]
    blk = q.shape[0]
    row = lax.broadcasted_iota(jnp.int32, (blk, blk), 0)
    col = lax.broadcasted_iota(jnp.int32, (blk, blk), 1)
    cur_ok = col <= row
    prev_ok = (col >= row) & (n > 0)
    lane = lax.broadcasted_iota(jnp.int32, q.shape, 1)
    nt = (((1,), (1,)), ((), ()))
    o_acc = jnp.zeros(q.shape, F32)
    lse_acc = jnp.zeros(q.shape, F32)
    for h in range(LANES // ATT_HEAD_DIM):
        in_head = (lane >= h * ATT_HEAD_DIM) & (lane < (h + 1) * ATT_HEAD_DIM)
        qh = jnp.where(in_head, q, jnp.zeros_like(q))
        sp = jnp.where(prev_ok, lax.dot_general(qh, kp, nt, preferred_element_type=F32), NEG_BIG)
        sc = jnp.where(cur_ok, lax.dot_general(qh, kc, nt, preferred_element_type=F32), NEG_BIG)
        m = jnp.maximum(jnp.max(sp, axis=-1, keepdims=True), jnp.max(sc, axis=-1, keepdims=True))
        pp = jnp.exp(sp - m)
        pc = jnp.exp(sc - m)
        den = jnp.sum(pp, axis=-1, keepdims=True) + jnp.sum(pc, axis=-1, keepdims=True)
        o = (jnp.dot(pp.astype(BF16), vp, preferred_element_type=F32)
             + jnp.dot(pc.astype(BF16), vc, preferred_element_type=F32)) / den
        o_acc = jnp.where(in_head, o, o_acc)
        lse_acc = jnp.where(in_head, m + jnp.log(den), lse_acc)
    o_ref[0] = o_acc
    lse_ref[0] = lse_acc


def _attn_pattern(qkv3, dil):
    b, s, c3 = qkv3.shape
    a = c3 // 3
    npair = a // LANES
    l = s // dil
    nb = l // ATT_BLOCK
    view = qkv3.reshape(b, l, dil * c3)
    cpb = c3 // LANES
    qmap = lambda bi, c, hp, n: (bi, n, c * cpb + hp)
    kcur = lambda bi, c, hp, n: (bi, n, c * cpb + npair + hp)
    kprev = lambda bi, c, hp, n: (bi, jnp.maximum(n - 1, 0), c * cpb + npair + hp)
    vcur = lambda bi, c, hp, n: (bi, n, c * cpb + 2 * npair + hp)
    vprev = lambda bi, c, hp, n: (bi, jnp.maximum(n - 1, 0), c * cpb + 2 * npair + hp)
    omap = lambda bi, c, hp, n: (bi, n, c * npair + hp)
    blk = (1, ATT_BLOCK, LANES)
    o, lse = pl.pallas_call(
        _attn_kernel,
        grid=(b, dil, npair, nb),
        in_specs=[pl.BlockSpec(blk, qmap), pl.BlockSpec(blk, kprev), pl.BlockSpec(blk, kcur),
                  pl.BlockSpec(blk, vprev), pl.BlockSpec(blk, vcur)],
        out_specs=[pl.BlockSpec(blk, omap), pl.BlockSpec(blk, omap)],
        out_shape=[jax.ShapeDtypeStruct((b, l, dil * a), F32)] * 2,
        compiler_params=_params("arbitrary", "arbitrary", "arbitrary", "arbitrary"),
        name=f"dilated_attn_d{dil}",
    )(view, view, view, view, view)
    return o.reshape(b * s, a), lse.reshape(b * s, a)


def _segment_sum(x, seg_ref):
    return jnp.dot(x, seg_ref[...], preferred_element_type=F32, precision=lax.Precision.HIGHEST)


def _rwkv_prep_kernel(rkv_ref, lora_ref, vfirst_ref, decay_up_ref, aaa_up_ref, gate_up_ref,
                      vres_up_ref, vec_ref, seg_ref,
                      r_ref, w_ref, k_ref, v_ref, a_ref, b_ref, g_ref, bonus_ref, *, has_vres):
    wdt = rkv_ref.shape[1] // 3
    r = rkv_ref[:, 0:wdt]
    k = rkv_ref[:, wdt:2 * wdt]
    v = rkv_ref[:, 2 * wdt:3 * wdt]
    wa = lora_ref[:, 0:LANES]
    gd = lora_ref[:, LANES:2 * LANES]
    decay_base, aaa_base, vres_base = vec_ref[0:1, :], vec_ref[1:2, :], vec_ref[2:3, :]
    k_k, k_a, r_k = vec_ref[3:4, :], vec_ref[4:5, :], vec_ref[5:6, :]

    z = -(decay_base + jnp.dot(jnp.tanh(wa).astype(BF16), decay_up_ref[...],
                               preferred_element_type=F32))
    softplus = jnp.maximum(z, 0.0) + jnp.log(1.0 + jnp.exp(-jnp.abs(z)))
    w_log = -softplus - 0.5
    w_ref[...] = jnp.exp(-jnp.exp(w_log))
    a = jax.nn.sigmoid(aaa_base + jnp.dot(wa.astype(BF16), aaa_up_ref[...],
                                          preferred_element_type=F32))
    g_ref[...] = jnp.dot(jax.nn.sigmoid(gd).astype(BF16), gate_up_ref[...],
                         preferred_element_type=F32)
    if has_vres:
        vd = lora_ref[:, 2 * LANES:3 * LANES]
        mix = jax.nn.sigmoid(vres_base + jnp.dot(vd.astype(BF16), vres_up_ref[...],
                                                 preferred_element_type=F32))
        v = v + (vfirst_ref[...] - v) * mix
    kk = k * k_k
    norm = jnp.sqrt(_segment_sum(kk * kk, seg_ref))
    kk = kk / jnp.maximum(norm, 1e-12)
    k2 = k * (1.0 + (a - 1.0) * k_a)
    r_ref[...] = r
    k_ref[...] = k2
    v_ref[...] = v
    a_ref[...] = -kk
    b_ref[...] = kk * a
    bonus_ref[...] = _segment_sum(r * k2 * r_k, seg_ref) * v


def _rwkv_prep(rkv, lora, vfirst, decay_up, aaa_up, gate_up, vres_up, vecs, seg, has_vres, tm=256):
    m = rkv.shape[0]
    wdt = rkv.shape[1] // 3
    tm = min(tm, m)
    rows = lambda n: pl.BlockSpec((tm, n), lambda i: (i, 0))
    full = lambda arr: pl.BlockSpec(arr.shape, lambda i: (0, 0))
    out = jax.ShapeDtypeStruct((m, wdt), F32)
    return pl.pallas_call(
        functools.partial(_rwkv_prep_kernel, has_vres=has_vres),
        grid=(m // tm,),
        in_specs=[rows(3 * wdt), rows(LORA_PAD), rows(wdt), full(decay_up), full(aaa_up),
                  full(gate_up), full(vres_up), full(vecs), full(seg)],
        out_specs=[rows(wdt)] * 8,
        out_shape=[out] * 8,
        compiler_params=_params("arbitrary"),
        name="rwkv_prep",
    )(rkv, lora, vfirst, decay_up, aaa_up, gate_up, vres_up, vecs, seg)


N_PARTIAL = 4


def _tree_sum(parts):
    while len(parts) > 1:
        parts = [parts[i] + parts[i + 1] for i in range(0, len(parts), 2)]
    return parts[0]


def _rwkv_scan_kernel(a_ref, w_ref, b_ref, k_ref, r_ref, v_ref, y_ref, state_ref):
    @pl.when(pl.program_id(0) == 0)
    def _():
        state_ref[...] = jnp.zeros_like(state_ref)

    steps, nk, _ = a_ref.shape
    vp = v_ref.shape[1]

    def step(t, carry):
        row = lambda ref, kx: jnp.broadcast_to(ref[t, pl.ds(kx, 1), :], (vp, LANES))
        acc = [None] * N_PARTIAL
        for kx in range(nk):
            term = state_ref[kx] * row(a_ref, kx)
            j = kx % N_PARTIAL
            acc[j] = term if acc[j] is None else acc[j] + term
        sa = _tree_sum(acc)
        v_t = v_ref[t]
        yacc = [None] * N_PARTIAL
        for kx in range(nk):
            new = (state_ref[kx] * row(w_ref, kx) + sa * row(b_ref, kx)
                   + v_t * row(k_ref, kx))
            state_ref[kx] = new
            term = new * row(r_ref, kx)
            j = kx % N_PARTIAL
            yacc[j] = term if yacc[j] is None else yacc[j] + term
        y_ref[t] = _tree_sum(yacc)
        return carry

    lax.fori_loop(0, steps, step, 0)


def _rwkv_scan(a_t, w_t, b_t, k_t, r_t, v_t, steps=32):
    s, nk, _ = a_t.shape
    vp = v_t.shape[1]
    steps = min(steps, s)
    kin = pl.BlockSpec((steps, nk, LANES), lambda i: (i, 0, 0))
    vin = pl.BlockSpec((steps, vp, LANES), lambda i: (i, 0, 0))
    return pl.pallas_call(
        _rwkv_scan_kernel,
        grid=(s // steps,),
        in_specs=[kin] * 5 + [vin],
        out_specs=vin,
        out_shape=jax.ShapeDtypeStruct((s, vp, LANES), F32),
        scratch_shapes=[pltpu.VMEM((nk, vp, LANES), F32)],
        compiler_params=_params("arbitrary"),
        name="rwkv_scan",
    )(a_t, w_t, b_t, k_t, r_t, v_t)


def _to_scan_layout(x, b, s, heads, vh):
    x = x.reshape(b, s, heads, RWKV_HEAD_DIM).transpose(1, 3, 0, 2).reshape(s, RWKV_HEAD_DIM, b * heads)
    return jnp.tile(x, (1, 1, vh))


def _value_to_scan_layout(x, b, s, heads, vh):
    vp = RWKV_HEAD_DIM // vh
    x = x.reshape(b, s, heads, vh, vp).transpose(1, 4, 3, 0, 2)
    return x.reshape(s, vp, vh * b * heads)


def _value_from_scan_layout(y, b, s, heads, vh):
    vp = RWKV_HEAD_DIM // vh
    y = y.reshape(s, vp, vh, b, heads).transpose(3, 0, 4, 2, 1)
    return y.reshape(b * s, heads * RWKV_HEAD_DIM)


def _layer_norm(z, g, b):
    mu = jnp.mean(z, axis=-1, keepdims=True)
    zc = z - mu
    var = jnp.mean(zc * zc, axis=-1, keepdims=True)
    return zc * lax.rsqrt(var + LN_EPS) * g + b


def _outproj_kernel(o1_ref, o2_ref, o3_ref, l1_ref, l2_ref, l3_ref, y_ref, bonus_ref, g_ref,
                    x_ref, wa_ref, wr_ref, vec_ref, ln_ref, seg_ref, out_ref, outb_ref, *, alpha):
    l1, l2, l3 = l1_ref[...], l2_ref[...], l3_ref[...]
    m = jnp.maximum(jnp.maximum(l1, l2), l3)
    e1, e2, e3 = jnp.exp(l1 - m), jnp.exp(l2 - m), jnp.exp(l3 - m)
    att = (o1_ref[...] * e1 + o2_ref[...] * e2 + o3_ref[...] * e3) / (e1 + e2 + e3)

    y = y_ref[...]
    inv_n = 1.0 / RWKV_HEAD_DIM
    mu = _segment_sum(y, seg_ref) * inv_n
    yc = y - mu
    var = _segment_sum(yc * yc, seg_ref) * inv_n
    yn = yc * lax.rsqrt(var + GN_EPS) * vec_ref[0:1, :] + vec_ref[1:2, :]
    rw = (yn + bonus_ref[...]) * g_ref[...]

    mix = (jnp.dot(att.astype(BF16), wa_ref[...], preferred_element_type=F32)
           + jnp.dot(rw.astype(BF16), wr_ref[...], preferred_element_type=F32))
    out = _layer_norm(alpha * x_ref[...] + mix, ln_ref[0:1, :], ln_ref[1:2, :])
    out_ref[...] = out
    outb_ref[...] = out.astype(BF16)


def _outproj(o, lse, y, bonus, g, x2, wa, wr, gn_vec, ln_vec, seg, alpha, tm=256):
    m, d = x2.shape
    a = o[0].shape[1]
    tm = min(tm, m)
    rows = lambda n: pl.BlockSpec((tm, n), lambda i: (i, 0))
    full = lambda arr: pl.BlockSpec(arr.shape, lambda i: (0, 0))
    return pl.pallas_call(
        functools.partial(_outproj_kernel, alpha=alpha),
        grid=(m // tm,),
        in_specs=[rows(a)] * 9 + [rows(d), full(wa), full(wr), full(gn_vec), full(ln_vec), full(seg)],
        out_specs=[rows(d), rows(d)],
        out_shape=[jax.ShapeDtypeStruct((m, d), F32), jax.ShapeDtypeStruct((m, d), BF16)],
        compiler_params=_params("arbitrary"),
        name="outproj_ln",
    )(*o, *lse, y, bonus, g, x2, wa, wr, gn_vec, ln_vec, seg)


def _swiglu_kernel(expert_ref, x_ref, wg_ref, wu_ref, wd_ref, out_ref, acc_ref):
    j = pl.program_id(1)

    @pl.when(j == 0)
    def _():
        acc_ref[...] = jnp.zeros_like(acc_ref)

    x = x_ref[...]
    gate = jnp.dot(x, wg_ref[0], preferred_element_type=F32)
    up = jnp.dot(x, wu_ref[0], preferred_element_type=F32)
    h = (gate * jax.nn.sigmoid(gate) * up).astype(BF16)
    acc_ref[...] += jnp.dot(h, wd_ref[0], preferred_element_type=F32)

    @pl.when(j == pl.num_programs(1) - 1)
    def _():
        out_ref[...] = acc_ref[...]


def _swiglu(tile_expert, xb, wg, wu, wd, tm, fc):
    m, d = xb.shape
    f = wg.shape[2]
    return pl.pallas_call(
        _swiglu_kernel,
        grid_spec=pltpu.PrefetchScalarGridSpec(
            num_scalar_prefetch=1,
            grid=(m // tm, f // fc),
            in_specs=[pl.BlockSpec((tm, d), lambda i, j, e: (i, 0)),
                      pl.BlockSpec((1, d, fc), lambda i, j, e: (e[i], 0, j)),
                      pl.BlockSpec((1, d, fc), lambda i, j, e: (e[i], 0, j)),
                      pl.BlockSpec((1, fc, d), lambda i, j, e: (e[i], j, 0))],
            out_specs=pl.BlockSpec((tm, d), lambda i, j, e: (i, 0)),
            scratch_shapes=[pltpu.VMEM((tm, d), F32)]),
        out_shape=jax.ShapeDtypeStruct((m, d), F32),
        compiler_params=_params("arbitrary", "arbitrary"),
        name="swiglu",
    )(tile_expert, xb, wg, wu, wd)


def _ffn_chunk(f):
    for fc in (512, 256, 128):
        if f % fc == 0:
            return fc
    raise ValueError(f"feed-forward width {f} is not a multiple of {LANES}")


def _router_kernel(x_ref, wr_ref, out_ref):
    logits = jnp.dot(x_ref[...], wr_ref[...], preferred_element_type=F32,
                     precision=lax.Precision.HIGHEST)
    lane = lax.broadcasted_iota(jnp.int32, logits.shape, 1)
    logits = jnp.where(lane < N_EXPERTS, logits, NEG_BIG)
    v1 = jnp.max(logits, axis=-1, keepdims=True)
    i1 = jnp.min(jnp.where(logits == v1, lane, LANES), axis=-1, keepdims=True)
    rest = jnp.where(lane == i1, NEG_BIG, logits)
    v2 = jnp.max(rest, axis=-1, keepdims=True)
    i2 = jnp.min(jnp.where(rest == v2, lane, LANES), axis=-1, keepdims=True)
    e2 = jnp.exp(v2 - v1)
    g1 = 1.0 / (1.0 + e2)
    g2 = e2 / (1.0 + e2)
    out_ref[...] = jnp.where(lane == 0, i1.astype(F32),
                             jnp.where(lane == 1, i2.astype(F32),
                                       jnp.where(lane == 2, g1, jnp.where(lane == 3, g2, 0.0))))


def _router(x2, wr_pad, tm=512):
    m, d = x2.shape
    tm = min(tm, m)
    return pl.pallas_call(
        _router_kernel,
        grid=(m // tm,),
        in_specs=[pl.BlockSpec((tm, d), lambda i: (i, 0)), pl.BlockSpec(wr_pad.shape, lambda i: (0, 0))],
        out_specs=pl.BlockSpec((tm, LANES), lambda i: (i, 0)),
        out_shape=jax.ShapeDtypeStruct((m, LANES), F32),
        compiler_params=_params("arbitrary"),
        name="router_top2",
    )(x2, wr_pad)


def _add_ln_kernel(x_ref, f_ref, ln_ref, out_ref, *, alpha):
    out_ref[...] = _layer_norm(alpha * x_ref[...] + f_ref[...], ln_ref[0:1, :], ln_ref[1:2, :])


def _moe_add_ln_kernel(x_ref, y1_ref, y2_ref, route_ref, ln_ref, out_ref, *, alpha):
    g1 = route_ref[:, 2:3]
    g2 = route_ref[:, 3:4]
    f = g1 * y1_ref[...] + g2 * y2_ref[...]
    out_ref[...] = _layer_norm(alpha * x_ref[...] + f, ln_ref[0:1, :], ln_ref[1:2, :])


def _add_ln(x2, f, ln_vec, alpha, tm=512):
    m, d = x2.shape
    tm = min(tm, m)
    rows = pl.BlockSpec((tm, d), lambda i: (i, 0))
    return pl.pallas_call(
        functools.partial(_add_ln_kernel, alpha=alpha),
        grid=(m // tm,),
        in_specs=[rows, rows, pl.BlockSpec(ln_vec.shape, lambda i: (0, 0))],
        out_specs=rows,
        out_shape=jax.ShapeDtypeStruct((m, d), F32),
        compiler_params=_params("arbitrary"),
        name="add_ln",
    )(x2, f, ln_vec)


def _moe_add_ln(x2, y1, y2, route, ln_vec, alpha, tm=512):
    m, d = x2.shape
    tm = min(tm, m)
    rows = pl.BlockSpec((tm, d), lambda i: (i, 0))
    return pl.pallas_call(
        functools.partial(_moe_add_ln_kernel, alpha=alpha),
        grid=(m // tm,),
        in_specs=[rows, rows, rows, pl.BlockSpec((tm, LANES), lambda i: (i, 0)),
                  pl.BlockSpec(ln_vec.shape, lambda i: (0, 0))],
        out_specs=rows,
        out_shape=jax.ShapeDtypeStruct((m, d), F32),
        compiler_params=_params("arbitrary"),
        name="moe_add_ln",
    )(x2, y1, y2, route, ln_vec)


def _moe(x2, xb, router_w, wg, wu, wd, ln_vec, alpha, tm):
    m, d = x2.shape
    n_exp = wg.shape[0]
    wr_pad = jnp.zeros((d, LANES), F32).at[:, :n_exp].set(router_w)
    route = _router(x2, wr_pad)
    idx = route[:, 0:2].astype(jnp.int32).reshape(-1)
    order = jnp.argsort(idx, stable=True)
    counts = jnp.zeros((n_exp,), jnp.int32).at[idx].add(1)
    padded = ((counts + tm - 1) // tm) * tm
    pad_start = jnp.cumsum(padded) - padded
    start = jnp.cumsum(counts) - counts
    sorted_expert = idx[order]
    dest = pad_start[sorted_expert] + (jnp.arange(2 * m, dtype=jnp.int32) - start[sorted_expert])
    rows_total = 2 * m + n_exp * tm
    src_token = jnp.zeros((rows_total,), jnp.int32).at[dest].set(order // 2)
    slot_pos = jnp.zeros((2 * m,), jnp.int32).at[order].set(dest)
    tile_start = jnp.arange(rows_total // tm, dtype=jnp.int32) * tm
    tile_expert = jnp.clip(jnp.searchsorted(pad_start + padded, tile_start, side="right"),
                           0, n_exp - 1).astype(jnp.int32)
    xg = jnp.take(xb, src_token, axis=0)
    yg = _swiglu(tile_expert, xg, wg, wu, wd, tm, _ffn_chunk(wg.shape[2]))
    pos = slot_pos.reshape(m, 2)
    y1 = jnp.take(yg, pos[:, 0], axis=0)
    y2 = jnp.take(yg, pos[:, 1], axis=0)
    return _moe_add_ln(x2, y1, y2, route, ln_vec, alpha)


def kernel(x, w_in, w_in_vres, shift_mu, shift_mu_vres, decay_up, decay_base, aaa_up, aaa_base, vres_up, vres_base, gate_up, k_k, k_a, r_k, lnx_g, lnx_b, w_out, ln1_g, ln1_b, ln2_g, ln2_b, ffn_w_gate, ffn_w_up, ffn_w_down, router, moe_w_gate, moe_w_up, moe_w_down):
    b, s, d = x.shape
    depth = w_in.shape[0]
    wdt = decay_up.shape[2]
    att = w_out.shape[1] - wdt
    heads = wdt // RWKV_HEAD_DIM
    vh = LANES // (b * heads)
    alpha = float((2 * depth) ** 0.25)
    m = b * s
    ffn_tm = min(512, m)

    seg = (jnp.arange(wdt)[:, None] // RWKV_HEAD_DIM == jnp.arange(wdt)[None, :] // RWKV_HEAD_DIM).astype(F32)
    q_scale = jnp.concatenate([jnp.full((att,), ATT_HEAD_DIM ** -0.5, F32), jnp.ones((2 * att,), F32)])
    zero_tiles = jnp.zeros((m // ffn_tm,), jnp.int32)

    x2 = x.reshape(m, d)
    v_first = None
    for l in range(depth):
        has_vres = l > 0
        w_l = w_in[l]
        c0 = 3 * att
        wqkv = (w_l[:, :c0] * q_scale).astype(BF16)
        wrkv = w_l[:, c0:c0 + 3 * wdt].astype(BF16)
        lora_w = w_l[:, c0 + 3 * wdt:]
        lora_mu = shift_mu[l, 3 * wdt:]
        if has_vres:
            lora_w = jnp.concatenate([lora_w, w_in_vres[l - 1]], axis=1)
            lora_mu = jnp.concatenate([lora_mu, shift_mu_vres[l - 1]])
        n_lora = lora_w.shape[1]
        wlora = jnp.pad(lora_w, ((0, 0), (0, LORA_PAD - n_lora))).astype(BF16)
        mu_lora = jnp.pad(lora_mu, (0, LORA_PAD - n_lora)).reshape(1, LORA_PAD)
        mu_rkv = shift_mu[l, :3 * wdt].reshape(1, 3 * wdt)

        qkv, rkv, lora = _inproj(x2, wqkv, wrkv, wlora, mu_rkv, mu_lora, s)

        qkv3 = qkv.reshape(b, s, 3 * att)
        branches = [_attn_pattern(qkv3, dil) for dil in DILATIONS]
        o_parts = [br[0] for br in branches]
        lse_parts = [br[1] for br in branches]

        zpad = lambda w, rows_before: jnp.pad(
            w, ((rows_before, LANES - rows_before - w.shape[0]), (0, 0))).astype(BF16)
        decay_up_p = zpad(decay_up[l], 0)
        aaa_up_p = zpad(aaa_up[l], DECAY_LORA)
        gate_up_p = gate_up[l].astype(BF16)
        if has_vres:
            vres_up_p = zpad(vres_up[l - 1], 0)
            vres_b = vres_base[l - 1]
            vf_in = v_first
        else:
            vres_up_p = jnp.zeros((LANES, wdt), BF16)
            vres_b = jnp.zeros((wdt,), F32)
            vf_in = rkv[:, 2 * wdt:]
        vecs = jnp.stack([decay_base[l], aaa_base[l], vres_b, k_k[l], k_a[l], r_k[l],
                          jnp.zeros((wdt,), F32), jnp.zeros((wdt,), F32)])
        r_, w_, k_, v_, a_, b_, g_, bonus = _rwkv_prep(
            rkv, lora, vf_in, decay_up_p, aaa_up_p, gate_up_p, vres_up_p, vecs, seg, has_vres)
        if not has_vres:
            v_first = v_

        kl = lambda t: _to_scan_layout(t, b, s, heads, vh)
        y_t = _rwkv_scan(kl(a_), kl(w_), kl(b_), kl(k_), kl(r_),
                         _value_to_scan_layout(v_, b, s, heads, vh))
        y = _value_from_scan_layout(y_t, b, s, heads, vh)

        gn_vec = jnp.stack([lnx_g[l], lnx_b[l]] + [jnp.zeros((wdt,), F32)] * 6)
        ln1 = jnp.stack([ln1_g[l], ln1_b[l]] + [jnp.zeros((d,), F32)] * 6)
        ln2 = jnp.stack([ln2_g[l], ln2_b[l]] + [jnp.zeros((d,), F32)] * 6)
        wo = w_out[l].astype(BF16)
        x2, xb = _outproj(o_parts, lse_parts, y, bonus, g_, x2, wo[:att], wo[att:],
                          gn_vec, ln1, seg, alpha)

        i = l // 2
        if l % 2 == 0:
            wg = ffn_w_gate[i].astype(BF16)[None]
            wu = ffn_w_up[i].astype(BF16)[None]
            wd = ffn_w_down[i].astype(BF16)[None]
            f = _swiglu(zero_tiles, xb, wg, wu, wd, ffn_tm, _ffn_chunk(wg.shape[2]))
            x2 = _add_ln(x2, f, ln2, alpha)
        else:
            x2 = _moe(x2, xb, router[i], moe_w_gate[i].astype(BF16), moe_w_up[i].astype(BF16),
                      moe_w_down[i].astype(BF16), ln2, alpha, ffn_tm)
    return x2.reshape(b, s, d)
```

```python
import functools

import jax
import jax.numpy as jnp
from jax import lax
from jax.experimental import pallas as pl
from jax.experimental.pallas import tpu as pltpu

ATT_HEAD_DIM = 64
RWKV_HEAD_DIM = 64
DILATIONS = (1, 4, 16)
ATT_BLOCK = 128
DECAY_LORA = 64
AAA_LORA = 64
GATE_LORA = 128
VRES_LORA = 32
N_EXPERTS = 8
LN_EPS = 1e-5
GN_EPS = 64e-5

LANES = 128
LORA_PAD = 384
VMEM_LIMIT = 56 * 1024 * 1024
NEG_BIG = -1e30

F32 = jnp.float32
BF16 = jnp.bfloat16


def _params(*sem):
    return pltpu.CompilerParams(dimension_semantics=sem, vmem_limit_bytes=VMEM_LIMIT)


def _inproj_kernel(x_ref, wqkv_ref, wrkv_ref, wlora_ref, mu_rkv_ref, mu_lora_ref,
                   qkv_ref, rkv_ref, lora_ref, carry_rkv, carry_lora, *, tiles_per_seq):
    i = pl.program_id(0)
    xb = x_ref[...].astype(BF16)
    qkv_ref[...] = jnp.dot(xb, wqkv_ref[...], preferred_element_type=F32).astype(BF16)

    first = (i % tiles_per_seq) == 0

    def shifted(w_ref, mu_ref, carry_ref, out_ref):
        p = jnp.dot(xb, w_ref[...], preferred_element_type=F32)
        tm = p.shape[0]
        prev_row = jnp.where(first, 0.0, carry_ref[...])
        rolled = pltpu.roll(p, 1, axis=0)
        row = lax.broadcasted_iota(jnp.int32, p.shape, 0)
        prev = jnp.where(row == 0, prev_row, rolled)
        out_ref[...] = p + (prev - p) * mu_ref[...]
        carry_ref[...] = p[tm - 1:tm, :]

    shifted(wrkv_ref, mu_rkv_ref, carry_rkv, rkv_ref)
    shifted(wlora_ref, mu_lora_ref, carry_lora, lora_ref)


def _inproj(x2, wqkv, wrkv, wlora, mu_rkv, mu_lora, seq, tm=512):
    m, d = x2.shape
    nq, nr, nl = wqkv.shape[1], wrkv.shape[1], wlora.shape[1]
    tm = min(tm, seq)
    full = lambda shape: pl.BlockSpec(shape, lambda i: (0, 0))
    return pl.pallas_call(
        functools.partial(_inproj_kernel, tiles_per_seq=seq // tm),
        grid=(m // tm,),
        in_specs=[pl.BlockSpec((tm, d), lambda i: (i, 0)),
                  full((d, nq)), full((d, nr)), full((d, nl)), full((1, nr)), full((1, nl))],
        out_specs=[pl.BlockSpec((tm, nq), lambda i: (i, 0)),
                   pl.BlockSpec((tm, nr), lambda i: (i, 0)),
                   pl.BlockSpec((tm, nl), lambda i: (i, 0))],
        out_shape=[jax.ShapeDtypeStruct((m, nq), BF16),
                   jax.ShapeDtypeStruct((m, nr), F32),
                   jax.ShapeDtypeStruct((m, nl), F32)],
        scratch_shapes=[pltpu.VMEM((1, nr), F32), pltpu.VMEM((1, nl), F32)],
        compiler_params=_params("arbitrary"),
        name="inproj",
    )(x2, wqkv, wrkv, wlora, mu_rkv, mu_lora)


def _attn_kernel(q_ref, k_ref, v_ref, o_ref, lse_ref):
    n_blocks = q_ref.shape[1] // ATT_BLOCK
    win = 2 * ATT_BLOCK
    rel = (lax.broadcasted_iota(jnp.int32, (ATT_BLOCK, win), 0)
           - lax.broadcasted_iota(jnp.int32, (ATT_BLOCK, win), 1))
    lane = lax.broadcasted_iota(jnp.int32, (ATT_BLOCK, LANES), 1)
    nt = (((1,), (1,)), ((), ()))

    def block(n, carry):
        q_start = pl.multiple_of(n * ATT_BLOCK, ATT_BLOCK)
        k_start = pl.multiple_of(jnp.maximum(n - 1, 0) * ATT_BLOCK, ATT_BLOCK)
        q = q_ref[0, pl.ds(q_start, ATT_BLOCK), :]
        kw = k_ref[0, pl.ds(k_start, win), :]
        vw = v_ref[0, pl.ds(k_start, win), :]
        dist = rel + (q_start - k_start)
        ok = (dist >= 0) & (dist <= ATT_BLOCK)
        o_acc = jnp.zeros((ATT_BLOCK, LANES), F32)
        lse_acc = jnp.zeros((ATT_BLOCK, LANES), F32)
        for h in range(LANES // ATT_HEAD_DIM):
            in_head = (lane >= h * ATT_HEAD_DIM) & (lane < (h + 1) * ATT_HEAD_DIM)
            qh = jnp.where(in_head, q, jnp.zeros_like(q))
            sc = jnp.where(ok, lax.dot_general(qh, kw, nt, preferred_element_type=F32), NEG_BIG)
            m = jnp.max(sc, axis=-1, keepdims=True)
            p = jnp.exp(sc - m)
            den = jnp.sum(p, axis=-1, keepdims=True)
            o = jnp.dot(p.astype(BF16), vw, preferred_element_type=F32) / den
            o_acc = jnp.where(in_head, o, o_acc)
            lse_acc = jnp.where(in_head, m + jnp.log(den), lse_acc)
        o_ref[0, pl.ds(q_start, ATT_BLOCK), :] = o_acc
        lse_ref[0, pl.ds(q_start, ATT_BLOCK), :] = lse_acc
        return carry

    lax.fori_loop(0, n_blocks, block, 0, unroll=2)


def _attn_pattern(qkv3, dil):
    b, s, c3 = qkv3.shape
    a = c3 // 3
    npair = a // LANES
    l = s // dil
    view = qkv3.reshape(b, l, dil * c3)
    cpb = c3 // LANES
    qmap = lambda bi, c, hp: (bi, 0, c * cpb + hp)
    kmap = lambda bi, c, hp: (bi, 0, c * cpb + npair + hp)
    vmap = lambda bi, c, hp: (bi, 0, c * cpb + 2 * npair + hp)
    omap = lambda bi, c, hp: (bi, 0, c * npair + hp)
    blk = (1, l, LANES)
    o, lse = pl.pallas_call(
        _attn_kernel,
        grid=(b, dil, npair),
        in_specs=[pl.BlockSpec(blk, qmap), pl.BlockSpec(blk, kmap), pl.BlockSpec(blk, vmap)],
        out_specs=[pl.BlockSpec(blk, omap), pl.BlockSpec(blk, omap)],
        out_shape=[jax.ShapeDtypeStruct((b, l, dil * a), F32)] * 2,
        compiler_params=_params("arbitrary", "arbitrary", "arbitrary"),
        name=f"dilated_attn_d{dil}",
    )(view, view, view)
    return o.reshape(b * s, a), lse.reshape(b * s, a)


def _segment_sum(x, seg_ref):
    return jnp.dot(x, seg_ref[...], preferred_element_type=F32, precision=lax.Precision.HIGHEST)


def _rwkv_prep_kernel(rkv_ref, lora_ref, vfirst_ref, decay_up_ref, aaa_up_ref, gate_up_ref,
                      vres_up_ref, vec_ref, seg_ref,
                      r_ref, w_ref, k_ref, v_ref, a_ref, b_ref, g_ref, bonus_ref, *, has_vres):
    wdt = rkv_ref.shape[1] // 3
    r = rkv_ref[:, 0:wdt]
    k = rkv_ref[:, wdt:2 * wdt]
    v = rkv_ref[:, 2 * wdt:3 * wdt]
    wa = lora_ref[:, 0:LANES]
    gd = lora_ref[:, LANES:2 * LANES]
    decay_base, aaa_base, vres_base = vec_ref[0:1, :], vec_ref[1:2, :], vec_ref[2:3, :]
    k_k, k_a, r_k = vec_ref[3:4, :], vec_ref[4:5, :], vec_ref[5:6, :]

    z = -(decay_base + jnp.dot(jnp.tanh(wa).astype(BF16), decay_up_ref[...],
                               preferred_element_type=F32))
    softplus = jnp.maximum(z, 0.0) + jnp.log(1.0 + jnp.exp(-jnp.abs(z)))
    w_log = -softplus - 0.5
    w_ref[...] = jnp.exp(-jnp.exp(w_log))
    a = jax.nn.sigmoid(aaa_base + jnp.dot(wa.astype(BF16), aaa_up_ref[...],
                                          preferred_element_type=F32))
    g_ref[...] = jnp.dot(jax.nn.sigmoid(gd).astype(BF16), gate_up_ref[...],
                         preferred_element_type=F32)
    if has_vres:
        vd = lora_ref[:, 2 * LANES:3 * LANES]
        mix = jax.nn.sigmoid(vres_base + jnp.dot(vd.astype(BF16), vres_up_ref[...],
                                                 preferred_element_type=F32))
        v = v + (vfirst_ref[...] - v) * mix
    kk = k * k_k
    norm = jnp.sqrt(_segment_sum(kk * kk, seg_ref))
    kk = kk / jnp.maximum(norm, 1e-12)
    k2 = k * (1.0 + (a - 1.0) * k_a)
    r_ref[...] = r
    k_ref[...] = k2
    v_ref[...] = v
    a_ref[...] = -kk
    b_ref[...] = kk * a
    bonus_ref[...] = _segment_sum(r * k2 * r_k, seg_ref) * v


def _rwkv_prep(rkv, lora, vfirst, decay_up, aaa_up, gate_up, vres_up, vecs, seg, has_vres, tm=256):
    m = rkv.shape[0]
    wdt = rkv.shape[1] // 3
    tm = min(tm, m)
    rows = lambda n: pl.BlockSpec((tm, n), lambda i: (i, 0))
    full = lambda arr: pl.BlockSpec(arr.shape, lambda i: (0, 0))
    out = jax.ShapeDtypeStruct((m, wdt), F32)
    return pl.pallas_call(
        functools.partial(_rwkv_prep_kernel, has_vres=has_vres),
        grid=(m // tm,),
        in_specs=[rows(3 * wdt), rows(LORA_PAD), rows(wdt), full(decay_up), full(aaa_up),
                  full(gate_up), full(vres_up), full(vecs), full(seg)],
        out_specs=[rows(wdt)] * 8,
        out_shape=[out] * 8,
        compiler_params=_params("arbitrary"),
        name="rwkv_prep",
    )(rkv, lora, vfirst, decay_up, aaa_up, gate_up, vres_up, vecs, seg)


N_PARTIAL = 4


def _tree_sum(parts):
    while len(parts) > 1:
        parts = [parts[i] + parts[i + 1] for i in range(0, len(parts), 2)]
    return parts[0]


def _rwkv_scan_kernel(a_ref, w_ref, b_ref, k_ref, r_ref, v_ref, y_ref, state_ref):
    @pl.when(pl.program_id(0) == 0)
    def _():
        state_ref[...] = jnp.zeros_like(state_ref)

    steps, nk, _ = a_ref.shape
    vp = v_ref.shape[1]

    def step(t, carry):
        row = lambda ref, kx: jnp.broadcast_to(ref[t, pl.ds(kx, 1), :], (vp, LANES))
        acc = [None] * N_PARTIAL
        for kx in range(nk):
            term = state_ref[kx] * row(a_ref, kx)
            j = kx % N_PARTIAL
            acc[j] = term if acc[j] is None else acc[j] + term
        sa = _tree_sum(acc)
        v_t = v_ref[t]
        yacc = [None] * N_PARTIAL
        for kx in range(nk):
            new = (state_ref[kx] * row(w_ref, kx) + sa * row(b_ref, kx)
                   + v_t * row(k_ref, kx))
            state_ref[kx] = new
            term = new * row(r_ref, kx)
            j = kx % N_PARTIAL
            yacc[j] = term if yacc[j] is None else yacc[j] + term
        y_ref[t] = _tree_sum(yacc)
        return carry

    lax.fori_loop(0, steps, step, 0)


def _rwkv_scan(a_t, w_t, b_t, k_t, r_t, v_t, steps=32):
    s, nk, _ = a_t.shape
    vp = v_t.shape[1]
    steps = min(steps, s)
    kin = pl.BlockSpec((steps, nk, LANES), lambda i: (i, 0, 0))
    vin = pl.BlockSpec((steps, vp, LANES), lambda i: (i, 0, 0))
    return pl.pallas_call(
        _rwkv_scan_kernel,
        grid=(s // steps,),
        in_specs=[kin] * 5 + [vin],
        out_specs=vin,
        out_shape=jax.ShapeDtypeStruct((s, vp, LANES), F32),
        scratch_shapes=[pltpu.VMEM((nk, vp, LANES), F32)],
        compiler_params=_params("arbitrary"),
        name="rwkv_scan",
    )(a_t, w_t, b_t, k_t, r_t, v_t)


def _to_scan_layout(x, b, s, heads, vh):
    x = x.reshape(b, s, heads, RWKV_HEAD_DIM).transpose(1, 3, 0, 2).reshape(s, RWKV_HEAD_DIM, b * heads)
    return jnp.tile(x, (1, 1, vh))


def _value_to_scan_layout(x, b, s, heads, vh):
    vp = RWKV_HEAD_DIM // vh
    x = x.reshape(b, s, heads, vh, vp).transpose(1, 4, 3, 0, 2)
    return x.reshape(s, vp, vh * b * heads)


def _value_from_scan_layout(y, b, s, heads, vh):
    vp = RWKV_HEAD_DIM // vh
    y = y.reshape(s, vp, vh, b, heads).transpose(3, 0, 4, 2, 1)
    return y.reshape(b * s, heads * RWKV_HEAD_DIM)


def _layer_norm(z, g, b):
    mu = jnp.mean(z, axis=-1, keepdims=True)
    zc = z - mu
    var = jnp.mean(zc * zc, axis=-1, keepdims=True)
    return zc * lax.rsqrt(var + LN_EPS) * g + b


def _outproj_kernel(o1_ref, o2_ref, o3_ref, l1_ref, l2_ref, l3_ref, y_ref, bonus_ref, g_ref,
                    x_ref, wa_ref, wr_ref, vec_ref, ln_ref, seg_ref, out_ref, outb_ref, *, alpha):
    l1, l2, l3 = l1_ref[...], l2_ref[...], l3_ref[...]
    m = jnp.maximum(jnp.maximum(l1, l2), l3)
    e1, e2, e3 = jnp.exp(l1 - m), jnp.exp(l2 - m), jnp.exp(l3 - m)
    att = (o1_ref[...] * e1 + o2_ref[...] * e2 + o3_ref[...] * e3) / (e1 + e2 + e3)

    y = y_ref[...]
    inv_n = 1.0 / RWKV_HEAD_DIM
    mu = _segment_sum(y, seg_ref) * inv_n
    yc = y - mu
    var = _segment_sum(yc * yc, seg_ref) * inv_n
    yn = yc * lax.rsqrt(var + GN_EPS) * vec_ref[0:1, :] + vec_ref[1:2, :]
    rw = (yn + bonus_ref[...]) * g_ref[...]

    mix = (jnp.dot(att.astype(BF16), wa_ref[...], preferred_element_type=F32)
           + jnp.dot(rw.astype(BF16), wr_ref[...], preferred_element_type=F32))
    out = _layer_norm(alpha * x_ref[...] + mix, ln_ref[0:1, :], ln_ref[1:2, :])
    out_ref[...] = out
    outb_ref[...] = out.astype(BF16)


def _outproj(o, lse, y, bonus, g, x2, wa, wr, gn_vec, ln_vec, seg, alpha, tm=256):
    m, d = x2.shape
    a = o[0].shape[1]
    tm = min(tm, m)
    rows = lambda n: pl.BlockSpec((tm, n), lambda i: (i, 0))
    full = lambda arr: pl.BlockSpec(arr.shape, lambda i: (0, 0))
    return pl.pallas_call(
        functools.partial(_outproj_kernel, alpha=alpha),
        grid=(m // tm,),
        in_specs=[rows(a)] * 9 + [rows(d), full(wa), full(wr), full(gn_vec), full(ln_vec), full(seg)],
        out_specs=[rows(d), rows(d)],
        out_shape=[jax.ShapeDtypeStruct((m, d), F32), jax.ShapeDtypeStruct((m, d), BF16)],
        compiler_params=_params("arbitrary"),
        name="outproj_ln",
    )(*o, *lse, y, bonus, g, x2, wa, wr, gn_vec, ln_vec, seg)


def _swiglu_kernel(expert_ref, x_ref, wg_ref, wu_ref, wd_ref, out_ref, acc_ref):
    j = pl.program_id(1)

    @pl.when(j == 0)
    def _():
        acc_ref[...] = jnp.zeros_like(acc_ref)

    x = x_ref[...]
    gate = jnp.dot(x, wg_ref[0], preferred_element_type=F32)
    up = jnp.dot(x, wu_ref[0], preferred_element_type=F32)
    h = (gate * jax.nn.sigmoid(gate) * up).astype(BF16)
    acc_ref[...] += jnp.dot(h, wd_ref[0], preferred_element_type=F32)

    @pl.when(j == pl.num_programs(1) - 1)
    def _():
        out_ref[...] = acc_ref[...]


def _swiglu(tile_expert, xb, wg, wu, wd, tm, fc):
    m, d = xb.shape
    f = wg.shape[2]
    return pl.pallas_call(
        _swiglu_kernel,
        grid_spec=pltpu.PrefetchScalarGridSpec(
            num_scalar_prefetch=1,
            grid=(m // tm, f // fc),
            in_specs=[pl.BlockSpec((tm, d), lambda i, j, e: (i, 0)),
                      pl.BlockSpec((1, d, fc), lambda i, j, e: (e[i], 0, j)),
                      pl.BlockSpec((1, d, fc), lambda i, j, e: (e[i], 0, j)),
                      pl.BlockSpec((1, fc, d), lambda i, j, e: (e[i], j, 0))],
            out_specs=pl.BlockSpec((tm, d), lambda i, j, e: (i, 0)),
            scratch_shapes=[pltpu.VMEM((tm, d), F32)]),
        out_shape=jax.ShapeDtypeStruct((m, d), F32),
        compiler_params=_params("arbitrary", "arbitrary"),
        name="swiglu",
    )(tile_expert, xb, wg, wu, wd)


def _ffn_chunk(f):
    for fc in (512, 256, 128):
        if f % fc == 0:
            return fc
    raise ValueError(f"feed-forward width {f} is not a multiple of {LANES}")


def _router_kernel(x_ref, wr_ref, out_ref):
    logits = jnp.dot(x_ref[...], wr_ref[...], preferred_element_type=F32,
                     precision=lax.Precision.HIGHEST)
    lane = lax.broadcasted_iota(jnp.int32, logits.shape, 1)
    logits = jnp.where(lane < N_EXPERTS, logits, NEG_BIG)
    v1 = jnp.max(logits, axis=-1, keepdims=True)
    i1 = jnp.min(jnp.where(logits == v1, lane, LANES), axis=-1, keepdims=True)
    rest = jnp.where(lane == i1, NEG_BIG, logits)
    v2 = jnp.max(rest, axis=-1, keepdims=True)
    i2 = jnp.min(jnp.where(rest == v2, lane, LANES), axis=-1, keepdims=True)
    e2 = jnp.exp(v2 - v1)
    g1 = 1.0 / (1.0 + e2)
    g2 = e2 / (1.0 + e2)
    out_ref[...] = jnp.where(lane == 0, i1.astype(F32),
                             jnp.where(lane == 1, i2.astype(F32),
                                       jnp.where(lane == 2, g1, jnp.where(lane == 3, g2, 0.0))))


def _router(x2, wr_pad, tm=512):
    m, d = x2.shape
    tm = min(tm, m)
    return pl.pallas_call(
        _router_kernel,
        grid=(m // tm,),
        in_specs=[pl.BlockSpec((tm, d), lambda i: (i, 0)), pl.BlockSpec(wr_pad.shape, lambda i: (0, 0))],
        out_specs=pl.BlockSpec((tm, LANES), lambda i: (i, 0)),
        out_shape=jax.ShapeDtypeStruct((m, LANES), F32),
        compiler_params=_params("arbitrary"),
        name="router_top2",
    )(x2, wr_pad)


def _add_ln_kernel(x_ref, f_ref, ln_ref, out_ref, *, alpha):
    out_ref[...] = _layer_norm(alpha * x_ref[...] + f_ref[...], ln_ref[0:1, :], ln_ref[1:2, :])


def _moe_add_ln_kernel(x_ref, y1_ref, y2_ref, route_ref, ln_ref, out_ref, *, alpha):
    g1 = route_ref[:, 2:3]
    g2 = route_ref[:, 3:4]
    f = g1 * y1_ref[...] + g2 * y2_ref[...]
    out_ref[...] = _layer_norm(alpha * x_ref[...] + f, ln_ref[0:1, :], ln_ref[1:2, :])


def _add_ln(x2, f, ln_vec, alpha, tm=512):
    m, d = x2.shape
    tm = min(tm, m)
    rows = pl.BlockSpec((tm, d), lambda i: (i, 0))
    return pl.pallas_call(
        functools.partial(_add_ln_kernel, alpha=alpha),
        grid=(m // tm,),
        in_specs=[rows, rows, pl.BlockSpec(ln_vec.shape, lambda i: (0, 0))],
        out_specs=rows,
        out_shape=jax.ShapeDtypeStruct((m, d), F32),
        compiler_params=_params("arbitrary"),
        name="add_ln",
    )(x2, f, ln_vec)


def _moe_add_ln(x2, y1, y2, route, ln_vec, alpha, tm=512):
    m, d = x2.shape
    tm = min(tm, m)
    rows = pl.BlockSpec((tm, d), lambda i: (i, 0))
    return pl.pallas_call(
        functools.partial(_moe_add_ln_kernel, alpha=alpha),
        grid=(m // tm,),
        in_specs=[rows, rows, rows, pl.BlockSpec((tm, LANES), lambda i: (i, 0)),
                  pl.BlockSpec(ln_vec.shape, lambda i: (0, 0))],
        out_specs=rows,
        out_shape=jax.ShapeDtypeStruct((m, d), F32),
        compiler_params=_params("arbitrary"),
        name="moe_add_ln",
    )(x2, y1, y2, route, ln_vec)


def _moe(x2, xb, router_w, wg, wu, wd, ln_vec, alpha, tm):
    m, d = x2.shape
    n_exp = wg.shape[0]
    wr_pad = jnp.zeros((d, LANES), F32).at[:, :n_exp].set(router_w)
    route = _router(x2, wr_pad)
    idx = route[:, 0:2].astype(jnp.int32).reshape(-1)
    order = jnp.argsort(idx, stable=True)
    counts = jnp.zeros((n_exp,), jnp.int32).at[idx].add(1)
    padded = ((counts + tm - 1) // tm) * tm
    pad_start = jnp.cumsum(padded) - padded
    start = jnp.cumsum(counts) - counts
    sorted_expert = idx[order]
    dest = pad_start[sorted_expert] + (jnp.arange(2 * m, dtype=jnp.int32) - start[sorted_expert])
    rows_total = 2 * m + n_exp * tm
    src_token = jnp.zeros((rows_total,), jnp.int32).at[dest].set(order // 2)
    slot_pos = jnp.zeros((2 * m,), jnp.int32).at[order].set(dest)
    tile_start = jnp.arange(rows_total // tm, dtype=jnp.int32) * tm
    tile_expert = jnp.clip(jnp.searchsorted(pad_start + padded, tile_start, side="right"),
                           0, n_exp - 1).astype(jnp.int32)
    xg = jnp.take(xb, src_token, axis=0)
    yg = _swiglu(tile_expert, xg, wg, wu, wd, tm, _ffn_chunk(wg.shape[2]))
    pos = slot_pos.reshape(m, 2)
    y1 = jnp.take(yg, pos[:, 0], axis=0)
    y2 = jnp.take(yg, pos[:, 1], axis=0)
    return _moe_add_ln(x2, y1, y2, route, ln_vec, alpha)


def kernel(x, w_in, w_in_vres, shift_mu, shift_mu_vres, decay_up, decay_base, aaa_up, aaa_base, vres_up, vres_base, gate_up, k_k, k_a, r_k, lnx_g, lnx_b, w_out, ln1_g, ln1_b, ln2_g, ln2_b, ffn_w_gate, ffn_w_up, ffn_w_down, router, moe_w_gate, moe_w_up, moe_w_down):
    b, s, d = x.shape
    depth = w_in.shape[0]
    wdt = decay_up.shape[2]
    att = w_out.shape[1] - wdt
    heads = wdt // RWKV_HEAD_DIM
    vh = LANES // (b * heads)
    alpha = float((2 * depth) ** 0.25)
    m = b * s
    ffn_tm = min(512, m)

    seg = (jnp.arange(wdt)[:, None] // RWKV_HEAD_DIM == jnp.arange(wdt)[None, :] // RWKV_HEAD_DIM).astype(F32)
    q_scale = jnp.concatenate([jnp.full((att,), ATT_HEAD_DIM ** -0.5, F32), jnp.ones((2 * att,), F32)])
    zero_tiles = jnp.zeros((m // ffn_tm,), jnp.int32)

    x2 = x.reshape(m, d)
    v_first = None
    for l in range(depth):
        has_vres = l > 0
        w_l = w_in[l]
        c0 = 3 * att
        wqkv = (w_l[:, :c0] * q_scale).astype(BF16)
        wrkv = w_l[:, c0:c0 + 3 * wdt].astype(BF16)
        lora_w = w_l[:, c0 + 3 * wdt:]
        lora_mu = shift_mu[l, 3 * wdt:]
        if has_vres:
            lora_w = jnp.concatenate([lora_w, w_in_vres[l - 1]], axis=1)
            lora_mu = jnp.concatenate([lora_mu, shift_mu_vres[l - 1]])
        n_lora = lora_w.shape[1]
        wlora = jnp.pad(lora_w, ((0, 0), (0, LORA_PAD - n_lora))).astype(BF16)
        mu_lora = jnp.pad(lora_mu, (0, LORA_PAD - n_lora)).reshape(1, LORA_PAD)
        mu_rkv = shift_mu[l, :3 * wdt].reshape(1, 3 * wdt)

        qkv, rkv, lora = _inproj(x2, wqkv, wrkv, wlora, mu_rkv, mu_lora, s)

        qkv3 = qkv.reshape(b, s, 3 * att)
        branches = [_attn_pattern(qkv3, dil) for dil in DILATIONS]
        o_parts = [br[0] for br in branches]
        lse_parts = [br[1] for br in branches]

        zpad = lambda w, rows_before: jnp.pad(
            w, ((rows_before, LANES - rows_before - w.shape[0]), (0, 0))).astype(BF16)
        decay_up_p = zpad(decay_up[l], 0)
        aaa_up_p = zpad(aaa_up[l], DECAY_LORA)
        gate_up_p = gate_up[l].astype(BF16)
        if has_vres:
            vres_up_p = zpad(vres_up[l - 1], 0)
            vres_b = vres_base[l - 1]
            vf_in = v_first
        else:
            vres_up_p = jnp.zeros((LANES, wdt), BF16)
            vres_b = jnp.zeros((wdt,), F32)
            vf_in = rkv[:, 2 * wdt:]
        vecs = jnp.stack([decay_base[l], aaa_base[l], vres_b, k_k[l], k_a[l], r_k[l],
                          jnp.zeros((wdt,), F32), jnp.zeros((wdt,), F32)])
        r_, w_, k_, v_, a_, b_, g_, bonus = _rwkv_prep(
            rkv, lora, vf_in, decay_up_p, aaa_up_p, gate_up_p, vres_up_p, vecs, seg, has_vres)
        if not has_vres:
            v_first = v_

        kl = lambda t: _to_scan_layout(t, b, s, heads, vh)
        y_t = _rwkv_scan(kl(a_), kl(w_), kl(b_), kl(k_), kl(r_),
                         _value_to_scan_layout(v_, b, s, heads, vh))
        y = _value_from_scan_layout(y_t, b, s, heads, vh)

        gn_vec = jnp.stack([lnx_g[l], lnx_b[l]] + [jnp.zeros((wdt,), F32)] * 6)
        ln1 = jnp.stack([ln1_g[l], ln1_b[l]] + [jnp.zeros((d,), F32)] * 6)
        ln2 = jnp.stack([ln2_g[l], ln2_b[l]] + [jnp.zeros((d,), F32)] * 6)
        wo = w_out[l].astype(BF16)
        x2, xb = _outproj(o_parts, lse_parts, y, bonus, g_, x2, wo[:att], wo[att:],
                          gn_vec, ln1, seg, alpha)

        i = l // 2
        if l % 2 == 0:
            wg = ffn_w_gate[i].astype(BF16)[None]
            wu = ffn_w_up[i].astype(BF16)[None]
            wd = ffn_w_down[i].astype(BF16)[None]
            f = _swiglu(zero_tiles, xb, wg, wu, wd, ffn_tm, _ffn_chunk(wg.shape[2]))
            x2 = _add_ln(x2, f, ln2, alpha)
        else:
            x2 = _moe(x2, xb, router[i], moe_w_gate[i].astype(BF16), moe_w_up[i].astype(BF16),
                      moe_w_down[i].astype(BF16), ln2, alpha, ffn_tm)
    return x2.reshape(b, s, d)
```

```python
import functools

import jax
import jax.numpy as jnp
from jax import lax
from jax.experimental import pallas as pl
from jax.experimental.pallas import tpu as pltpu

ATT_HEAD_DIM = 64
RWKV_HEAD_DIM = 64
DILATIONS = (1, 4, 16)
ATT_BLOCK = 128
DECAY_LORA = 64
AAA_LORA = 64
GATE_LORA = 128
VRES_LORA = 32
N_EXPERTS = 8
LN_EPS = 1e-5
GN_EPS = 64e-5

LANES = 128
LORA_PAD = 384
VMEM_LIMIT = 56 * 1024 * 1024
NEG_BIG = -1e30

F32 = jnp.float32
BF16 = jnp.bfloat16


def _params(*sem):
    return pltpu.CompilerParams(dimension_semantics=sem, vmem_limit_bytes=VMEM_LIMIT)


def _inproj_kernel(x_ref, wqkv_ref, wrkv_ref, wlora_ref, mu_rkv_ref, mu_lora_ref,
                   qkv_ref, rkv_ref, lora_ref, carry_rkv, carry_lora, *, batch):
    xb = x_ref[...].astype(BF16)
    qkv_ref[...] = jnp.dot(xb, wqkv_ref[...], preferred_element_type=F32).astype(BF16)

    first = pl.program_id(0) == 0

    def shifted(w_ref, mu_ref, carry_ref, out_ref):
        p = jnp.dot(xb, w_ref[...], preferred_element_type=F32)
        tm = p.shape[0]
        mu = mu_ref[...]
        out_ref[...] = p + (pltpu.roll(p, batch, axis=0) - p) * mu
        head = p[0:batch, :]
        prev_head = jnp.where(first, 0.0, carry_ref[...])
        out_ref[0:batch, :] = head + (prev_head - head) * mu
        carry_ref[...] = p[tm - batch:tm, :]

    shifted(wrkv_ref, mu_rkv_ref, carry_rkv, rkv_ref)
    shifted(wlora_ref, mu_lora_ref, carry_lora, lora_ref)


def _inproj(x2, wqkv, wrkv, wlora, mu_rkv, mu_lora, batch, tm=512):
    m, d = x2.shape
    nq, nr, nl = wqkv.shape[1], wrkv.shape[1], wlora.shape[1]
    tm = min(tm, m)
    full = lambda shape: pl.BlockSpec(shape, lambda i: (0, 0))
    return pl.pallas_call(
        functools.partial(_inproj_kernel, batch=batch),
        grid=(m // tm,),
        in_specs=[pl.BlockSpec((tm, d), lambda i: (i, 0)),
                  full((d, nq)), full((d, nr)), full((d, nl)), full((1, nr)), full((1, nl))],
        out_specs=[pl.BlockSpec((tm, nq), lambda i: (i, 0)),
                   pl.BlockSpec((tm, nr), lambda i: (i, 0)),
                   pl.BlockSpec((tm, nl), lambda i: (i, 0))],
        out_shape=[jax.ShapeDtypeStruct((m, nq), BF16),
                   jax.ShapeDtypeStruct((m, nr), F32),
                   jax.ShapeDtypeStruct((m, nl), F32)],
        scratch_shapes=[pltpu.VMEM((batch, nr), F32), pltpu.VMEM((batch, nl), F32)],
        compiler_params=_params("arbitrary"),
        name="inproj",
    )(x2, wqkv, wrkv, wlora, mu_rkv, mu_lora)


def _attn_kernel(q_ref, k_ref, v_ref, o_ref, lse_ref):
    n_blocks = q_ref.shape[0] // ATT_BLOCK
    win = 2 * ATT_BLOCK
    rel = (lax.broadcasted_iota(jnp.int32, (ATT_BLOCK, win), 0)
           - lax.broadcasted_iota(jnp.int32, (ATT_BLOCK, win), 1))
    lane = lax.broadcasted_iota(jnp.int32, (ATT_BLOCK, LANES), 1)
    nt = (((1,), (1,)), ((), ()))

    def block(n, carry):
        q_start = pl.multiple_of(n * ATT_BLOCK, ATT_BLOCK)
        k_start = pl.multiple_of(jnp.maximum(n - 1, 0) * ATT_BLOCK, ATT_BLOCK)
        q = q_ref[pl.ds(q_start, ATT_BLOCK), :]
        kw = k_ref[pl.ds(k_start, win), :]
        vw = v_ref[pl.ds(k_start, win), :]
        dist = rel + (q_start - k_start)
        ok = (dist >= 0) & (dist <= ATT_BLOCK)
        o_acc = jnp.zeros((ATT_BLOCK, LANES), F32)
        lse_acc = jnp.zeros((ATT_BLOCK, LANES), F32)
        for h in range(LANES // ATT_HEAD_DIM):
            in_head = (lane >= h * ATT_HEAD_DIM) & (lane < (h + 1) * ATT_HEAD_DIM)
            qh = jnp.where(in_head, q, jnp.zeros_like(q))
            sc = jnp.where(ok, lax.dot_general(qh, kw, nt, preferred_element_type=F32), NEG_BIG)
            m = jnp.max(sc, axis=-1, keepdims=True)
            p = jnp.exp(sc - m)
            den = jnp.sum(p, axis=-1, keepdims=True)
            o = jnp.dot(p.astype(BF16), vw, preferred_element_type=F32) / den
            o_acc = jnp.where(in_head, o, o_acc)
            lse_acc = jnp.where(in_head, m + jnp.log(den), lse_acc)
        o_ref[pl.ds(q_start, ATT_BLOCK), :] = o_acc
        lse_ref[pl.ds(q_start, ATT_BLOCK), :] = lse_acc
        return carry

    lax.fori_loop(0, n_blocks, block, 0, unroll=2)


def _attn_pattern(qkv, batch, dil):
    m, c3 = qkv.shape
    a = c3 // 3
    npair = a // LANES
    l = m // (batch * dil)
    view = qkv.reshape(l, dil * batch * c3)
    cpb = c3 // LANES
    qmap = lambda bi, c, hp: (0, (c * batch + bi) * cpb + hp)
    kmap = lambda bi, c, hp: (0, (c * batch + bi) * cpb + npair + hp)
    vmap = lambda bi, c, hp: (0, (c * batch + bi) * cpb + 2 * npair + hp)
    omap = lambda bi, c, hp: (0, (c * batch + bi) * npair + hp)
    blk = (l, LANES)
    o, lse = pl.pallas_call(
        _attn_kernel,
        grid=(batch, dil, npair),
        in_specs=[pl.BlockSpec(blk, qmap), pl.BlockSpec(blk, kmap), pl.BlockSpec(blk, vmap)],
        out_specs=[pl.BlockSpec(blk, omap), pl.BlockSpec(blk, omap)],
        out_shape=[jax.ShapeDtypeStruct((l, dil * batch * a), F32)] * 2,
        compiler_params=_params("arbitrary", "arbitrary", "arbitrary"),
        name=f"dilated_attn_d{dil}",
    )(view, view, view)
    return o.reshape(m, a), lse.reshape(m, a)


def _segment_sum(x, seg_ref):
    return jnp.dot(x, seg_ref[...], preferred_element_type=F32, precision=lax.Precision.HIGHEST)


KEY_HALVES = 2


def _chunk_transpose(parts, chunk):
    n = len(parts)
    lane = lax.broadcasted_iota(jnp.int32, parts[0].shape, 1)
    s = n // 2
    while s >= 1:
        swap = ((lane // chunk) & s) != 0
        new = list(parts)
        for j in range(n):
            if j & s == 0:
                lo, hi = parts[j], parts[j + s]
                new[j] = jnp.where(swap, pltpu.roll(hi, s * chunk, axis=1), lo)
                new[j + s] = jnp.where(swap, hi, pltpu.roll(lo, LANES - s * chunk, axis=1))
        parts = new
        s //= 2
    return parts


def _keys_to_lanes(p, heads, batch):
    q = p.T
    half = RWKV_HEAD_DIM // KEY_HALVES
    parts = [q[(h * KEY_HALVES + kh) * half:(h * KEY_HALVES + kh + 1) * half, :]
             for kh in range(KEY_HALVES) for h in range(heads)]
    return _chunk_transpose(parts, batch)


def _values_to_lanes(p, heads, batch):
    q = p.T
    parts = [q[h * RWKV_HEAD_DIM:(h + 1) * RWKV_HEAD_DIM, :]
             for _ in range(KEY_HALVES) for h in range(heads)]
    return _chunk_transpose(parts, batch)


def _values_from_lanes(parts, heads, batch):
    per_head = _chunk_transpose(parts, batch)[:heads]
    return jnp.concatenate(per_head, axis=0).T


def _rwkv_prep_kernel(rkv_ref, lora_ref, vfirst_ref, decay_up_ref, aaa_up_ref, gate_up_ref,
                      vres_up_ref, vec_ref, seg_ref,
                      r_ref, w_ref, k_ref, a_ref, b_ref, vl_ref, v_ref, g_ref, bonus_ref,
                      *, has_vres, batch):
    wdt = rkv_ref.shape[1] // 3
    r = rkv_ref[:, 0:wdt]
    k = rkv_ref[:, wdt:2 * wdt]
    v = rkv_ref[:, 2 * wdt:3 * wdt]
    wa = lora_ref[:, 0:LANES]
    gd = lora_ref[:, LANES:2 * LANES]
    decay_base, aaa_base, vres_base = vec_ref[0:1, :], vec_ref[1:2, :], vec_ref[2:3, :]
    k_k, k_a, r_k = vec_ref[3:4, :], vec_ref[4:5, :], vec_ref[5:6, :]

    z = -(decay_base + jnp.dot(jnp.tanh(wa).astype(BF16), decay_up_ref[...],
                               preferred_element_type=F32))
    softplus = jnp.maximum(z, 0.0) + jnp.log(1.0 + jnp.exp(-jnp.abs(z)))
    w_log = -softplus - 0.5
    w = jnp.exp(-jnp.exp(w_log))
    a = jax.nn.sigmoid(aaa_base + jnp.dot(wa.astype(BF16), aaa_up_ref[...],
                                          preferred_element_type=F32))
    g_ref[...] = jnp.dot(jax.nn.sigmoid(gd).astype(BF16), gate_up_ref[...],
                         preferred_element_type=F32)
    if has_vres:
        vd = lora_ref[:, 2 * LANES:3 * LANES]
        mix = jax.nn.sigmoid(vres_base + jnp.dot(vd.astype(BF16), vres_up_ref[...],
                                                 preferred_element_type=F32))
        v = v + (vfirst_ref[...] - v) * mix
    kk = k * k_k
    norm = jnp.sqrt(_segment_sum(kk * kk, seg_ref))
    kk = kk / jnp.maximum(norm, 1e-12)
    k2 = k * (1.0 + (a - 1.0) * k_a)
    v_ref[...] = v
    bonus_ref[...] = _segment_sum(r * k2 * r_k, seg_ref) * v

    heads = wdt // RWKV_HEAD_DIM
    group = LANES
    steps = group // batch
    for gi in range(rkv_ref.shape[0] // group):
        rows = slice(gi * group, (gi + 1) * group)
        for ref, val in ((r_ref, r), (w_ref, w), (k_ref, k2), (a_ref, -kk), (b_ref, kk * a)):
            for t, part in enumerate(_keys_to_lanes(val[rows], heads, batch)):
                ref[gi * steps + t] = part
        for t, part in enumerate(_values_to_lanes(v[rows], heads, batch)):
            vl_ref[gi * steps + t] = part


def _rwkv_prep(rkv, lora, vfirst, decay_up, aaa_up, gate_up, vres_up, vecs, seg, has_vres, batch, tm=256):
    m = rkv.shape[0]
    wdt = rkv.shape[1] // 3
    heads = wdt // RWKV_HEAD_DIM
    assert KEY_HALVES * heads * batch == LANES, "recurrence layout needs 2*heads*batch == 128 lanes"
    tm = min(tm, m)
    assert tm % LANES == 0
    steps = tm // batch
    seq = m // batch
    half = RWKV_HEAD_DIM // KEY_HALVES
    rows = lambda n: pl.BlockSpec((tm, n), lambda i: (i, 0))
    full = lambda arr: pl.BlockSpec(arr.shape, lambda i: (0, 0))
    key_spec = pl.BlockSpec((steps, half, LANES), lambda i: (i, 0, 0))
    val_spec = pl.BlockSpec((steps, RWKV_HEAD_DIM, LANES), lambda i: (i, 0, 0))
    key_out = jax.ShapeDtypeStruct((seq, half, LANES), F32)
    val_out = jax.ShapeDtypeStruct((seq, RWKV_HEAD_DIM, LANES), F32)
    nat_out = jax.ShapeDtypeStruct((m, wdt), F32)
    return pl.pallas_call(
        functools.partial(_rwkv_prep_kernel, has_vres=has_vres, batch=batch),
        grid=(m // tm,),
        in_specs=[rows(3 * wdt), rows(LORA_PAD), rows(wdt), full(decay_up), full(aaa_up),
                  full(gate_up), full(vres_up), full(vecs), full(seg)],
        out_specs=[key_spec] * 5 + [val_spec] + [rows(wdt)] * 3,
        out_shape=[key_out] * 5 + [val_out] + [nat_out] * 3,
        compiler_params=_params("arbitrary"),
        name="rwkv_prep",
    )(rkv, lora, vfirst, decay_up, aaa_up, gate_up, vres_up, vecs, seg)


N_PARTIAL = 2


def _tree_sum(parts):
    while len(parts) > 1:
        parts = [parts[i] + parts[i + 1] for i in range(0, len(parts), 2)]
    return parts[0]


def _rwkv_scan_kernel(a_ref, w_ref, b_ref, k_ref, r_ref, v_ref, y_ref, state_ref):
    @pl.when(pl.program_id(0) == 0)
    def _():
        state_ref[...] = jnp.zeros_like(state_ref)

    steps, nk, _ = a_ref.shape
    nv = v_ref.shape[1]
    both_halves = lambda x: x + pltpu.roll(x, LANES // KEY_HALVES, axis=1)

    def step(t, carry):
        row = lambda ref, kx: jnp.broadcast_to(ref[t, pl.ds(kx, 1), :], (nv, LANES))
        acc = [None] * N_PARTIAL
        for kx in range(nk):
            term = state_ref[kx] * row(a_ref, kx)
            j = kx % N_PARTIAL
            acc[j] = term if acc[j] is None else acc[j] + term
        sa = both_halves(_tree_sum(acc))
        v_t = v_ref[t]
        yacc = [None] * N_PARTIAL
        for kx in range(nk):
            new = (state_ref[kx] * row(w_ref, kx) + sa * row(b_ref, kx)
                   + v_t * row(k_ref, kx))
            state_ref[kx] = new
            term = new * row(r_ref, kx)
            j = kx % N_PARTIAL
            yacc[j] = term if yacc[j] is None else yacc[j] + term
        y_ref[t] = both_halves(_tree_sum(yacc))
        return carry

    lax.fori_loop(0, steps, step, 0)


def _rwkv_scan(a_t, w_t, b_t, k_t, r_t, v_t, steps=32):
    s, nk, _ = a_t.shape
    nv = v_t.shape[1]
    steps = min(steps, s)
    kin = pl.BlockSpec((steps, nk, LANES), lambda i: (i, 0, 0))
    vin = pl.BlockSpec((steps, nv, LANES), lambda i: (i, 0, 0))
    return pl.pallas_call(
        _rwkv_scan_kernel,
        grid=(s // steps,),
        in_specs=[kin] * 5 + [vin],
        out_specs=vin,
        out_shape=jax.ShapeDtypeStruct((s, nv, LANES), F32),
        scratch_shapes=[pltpu.VMEM((nk, nv, LANES), F32)],
        compiler_params=_params("arbitrary"),
        name="rwkv_scan",
    )(a_t, w_t, b_t, k_t, r_t, v_t)


def _layer_norm(z, g, b):
    mu = jnp.mean(z, axis=-1, keepdims=True)
    zc = z - mu
    var = jnp.mean(zc * zc, axis=-1, keepdims=True)
    return zc * lax.rsqrt(var + LN_EPS) * g + b


def _outproj_kernel(o1_ref, o2_ref, o3_ref, l1_ref, l2_ref, l3_ref, y_ref, bonus_ref, g_ref,
                    x_ref, wa_ref, wr_ref, vec_ref, ln_ref, seg_ref, out_ref, outb_ref,
                    *, alpha, batch):
    l1, l2, l3 = l1_ref[...], l2_ref[...], l3_ref[...]
    m = jnp.maximum(jnp.maximum(l1, l2), l3)
    e1, e2, e3 = jnp.exp(l1 - m), jnp.exp(l2 - m), jnp.exp(l3 - m)
    att = (o1_ref[...] * e1 + o2_ref[...] * e2 + o3_ref[...] * e3) / (e1 + e2 + e3)

    heads = bonus_ref.shape[1] // RWKV_HEAD_DIM
    steps = LANES // batch
    y = jnp.concatenate(
        [_values_from_lanes([y_ref[gi * steps + t] for t in range(steps)], heads, batch)
         for gi in range(y_ref.shape[0] // steps)], axis=0)
    inv_n = 1.0 / RWKV_HEAD_DIM
    mu = _segment_sum(y, seg_ref) * inv_n
    yc = y - mu
    var = _segment_sum(yc * yc, seg_ref) * inv_n
    yn = yc * lax.rsqrt(var + GN_EPS) * vec_ref[0:1, :] + vec_ref[1:2, :]
    rw = (yn + bonus_ref[...]) * g_ref[...]

    mix = (jnp.dot(att.astype(BF16), wa_ref[...], preferred_element_type=F32)
           + jnp.dot(rw.astype(BF16), wr_ref[...], preferred_element_type=F32))
    out = _layer_norm(alpha * x_ref[...] + mix, ln_ref[0:1, :], ln_ref[1:2, :])
    out_ref[...] = out
    outb_ref[...] = out.astype(BF16)


def _outproj(o, lse, y_lanes, bonus, g, x2, wa, wr, gn_vec, ln_vec, seg, alpha, batch, tm=256):
    m, d = x2.shape
    a = o[0].shape[1]
    tm = min(tm, m)
    assert tm % LANES == 0
    rows = lambda n: pl.BlockSpec((tm, n), lambda i: (i, 0))
    full = lambda arr: pl.BlockSpec(arr.shape, lambda i: (0, 0))
    y_spec = pl.BlockSpec((tm // batch,) + y_lanes.shape[1:], lambda i: (i, 0, 0))
    return pl.pallas_call(
        functools.partial(_outproj_kernel, alpha=alpha, batch=batch),
        grid=(m // tm,),
        in_specs=[rows(a)] * 6 + [y_spec] + [rows(a)] * 2
                 + [rows(d), full(wa), full(wr), full(gn_vec), full(ln_vec), full(seg)],
        out_specs=[rows(d), rows(d)],
        out_shape=[jax.ShapeDtypeStruct((m, d), F32), jax.ShapeDtypeStruct((m, d), BF16)],
        compiler_params=_params("arbitrary"),
        name="outproj_ln",
    )(*o, *lse, y_lanes, bonus, g, x2, wa, wr, gn_vec, ln_vec, seg)


def _swiglu_kernel(expert_ref, x_ref, wg_ref, wu_ref, wd_ref, out_ref, acc_ref):
    j = pl.program_id(1)

    @pl.when(j == 0)
    def _():
        acc_ref[...] = jnp.zeros_like(acc_ref)

    x = x_ref[...]
    gate = jnp.dot(x, wg_ref[0], preferred_element_type=F32)
    up = jnp.dot(x, wu_ref[0], preferred_element_type=F32)
    h = (gate * jax.nn.sigmoid(gate) * up).astype(BF16)
    acc_ref[...] += jnp.dot(h, wd_ref[0], preferred_element_type=F32)

    @pl.when(j == pl.num_programs(1) - 1)
    def _():
        out_ref[...] = acc_ref[...]


def _swiglu(tile_expert, xb, wg, wu, wd, tm, fc):
    m, d = xb.shape
    f = wg.shape[2]
    return pl.pallas_call(
        _swiglu_kernel,
        grid_spec=pltpu.PrefetchScalarGridSpec(
            num_scalar_prefetch=1,
            grid=(m // tm, f // fc),
            in_specs=[pl.BlockSpec((tm, d), lambda i, j, e: (i, 0)),
                      pl.BlockSpec((1, d, fc), lambda i, j, e: (e[i], 0, j)),
                      pl.BlockSpec((1, d, fc), lambda i, j, e: (e[i], 0, j)),
                      pl.BlockSpec((1, fc, d), lambda i, j, e: (e[i], j, 0))],
            out_specs=pl.BlockSpec((tm, d), lambda i, j, e: (i, 0)),
            scratch_shapes=[pltpu.VMEM((tm, d), F32)]),
        out_shape=jax.ShapeDtypeStruct((m, d), F32),
        compiler_params=_params("arbitrary", "arbitrary"),
        name="swiglu",
    )(tile_expert, xb, wg, wu, wd)


def _ffn_chunk(f):
    for fc in (512, 256, 128):
        if f % fc == 0:
            return fc
    raise ValueError(f"feed-forward width {f} is not a multiple of {LANES}")


def _router_kernel(x_ref, wr_ref, out_ref):
    logits = jnp.dot(x_ref[...], wr_ref[...], preferred_element_type=F32,
                     precision=lax.Precision.HIGHEST)
    lane = lax.broadcasted_iota(jnp.int32, logits.shape, 1)
    logits = jnp.where(lane < N_EXPERTS, logits, NEG_BIG)
    v1 = jnp.max(logits, axis=-1, keepdims=True)
    i1 = jnp.min(jnp.where(logits == v1, lane, LANES), axis=-1, keepdims=True)
    rest = jnp.where(lane == i1, NEG_BIG, logits)
    v2 = jnp.max(rest, axis=-1, keepdims=True)
    i2 = jnp.min(jnp.where(rest == v2, lane, LANES), axis=-1, keepdims=True)
    e2 = jnp.exp(v2 - v1)
    g1 = 1.0 / (1.0 + e2)
    g2 = e2 / (1.0 + e2)
    out_ref[...] = jnp.where(lane == 0, i1.astype(F32),
                             jnp.where(lane == 1, i2.astype(F32),
                                       jnp.where(lane == 2, g1, jnp.where(lane == 3, g2, 0.0))))


def _router(x2, wr_pad, tm=512):
    m, d = x2.shape
    tm = min(tm, m)
    return pl.pallas_call(
        _router_kernel,
        grid=(m // tm,),
        in_specs=[pl.BlockSpec((tm, d), lambda i: (i, 0)), pl.BlockSpec(wr_pad.shape, lambda i: (0, 0))],
        out_specs=pl.BlockSpec((tm, LANES), lambda i: (i, 0)),
        out_shape=jax.ShapeDtypeStruct((m, LANES), F32),
        compiler_params=_params("arbitrary"),
        name="router_top2",
    )(x2, wr_pad)


def _add_ln_kernel(x_ref, f_ref, ln_ref, out_ref, *, alpha):
    out_ref[...] = _layer_norm(alpha * x_ref[...] + f_ref[...], ln_ref[0:1, :], ln_ref[1:2, :])


def _moe_add_ln_kernel(x_ref, y1_ref, y2_ref, route_ref, ln_ref, out_ref, *, alpha):
    g1 = route_ref[:, 2:3]
    g2 = route_ref[:, 3:4]
    f = g1 * y1_ref[...] + g2 * y2_ref[...]
    out_ref[...] = _layer_norm(alpha * x_ref[...] + f, ln_ref[0:1, :], ln_ref[1:2, :])


def _add_ln(x2, f, ln_vec, alpha, tm=512):
    m, d = x2.shape
    tm = min(tm, m)
    rows = pl.BlockSpec((tm, d), lambda i: (i, 0))
    return pl.pallas_call(
        functools.partial(_add_ln_kernel, alpha=alpha),
        grid=(m // tm,),
        in_specs=[rows, rows, pl.BlockSpec(ln_vec.shape, lambda i: (0, 0))],
        out_specs=rows,
        out_shape=jax.ShapeDtypeStruct((m, d), F32),
        compiler_params=_params("arbitrary"),
        name="add_ln",
    )(x2, f, ln_vec)


def _moe_add_ln(x2, y1, y2, route, ln_vec, alpha, tm=512):
    m, d = x2.shape
    tm = min(tm, m)
    rows = pl.BlockSpec((tm, d), lambda i: (i, 0))
    return pl.pallas_call(
        functools.partial(_moe_add_ln_kernel, alpha=alpha),
        grid=(m // tm,),
        in_specs=[rows, rows, rows, pl.BlockSpec((tm, LANES), lambda i: (i, 0)),
                  pl.BlockSpec(ln_vec.shape, lambda i: (0, 0))],
        out_specs=rows,
        out_shape=jax.ShapeDtypeStruct((m, d), F32),
        compiler_params=_params("arbitrary"),
        name="moe_add_ln",
    )(x2, y1, y2, route, ln_vec)


def _moe(x2, xb, router_w, wg, wu, wd, ln_vec, alpha, tm):
    m, d = x2.shape
    n_exp = wg.shape[0]
    wr_pad = jnp.zeros((d, LANES), F32).at[:, :n_exp].set(router_w)
    route = _router(x2, wr_pad)
    idx = route[:, 0:2].astype(jnp.int32).reshape(-1)
    order = jnp.argsort(idx, stable=True)
    counts = jnp.zeros((n_exp,), jnp.int32).at[idx].add(1)
    padded = ((counts + tm - 1) // tm) * tm
    pad_start = jnp.cumsum(padded) - padded
    start = jnp.cumsum(counts) - counts
    sorted_expert = idx[order]
    dest = pad_start[sorted_expert] + (jnp.arange(2 * m, dtype=jnp.int32) - start[sorted_expert])
    rows_total = 2 * m + n_exp * tm
    src_token = jnp.zeros((rows_total,), jnp.int32).at[dest].set(order // 2)
    slot_pos = jnp.zeros((2 * m,), jnp.int32).at[order].set(dest)
    tile_start = jnp.arange(rows_total // tm, dtype=jnp.int32) * tm
    tile_expert = jnp.clip(jnp.searchsorted(pad_start + padded, tile_start, side="right"),
                           0, n_exp - 1).astype(jnp.int32)
    xg = jnp.take(xb, src_token, axis=0)
    yg = _swiglu(tile_expert, xg, wg, wu, wd, tm, _ffn_chunk(wg.shape[2]))
    pos = slot_pos.reshape(m, 2)
    y1 = jnp.take(yg, pos[:, 0], axis=0)
    y2 = jnp.take(yg, pos[:, 1], axis=0)
    return _moe_add_ln(x2, y1, y2, route, ln_vec, alpha)


def kernel(x, w_in, w_in_vres, shift_mu, shift_mu_vres, decay_up, decay_base, aaa_up, aaa_base, vres_up, vres_base, gate_up, k_k, k_a, r_k, lnx_g, lnx_b, w_out, ln1_g, ln1_b, ln2_g, ln2_b, ffn_w_gate, ffn_w_up, ffn_w_down, router, moe_w_gate, moe_w_up, moe_w_down):
    b, s, d = x.shape
    depth = w_in.shape[0]
    wdt = decay_up.shape[2]
    att = w_out.shape[1] - wdt
    alpha = float((2 * depth) ** 0.25)
    m = b * s
    ffn_tm = min(512, m)

    seg = (jnp.arange(wdt)[:, None] // RWKV_HEAD_DIM == jnp.arange(wdt)[None, :] // RWKV_HEAD_DIM).astype(F32)
    q_scale = jnp.concatenate([jnp.full((att,), ATT_HEAD_DIM ** -0.5, F32), jnp.ones((2 * att,), F32)])
    zero_tiles = jnp.zeros((m // ffn_tm,), jnp.int32)

    x2 = x.transpose(1, 0, 2).reshape(m, d)
    v_first = None
    for l in range(depth):
        has_vres = l > 0
        w_l = w_in[l]
        c0 = 3 * att
        wqkv = (w_l[:, :c0] * q_scale).astype(BF16)
        wrkv = w_l[:, c0:c0 + 3 * wdt].astype(BF16)
        lora_w = w_l[:, c0 + 3 * wdt:]
        lora_mu = shift_mu[l, 3 * wdt:]
        if has_vres:
            lora_w = jnp.concatenate([lora_w, w_in_vres[l - 1]], axis=1)
            lora_mu = jnp.concatenate([lora_mu, shift_mu_vres[l - 1]])
        n_lora = lora_w.shape[1]
        wlora = jnp.pad(lora_w, ((0, 0), (0, LORA_PAD - n_lora))).astype(BF16)
        mu_lora = jnp.pad(lora_mu, (0, LORA_PAD - n_lora)).reshape(1, LORA_PAD)
        mu_rkv = shift_mu[l, :3 * wdt].reshape(1, 3 * wdt)

        qkv, rkv, lora = _inproj(x2, wqkv, wrkv, wlora, mu_rkv, mu_lora, b)

        branches = [_attn_pattern(qkv, b, dil) for dil in DILATIONS]
        o_parts = [br[0] for br in branches]
        lse_parts = [br[1] for br in branches]

        zpad = lambda w, rows_before: jnp.pad(
            w, ((rows_before, LANES - rows_before - w.shape[0]), (0, 0))).astype(BF16)
        decay_up_p = zpad(decay_up[l], 0)
        aaa_up_p = zpad(aaa_up[l], DECAY_LORA)
        gate_up_p = gate_up[l].astype(BF16)
        if has_vres:
            vres_up_p = zpad(vres_up[l - 1], 0)
            vres_b = vres_base[l - 1]
            vf_in = v_first
        else:
            vres_up_p = jnp.zeros((LANES, wdt), BF16)
            vres_b = jnp.zeros((wdt,), F32)
            vf_in = rkv[:, 2 * wdt:]
        vecs = jnp.stack([decay_base[l], aaa_base[l], vres_b, k_k[l], k_a[l], r_k[l],
                          jnp.zeros((wdt,), F32), jnp.zeros((wdt,), F32)])
        r_, w_, k_, a_, b_, vl_, v_, g_, bonus = _rwkv_prep(
            rkv, lora, vf_in, decay_up_p, aaa_up_p, gate_up_p, vres_up_p, vecs, seg, has_vres, b)
        if not has_vres:
            v_first = v_
        y_lanes = _rwkv_scan(a_, w_, b_, k_, r_, vl_)

        gn_vec = jnp.stack([lnx_g[l], lnx_b[l]] + [jnp.zeros((wdt,), F32)] * 6)
        ln1 = jnp.stack([ln1_g[l], ln1_b[l]] + [jnp.zeros((d,), F32)] * 6)
        ln2 = jnp.stack([ln2_g[l], ln2_b[l]] + [jnp.zeros((d,), F32)] * 6)
        wo = w_out[l].astype(BF16)
        x2, xb = _outproj(o_parts, lse_parts, y_lanes, bonus, g_, x2, wo[:att], wo[att:],
                          gn_vec, ln1, seg, alpha, b)

        i = l // 2
        if l % 2 == 0:
            wg = ffn_w_gate[i].astype(BF16)[None]
            wu = ffn_w_up[i].astype(BF16)[None]
            wd = ffn_w_down[i].astype(BF16)[None]
            f = _swiglu(zero_tiles, xb, wg, wu, wd, ffn_tm, _ffn_chunk(wg.shape[2]))
            x2 = _add_ln(x2, f, ln2, alpha)
        else:
            x2 = _moe(x2, xb, router[i], moe_w_gate[i].astype(BF16), moe_w_up[i].astype(BF16),
                      moe_w_down[i].astype(BF16), ln2, alpha, ffn_tm)
    return x2.reshape(s, b, d).transpose(1, 0, 2)
```

```python
import functools

import jax
import jax.numpy as jnp
from jax import lax
from jax.experimental import pallas as pl
from jax.experimental.pallas import tpu as pltpu

ATT_HEAD_DIM = 64
RWKV_HEAD_DIM = 64
DILATIONS = (1, 4, 16)
ATT_BLOCK = 128
DECAY_LORA = 64
AAA_LORA = 64
GATE_LORA = 128
VRES_LORA = 32
N_EXPERTS = 8
LN_EPS = 1e-5
GN_EPS = 64e-5

LANES = 128
LORA_PAD = 384
VMEM_LIMIT = 56 * 1024 * 1024
NEG_BIG = -1e30

F32 = jnp.float32
BF16 = jnp.bfloat16


def _params(*sem):
    return pltpu.CompilerParams(dimension_semantics=sem, vmem_limit_bytes=VMEM_LIMIT)


def _inproj_kernel(x_ref, wqkv_ref, wrkv_ref, wlora_ref, mu_rkv_ref, mu_lora_ref,
                   qkv_ref, rkv_ref, lora_ref, carry_rkv, carry_lora, *, batch):
    xb = x_ref[...].astype(BF16)
    qkv_ref[...] = jnp.dot(xb, wqkv_ref[...], preferred_element_type=F32).astype(BF16)

    first = pl.program_id(0) == 0

    def shifted(w_ref, mu_ref, carry_ref, out_ref):
        p = jnp.dot(xb, w_ref[...], preferred_element_type=F32)
        tm = p.shape[0]
        mu = mu_ref[...]
        out_ref[...] = p + (pltpu.roll(p, batch, axis=0) - p) * mu
        head = p[0:batch, :]
        prev_head = jnp.where(first, 0.0, carry_ref[...])
        out_ref[0:batch, :] = head + (prev_head - head) * mu
        carry_ref[...] = p[tm - batch:tm, :]

    shifted(wrkv_ref, mu_rkv_ref, carry_rkv, rkv_ref)
    shifted(wlora_ref, mu_lora_ref, carry_lora, lora_ref)


def _inproj(x2, wqkv, wrkv, wlora, mu_rkv, mu_lora, batch, tm=512):
    m, d = x2.shape
    nq, nr, nl = wqkv.shape[1], wrkv.shape[1], wlora.shape[1]
    tm = min(tm, m)
    full = lambda shape: pl.BlockSpec(shape, lambda i: (0, 0))
    return pl.pallas_call(
        functools.partial(_inproj_kernel, batch=batch),
        grid=(m // tm,),
        in_specs=[pl.BlockSpec((tm, d), lambda i: (i, 0)),
                  full((d, nq)), full((d, nr)), full((d, nl)), full((1, nr)), full((1, nl))],
        out_specs=[pl.BlockSpec((tm, nq), lambda i: (i, 0)),
                   pl.BlockSpec((tm, nr), lambda i: (i, 0)),
                   pl.BlockSpec((tm, nl), lambda i: (i, 0))],
        out_shape=[jax.ShapeDtypeStruct((m, nq), BF16),
                   jax.ShapeDtypeStruct((m, nr), F32),
                   jax.ShapeDtypeStruct((m, nl), F32)],
        scratch_shapes=[pltpu.VMEM((batch, nr), F32), pltpu.VMEM((batch, nl), F32)],
        compiler_params=_params("arbitrary"),
        name="inproj",
    )(x2, wqkv, wrkv, wlora, mu_rkv, mu_lora)


def _attn_kernel(q_ref, k_ref, v_ref, o_ref, lse_ref):
    n_blocks = q_ref.shape[0] // ATT_BLOCK
    win = 2 * ATT_BLOCK
    rel = (lax.broadcasted_iota(jnp.int32, (ATT_BLOCK, win), 0)
           - lax.broadcasted_iota(jnp.int32, (ATT_BLOCK, win), 1))
    lane = lax.broadcasted_iota(jnp.int32, (ATT_BLOCK, LANES), 1)
    nt = (((1,), (1,)), ((), ()))

    def block(n, carry):
        q_start = pl.multiple_of(n * ATT_BLOCK, ATT_BLOCK)
        k_start = pl.multiple_of(jnp.maximum(n - 1, 0) * ATT_BLOCK, ATT_BLOCK)
        q = q_ref[pl.ds(q_start, ATT_BLOCK), :]
        kw = k_ref[pl.ds(k_start, win), :]
        vw = v_ref[pl.ds(k_start, win), :]
        dist = rel + (q_start - k_start)
        ok = (dist >= 0) & (dist <= ATT_BLOCK)
        o_acc = jnp.zeros((ATT_BLOCK, LANES), F32)
        lse_acc = jnp.zeros((ATT_BLOCK, LANES), F32)
        for h in range(LANES // ATT_HEAD_DIM):
            in_head = (lane >= h * ATT_HEAD_DIM) & (lane < (h + 1) * ATT_HEAD_DIM)
            qh = jnp.where(in_head, q, jnp.zeros_like(q))
            sc = jnp.where(ok, lax.dot_general(qh, kw, nt, preferred_element_type=F32), NEG_BIG)
            m = jnp.max(sc, axis=-1, keepdims=True)
            p = jnp.exp(sc - m)
            den = jnp.sum(p, axis=-1, keepdims=True)
            o = jnp.dot(p.astype(BF16), vw, preferred_element_type=F32) / den
            o_acc = jnp.where(in_head, o, o_acc)
            lse_acc = jnp.where(in_head, m + jnp.log(den), lse_acc)
        o_ref[pl.ds(q_start, ATT_BLOCK), :] = o_acc
        lse_ref[pl.ds(q_start, ATT_BLOCK), :] = lse_acc
        return carry

    lax.fori_loop(0, n_blocks, block, 0, unroll=2)


def _attn_pattern(qkv, batch, dil):
    m, c3 = qkv.shape
    a = c3 // 3
    npair = a // LANES
    l = m // (batch * dil)
    view = qkv.reshape(l, dil * batch * c3)
    cpb = c3 // LANES
    qmap = lambda bi, c, hp: (0, (c * batch + bi) * cpb + hp)
    kmap = lambda bi, c, hp: (0, (c * batch + bi) * cpb + npair + hp)
    vmap = lambda bi, c, hp: (0, (c * batch + bi) * cpb + 2 * npair + hp)
    omap = lambda bi, c, hp: (0, (c * batch + bi) * npair + hp)
    blk = (l, LANES)
    o, lse = pl.pallas_call(
        _attn_kernel,
        grid=(batch, dil, npair),
        in_specs=[pl.BlockSpec(blk, qmap), pl.BlockSpec(blk, kmap), pl.BlockSpec(blk, vmap)],
        out_specs=[pl.BlockSpec(blk, omap), pl.BlockSpec(blk, omap)],
        out_shape=[jax.ShapeDtypeStruct((l, dil * batch * a), F32)] * 2,
        compiler_params=_params("arbitrary", "arbitrary", "arbitrary"),
        name=f"dilated_attn_d{dil}",
    )(view, view, view)
    return o.reshape(m, a), lse.reshape(m, a)


def _segment_sum(x, seg_ref):
    return jnp.dot(x, seg_ref[...], preferred_element_type=F32, precision=lax.Precision.HIGHEST)


KEY_HALVES = 2


def _chunk_transpose(parts, chunk):
    n = len(parts)
    lane = lax.broadcasted_iota(jnp.int32, parts[0].shape, 1)
    s = n // 2
    while s >= 1:
        swap = ((lane // chunk) & s) != 0
        new = list(parts)
        for j in range(n):
            if j & s == 0:
                lo, hi = parts[j], parts[j + s]
                new[j] = jnp.where(swap, pltpu.roll(hi, s * chunk, axis=1), lo)
                new[j + s] = jnp.where(swap, hi, pltpu.roll(lo, LANES - s * chunk, axis=1))
        parts = new
        s //= 2
    return parts


def _keys_to_lanes(p, heads, batch):
    q = p.T
    half = RWKV_HEAD_DIM // KEY_HALVES
    parts = [q[(h * KEY_HALVES + kh) * half:(h * KEY_HALVES + kh + 1) * half, :]
             for kh in range(KEY_HALVES) for h in range(heads)]
    return _chunk_transpose(parts, batch)


def _values_to_lanes(p, heads, batch):
    q = p.T
    parts = [q[h * RWKV_HEAD_DIM:(h + 1) * RWKV_HEAD_DIM, :]
             for _ in range(KEY_HALVES) for h in range(heads)]
    return _chunk_transpose(parts, batch)


def _values_from_lanes(parts, heads, batch):
    per_half = _chunk_transpose(parts, batch)
    per_head = [per_half[h] + per_half[heads + h] for h in range(heads)]
    return jnp.concatenate(per_head, axis=0).T


def _rwkv_prep_kernel(rkv_ref, lora_ref, vfirst_ref, decay_up_ref, aaa_up_ref, gate_up_ref,
                      vres_up_ref, vec_ref, seg_ref,
                      r_ref, w_ref, k_ref, a_ref, b_ref, vl_ref, v_ref, g_ref, bonus_ref,
                      *, has_vres, batch):
    wdt = rkv_ref.shape[1] // 3
    r = rkv_ref[:, 0:wdt]
    k = rkv_ref[:, wdt:2 * wdt]
    v = rkv_ref[:, 2 * wdt:3 * wdt]
    wa = lora_ref[:, 0:LANES]
    gd = lora_ref[:, LANES:2 * LANES]
    decay_base, aaa_base, vres_base = vec_ref[0:1, :], vec_ref[1:2, :], vec_ref[2:3, :]
    k_k, k_a, r_k = vec_ref[3:4, :], vec_ref[4:5, :], vec_ref[5:6, :]

    z = -(decay_base + jnp.dot(jnp.tanh(wa).astype(BF16), decay_up_ref[...],
                               preferred_element_type=F32))
    softplus = jnp.maximum(z, 0.0) + jnp.log(1.0 + jnp.exp(-jnp.abs(z)))
    w_log = -softplus - 0.5
    w = jnp.exp(-jnp.exp(w_log))
    a = jax.nn.sigmoid(aaa_base + jnp.dot(wa.astype(BF16), aaa_up_ref[...],
                                          preferred_element_type=F32))
    g_ref[...] = jnp.dot(jax.nn.sigmoid(gd).astype(BF16), gate_up_ref[...],
                         preferred_element_type=F32)
    if has_vres:
        vd = lora_ref[:, 2 * LANES:3 * LANES]
        mix = jax.nn.sigmoid(vres_base + jnp.dot(vd.astype(BF16), vres_up_ref[...],
                                                 preferred_element_type=F32))
        v = v + (vfirst_ref[...] - v) * mix
    kk = k * k_k
    norm = jnp.sqrt(_segment_sum(kk * kk, seg_ref))
    kk = kk / jnp.maximum(norm, 1e-12)
    k2 = k * (1.0 + (a - 1.0) * k_a)
    v_ref[...] = v
    bonus_ref[...] = _segment_sum(r * k2 * r_k, seg_ref) * v

    heads = wdt // RWKV_HEAD_DIM
    group = LANES
    steps = group // batch
    for gi in range(rkv_ref.shape[0] // group):
        rows = slice(gi * group, (gi + 1) * group)
        for ref, val in ((r_ref, r), (w_ref, w), (k_ref, k2), (a_ref, -kk), (b_ref, kk * a)):
            for t, part in enumerate(_keys_to_lanes(val[rows], heads, batch)):
                ref[gi * steps + t] = part
        for t, part in enumerate(_values_to_lanes(v[rows], heads, batch)):
            vl_ref[gi * steps + t] = part


def _rwkv_prep(rkv, lora, vfirst, decay_up, aaa_up, gate_up, vres_up, vecs, seg, has_vres, batch, tm=256):
    m = rkv.shape[0]
    wdt = rkv.shape[1] // 3
    heads = wdt // RWKV_HEAD_DIM
    assert KEY_HALVES * heads * batch == LANES, "recurrence layout needs 2*heads*batch == 128 lanes"
    tm = min(tm, m)
    assert tm % LANES == 0
    steps = tm // batch
    seq = m // batch
    half = RWKV_HEAD_DIM // KEY_HALVES
    rows = lambda n: pl.BlockSpec((tm, n), lambda i: (i, 0))
    full = lambda arr: pl.BlockSpec(arr.shape, lambda i: (0, 0))
    key_spec = pl.BlockSpec((steps, half, LANES), lambda i: (i, 0, 0))
    val_spec = pl.BlockSpec((steps, RWKV_HEAD_DIM, LANES), lambda i: (i, 0, 0))
    key_out = jax.ShapeDtypeStruct((seq, half, LANES), F32)
    val_out = jax.ShapeDtypeStruct((seq, RWKV_HEAD_DIM, LANES), F32)
    nat_out = jax.ShapeDtypeStruct((m, wdt), F32)
    return pl.pallas_call(
        functools.partial(_rwkv_prep_kernel, has_vres=has_vres, batch=batch),
        grid=(m // tm,),
        in_specs=[rows(3 * wdt), rows(LORA_PAD), rows(wdt), full(decay_up), full(aaa_up),
                  full(gate_up), full(vres_up), full(vecs), full(seg)],
        out_specs=[key_spec] * 5 + [val_spec] + [rows(wdt)] * 3,
        out_shape=[key_out] * 5 + [val_out] + [nat_out] * 3,
        compiler_params=_params("arbitrary"),
        name="rwkv_prep",
    )(rkv, lora, vfirst, decay_up, aaa_up, gate_up, vres_up, vecs, seg)


def _rwkv_scan_kernel(a_ref, a_next_ref, w_ref, b_ref, k_ref, r_ref, v_ref, y_ref, state_ref, sa_ref):
    @pl.when(pl.program_id(0) == 0)
    def _():
        state_ref[...] = jnp.zeros_like(state_ref)
        sa_ref[...] = jnp.zeros_like(sa_ref)

    steps, nk, _ = a_ref.shape
    nv = v_ref.shape[1]

    def step(t, sa_half, next_a_row):
        sa = sa_half + pltpu.roll(sa_half, LANES // KEY_HALVES, axis=1)
        row = lambda ref, kx: jnp.broadcast_to(ref[t, pl.ds(kx, 1), :], (nv, LANES))
        v_t = v_ref[t]
        acc_y = None
        acc_sa = None
        for kx in range(nk):
            new = state_ref[kx] * row(w_ref, kx) + v_t * row(k_ref, kx) + sa * row(b_ref, kx)
            state_ref[kx] = new
            term_y = new * row(r_ref, kx)
            term_sa = new * jnp.broadcast_to(next_a_row(kx), (nv, LANES))
            acc_y = term_y if acc_y is None else acc_y + term_y
            acc_sa = term_sa if acc_sa is None else acc_sa + term_sa
        y_ref[t] = acc_y
        return acc_sa

    def body(t, sa_half):
        return step(t, sa_half, lambda kx: a_ref[t + 1, pl.ds(kx, 1), :])

    sa_half = lax.fori_loop(0, steps - 1, body, sa_ref[...])
    sa_ref[...] = step(steps - 1, sa_half, lambda kx: a_next_ref[0, pl.ds(kx, 1), :])


def _rwkv_scan(a_t, w_t, b_t, k_t, r_t, v_t, steps=32):
    s, nk, _ = a_t.shape
    nv = v_t.shape[1]
    steps = min(steps, s)
    n_blocks = s // steps
    kin = pl.BlockSpec((steps, nk, LANES), lambda i: (i, 0, 0))
    knext = pl.BlockSpec((1, nk, LANES), lambda i: (jnp.minimum((i + 1) * steps, s - 1), 0, 0))
    vin = pl.BlockSpec((steps, nv, LANES), lambda i: (i, 0, 0))
    return pl.pallas_call(
        _rwkv_scan_kernel,
        grid=(n_blocks,),
        in_specs=[kin, knext] + [kin] * 4 + [vin],
        out_specs=vin,
        out_shape=jax.ShapeDtypeStruct((s, nv, LANES), F32),
        scratch_shapes=[pltpu.VMEM((nk, nv, LANES), F32), pltpu.VMEM((nv, LANES), F32)],
        compiler_params=_params("arbitrary"),
        name="rwkv_scan",
    )(a_t, a_t, w_t, b_t, k_t, r_t, v_t)


def _layer_norm(z, g, b):
    mu = jnp.mean(z, axis=-1, keepdims=True)
    zc = z - mu
    var = jnp.mean(zc * zc, axis=-1, keepdims=True)
    return zc * lax.rsqrt(var + LN_EPS) * g + b


def _outproj_kernel(o1_ref, o2_ref, o3_ref, l1_ref, l2_ref, l3_ref, y_ref, bonus_ref, g_ref,
                    x_ref, wa_ref, wr_ref, vec_ref, ln_ref, seg_ref, out_ref, outb_ref,
                    *, alpha, batch):
    l1, l2, l3 = l1_ref[...], l2_ref[...], l3_ref[...]
    m = jnp.maximum(jnp.maximum(l1, l2), l3)
    e1, e2, e3 = jnp.exp(l1 - m), jnp.exp(l2 - m), jnp.exp(l3 - m)
    att = (o1_ref[...] * e1 + o2_ref[...] * e2 + o3_ref[...] * e3) / (e1 + e2 + e3)

    heads = bonus_ref.shape[1] // RWKV_HEAD_DIM
    steps = LANES // batch
    y = jnp.concatenate(
        [_values_from_lanes([y_ref[gi * steps + t] for t in range(steps)], heads, batch)
         for gi in range(y_ref.shape[0] // steps)], axis=0)
    inv_n = 1.0 / RWKV_HEAD_DIM
    mu = _segment_sum(y, seg_ref) * inv_n
    yc = y - mu
    var = _segment_sum(yc * yc, seg_ref) * inv_n
    yn = yc * lax.rsqrt(var + GN_EPS) * vec_ref[0:1, :] + vec_ref[1:2, :]
    rw = (yn + bonus_ref[...]) * g_ref[...]

    mix = (jnp.dot(att.astype(BF16), wa_ref[...], preferred_element_type=F32)
           + jnp.dot(rw.astype(BF16), wr_ref[...], preferred_element_type=F32))
    out = _layer_norm(alpha * x_ref[...] + mix, ln_ref[0:1, :], ln_ref[1:2, :])
    out_ref[...] = out
    outb_ref[...] = out.astype(BF16)


def _outproj(o, lse, y_lanes, bonus, g, x2, wa, wr, gn_vec, ln_vec, seg, alpha, batch, tm=256):
    m, d = x2.shape
    a = o[0].shape[1]
    tm = min(tm, m)
    assert tm % LANES == 0
    rows = lambda n: pl.BlockSpec((tm, n), lambda i: (i, 0))
    full = lambda arr: pl.BlockSpec(arr.shape, lambda i: (0, 0))
    y_spec = pl.BlockSpec((tm // batch,) + y_lanes.shape[1:], lambda i: (i, 0, 0))
    return pl.pallas_call(
        functools.partial(_outproj_kernel, alpha=alpha, batch=batch),
        grid=(m // tm,),
        in_specs=[rows(a)] * 6 + [y_spec] + [rows(a)] * 2
                 + [rows(d), full(wa), full(wr), full(gn_vec), full(ln_vec), full(seg)],
        out_specs=[rows(d), rows(d)],
        out_shape=[jax.ShapeDtypeStruct((m, d), F32), jax.ShapeDtypeStruct((m, d), BF16)],
        compiler_params=_params("arbitrary"),
        name="outproj_ln",
    )(*o, *lse, y_lanes, bonus, g, x2, wa, wr, gn_vec, ln_vec, seg)


def _swiglu_kernel(expert_ref, x_ref, wg_ref, wu_ref, wd_ref, out_ref, acc_ref):
    j = pl.program_id(1)

    @pl.when(j == 0)
    def _():
        acc_ref[...] = jnp.zeros_like(acc_ref)

    x = x_ref[...]
    gate = jnp.dot(x, wg_ref[0], preferred_element_type=F32)
    up = jnp.dot(x, wu_ref[0], preferred_element_type=F32)
    h = (gate * jax.nn.sigmoid(gate) * up).astype(BF16)
    acc_ref[...] += jnp.dot(h, wd_ref[0], preferred_element_type=F32)

    @pl.when(j == pl.num_programs(1) - 1)
    def _():
        out_ref[...] = acc_ref[...]


def _swiglu(tile_expert, xb, wg, wu, wd, tm, fc):
    m, d = xb.shape
    f = wg.shape[2]
    return pl.pallas_call(
        _swiglu_kernel,
        grid_spec=pltpu.PrefetchScalarGridSpec(
            num_scalar_prefetch=1,
            grid=(m // tm, f // fc),
            in_specs=[pl.BlockSpec((tm, d), lambda i, j, e: (i, 0)),
                      pl.BlockSpec((1, d, fc), lambda i, j, e: (e[i], 0, j)),
                      pl.BlockSpec((1, d, fc), lambda i, j, e: (e[i], 0, j)),
                      pl.BlockSpec((1, fc, d), lambda i, j, e: (e[i], j, 0))],
            out_specs=pl.BlockSpec((tm, d), lambda i, j, e: (i, 0)),
            scratch_shapes=[pltpu.VMEM((tm, d), F32)]),
        out_shape=jax.ShapeDtypeStruct((m, d), F32),
        compiler_params=_params("arbitrary", "arbitrary"),
        name="swiglu",
    )(tile_expert, xb, wg, wu, wd)


def _ffn_chunk(f):
    for fc in (512, 256, 128):
        if f % fc == 0:
            return fc
    raise ValueError(f"feed-forward width {f} is not a multiple of {LANES}")


def _router_kernel(x_ref, wr_ref, out_ref):
    logits = jnp.dot(x_ref[...], wr_ref[...], preferred_element_type=F32,
                     precision=lax.Precision.HIGHEST)
    lane = lax.broadcasted_iota(jnp.int32, logits.shape, 1)
    logits = jnp.where(lane < N_EXPERTS, logits, NEG_BIG)
    v1 = jnp.max(logits, axis=-1, keepdims=True)
    i1 = jnp.min(jnp.where(logits == v1, lane, LANES), axis=-1, keepdims=True)
    rest = jnp.where(lane == i1, NEG_BIG, logits)
    v2 = jnp.max(rest, axis=-1, keepdims=True)
    i2 = jnp.min(jnp.where(rest == v2, lane, LANES), axis=-1, keepdims=True)
    e2 = jnp.exp(v2 - v1)
    g1 = 1.0 / (1.0 + e2)
    g2 = e2 / (1.0 + e2)
    out_ref[...] = jnp.where(lane == 0, i1.astype(F32),
                             jnp.where(lane == 1, i2.astype(F32),
                                       jnp.where(lane == 2, g1, jnp.where(lane == 3, g2, 0.0))))


def _router(x2, wr_pad, tm=512):
    m, d = x2.shape
    tm = min(tm, m)
    return pl.pallas_call(
        _router_kernel,
        grid=(m // tm,),
        in_specs=[pl.BlockSpec((tm, d), lambda i: (i, 0)), pl.BlockSpec(wr_pad.shape, lambda i: (0, 0))],
        out_specs=pl.BlockSpec((tm, LANES), lambda i: (i, 0)),
        out_shape=jax.ShapeDtypeStruct((m, LANES), F32),
        compiler_params=_params("arbitrary"),
        name="router_top2",
    )(x2, wr_pad)


def _add_ln_kernel(x_ref, f_ref, ln_ref, out_ref, *, alpha):
    out_ref[...] = _layer_norm(alpha * x_ref[...] + f_ref[...], ln_ref[0:1, :], ln_ref[1:2, :])


def _moe_add_ln_kernel(x_ref, y1_ref, y2_ref, route_ref, ln_ref, out_ref, *, alpha):
    g1 = route_ref[:, 2:3]
    g2 = route_ref[:, 3:4]
    f = g1 * y1_ref[...] + g2 * y2_ref[...]
    out_ref[...] = _layer_norm(alpha * x_ref[...] + f, ln_ref[0:1, :], ln_ref[1:2, :])


def _add_ln(x2, f, ln_vec, alpha, tm=512):
    m, d = x2.shape
    tm = min(tm, m)
    rows = pl.BlockSpec((tm, d), lambda i: (i, 0))
    return pl.pallas_call(
        functools.partial(_add_ln_kernel, alpha=alpha),
        grid=(m // tm,),
        in_specs=[rows, rows, pl.BlockSpec(ln_vec.shape, lambda i: (0, 0))],
        out_specs=rows,
        out_shape=jax.ShapeDtypeStruct((m, d), F32),
        compiler_params=_params("arbitrary"),
        name="add_ln",
    )(x2, f, ln_vec)


def _moe_add_ln(x2, y1, y2, route, ln_vec, alpha, tm=512):
    m, d = x2.shape
    tm = min(tm, m)
    rows = pl.BlockSpec((tm, d), lambda i: (i, 0))
    return pl.pallas_call(
        functools.partial(_moe_add_ln_kernel, alpha=alpha),
        grid=(m // tm,),
        in_specs=[rows, rows, rows, pl.BlockSpec((tm, LANES), lambda i: (i, 0)),
                  pl.BlockSpec(ln_vec.shape, lambda i: (0, 0))],
        out_specs=rows,
        out_shape=jax.ShapeDtypeStruct((m, d), F32),
        compiler_params=_params("arbitrary"),
        name="moe_add_ln",
    )(x2, y1, y2, route, ln_vec)


def _moe(x2, xb, router_w, wg, wu, wd, first_expert, n_exp, ln_vec, alpha, tm):
    m, d = x2.shape
    wr_pad = jnp.pad(router_w, ((0, 0), (0, LANES - n_exp)))
    route = _router(x2, wr_pad)
    idx = route[:, 0:2].astype(jnp.int32).reshape(-1)
    experts = jnp.arange(n_exp, dtype=jnp.int32)
    counts = jnp.sum((idx[:, None] == experts[None, :]).astype(jnp.int32), axis=0)
    padded = ((counts + tm - 1) // tm) * tm
    pad_end = jnp.cumsum(padded)
    pad_start = pad_end - padded
    start = jnp.cumsum(counts) - counts
    order = jnp.argsort(idx, stable=True)
    rank = jnp.argsort(order)
    slot_pos = pad_start[idx] + (rank - start[idx])
    rows_total = 2 * m + n_exp * tm
    tile_start = jnp.arange(rows_total // tm, dtype=jnp.int32) * tm
    tile_expert = jnp.minimum(jnp.sum((tile_start[:, None] >= pad_end[None, :]).astype(jnp.int32), axis=1),
                              n_exp - 1)
    row = jnp.arange(rows_total, dtype=jnp.int32)
    row_expert = jnp.repeat(tile_expert, tm)
    sorted_pos = jnp.clip(row - pad_start[row_expert] + start[row_expert], 0, 2 * m - 1)
    src_token = order[sorted_pos] // 2
    xg = jnp.take(xb, src_token, axis=0)
    yg = _swiglu(tile_expert + first_expert, xg, wg, wu, wd, tm, _ffn_chunk(wg.shape[2]))
    pos = slot_pos.reshape(m, 2)
    y1 = jnp.take(yg, pos[:, 0], axis=0)
    y2 = jnp.take(yg, pos[:, 1], axis=0)
    return _moe_add_ln(x2, y1, y2, route, ln_vec, alpha)


def kernel(x, w_in, w_in_vres, shift_mu, shift_mu_vres, decay_up, decay_base, aaa_up, aaa_base, vres_up, vres_base, gate_up, k_k, k_a, r_k, lnx_g, lnx_b, w_out, ln1_g, ln1_b, ln2_g, ln2_b, ffn_w_gate, ffn_w_up, ffn_w_down, router, moe_w_gate, moe_w_up, moe_w_down):
    b, s, d = x.shape
    depth = w_in.shape[0]
    wdt = decay_up.shape[2]
    att = w_out.shape[1] - wdt
    alpha = float((2 * depth) ** 0.25)
    m = b * s
    ffn_tm = min(512, m)

    seg = (jnp.arange(wdt)[:, None] // RWKV_HEAD_DIM == jnp.arange(wdt)[None, :] // RWKV_HEAD_DIM).astype(F32)
    q_scale = jnp.concatenate([jnp.full((att,), ATT_HEAD_DIM ** -0.5, F32), jnp.ones((2 * att,), F32)])
    zero_tiles = jnp.zeros((m // ffn_tm,), jnp.int32)
    ffn_w = [w.astype(BF16) for w in (ffn_w_gate, ffn_w_up, ffn_w_down)]
    n_exp = moe_w_gate.shape[1]
    moe_w = [w.astype(BF16).reshape((-1,) + w.shape[2:]) for w in (moe_w_gate, moe_w_up, moe_w_down)]

    x2 = x.transpose(1, 0, 2).reshape(m, d)
    v_first = None
    for l in range(depth):
        has_vres = l > 0
        w_l = w_in[l]
        c0 = 3 * att
        wqkv = (w_l[:, :c0] * q_scale).astype(BF16)
        wrkv = w_l[:, c0:c0 + 3 * wdt].astype(BF16)
        lora_w = w_l[:, c0 + 3 * wdt:]
        lora_mu = shift_mu[l, 3 * wdt:]
        if has_vres:
            lora_w = jnp.concatenate([lora_w, w_in_vres[l - 1]], axis=1)
            lora_mu = jnp.concatenate([lora_mu, shift_mu_vres[l - 1]])
        n_lora = lora_w.shape[1]
        wlora = jnp.pad(lora_w, ((0, 0), (0, LORA_PAD - n_lora))).astype(BF16)
        mu_lora = jnp.pad(lora_mu, (0, LORA_PAD - n_lora)).reshape(1, LORA_PAD)
        mu_rkv = shift_mu[l, :3 * wdt].reshape(1, 3 * wdt)

        qkv, rkv, lora = _inproj(x2, wqkv, wrkv, wlora, mu_rkv, mu_lora, b)

        branches = [_attn_pattern(qkv, b, dil) for dil in DILATIONS]
        o_parts = [br[0] for br in branches]
        lse_parts = [br[1] for br in branches]

        zpad = lambda w, rows_before: jnp.pad(
            w, ((rows_before, LANES - rows_before - w.shape[0]), (0, 0))).astype(BF16)
        decay_up_p = zpad(decay_up[l], 0)
        aaa_up_p = zpad(aaa_up[l], DECAY_LORA)
        gate_up_p = gate_up[l].astype(BF16)
        if has_vres:
            vres_up_p = zpad(vres_up[l - 1], 0)
            vres_b = vres_base[l - 1]
            vf_in = v_first
        else:
            vres_up_p = jnp.zeros((LANES, wdt), BF16)
            vres_b = jnp.zeros((wdt,), F32)
            vf_in = rkv[:, 2 * wdt:]
        vecs = jnp.stack([decay_base[l], aaa_base[l], vres_b, k_k[l], k_a[l], r_k[l],
                          jnp.zeros((wdt,), F32), jnp.zeros((wdt,), F32)])
        r_, w_, k_, a_, b_, vl_, v_, g_, bonus = _rwkv_prep(
            rkv, lora, vf_in, decay_up_p, aaa_up_p, gate_up_p, vres_up_p, vecs, seg, has_vres, b)
        if not has_vres:
            v_first = v_
        y_lanes = _rwkv_scan(a_, w_, b_, k_, r_, vl_)

        gn_vec = jnp.stack([lnx_g[l], lnx_b[l]] + [jnp.zeros((wdt,), F32)] * 6)
        ln1 = jnp.stack([ln1_g[l], ln1_b[l]] + [jnp.zeros((d,), F32)] * 6)
        ln2 = jnp.stack([ln2_g[l], ln2_b[l]] + [jnp.zeros((d,), F32)] * 6)
        wo = w_out[l].astype(BF16)
        x2, xb = _outproj(o_parts, lse_parts, y_lanes, bonus, g_, x2, wo[:att], wo[att:],
                          gn_vec, ln1, seg, alpha, b)

        i = l // 2
        if l % 2 == 0:
            f = _swiglu(zero_tiles + i, xb, *ffn_w, ffn_tm, _ffn_chunk(ffn_w[0].shape[2]))
            x2 = _add_ln(x2, f, ln2, alpha)
        else:
            x2 = _moe(x2, xb, router[i], *moe_w, i * n_exp, n_exp, ln2, alpha, ffn_tm)
    return x2.reshape(s, b, d).transpose(1, 0, 2)
```

```python
import functools

import jax
import jax.numpy as jnp
from jax import lax
from jax.experimental import pallas as pl
from jax.experimental.pallas import tpu as pltpu

ATT_HEAD_DIM = 64
RWKV_HEAD_DIM = 64
DILATIONS = (1, 4, 16)
ATT_BLOCK = 128
DECAY_LORA = 64
AAA_LORA = 64
GATE_LORA = 128
VRES_LORA = 32
N_EXPERTS = 8
LN_EPS = 1e-5
GN_EPS = 64e-5

LANES = 128
LORA_PAD = 384
VMEM_LIMIT = 56 * 1024 * 1024
NEG_BIG = -1e30

F32 = jnp.float32
BF16 = jnp.bfloat16


def _params(*sem):
    return pltpu.CompilerParams(dimension_semantics=sem, vmem_limit_bytes=VMEM_LIMIT)


def _inproj_kernel(x_ref, wqkv_ref, wrkv_ref, wlora_ref, mu_rkv_ref, mu_lora_ref,
                   qkv_ref, rkv_ref, lora_ref, carry_rkv, carry_lora, qkv_rows, *, batch):
    xb = x_ref[...].astype(BF16)
    qkv = jnp.dot(xb, wqkv_ref[...], preferred_element_type=F32)
    n_tiles, rows, _ = qkv_rows.shape
    for j in range(n_tiles):
        qkv_rows[j] = qkv[:, j * LANES:(j + 1) * LANES]
    for bi in range(batch):
        for j in range(n_tiles):
            qkv_ref[bi, :, j * LANES:(j + 1) * LANES] = qkv_rows[j, pl.ds(bi, rows // batch, stride=batch), :]

    first = pl.program_id(0) == 0

    def shifted(w_ref, mu_ref, carry_ref, out_ref):
        p = jnp.dot(xb, w_ref[...], preferred_element_type=F32)
        tm = p.shape[0]
        mu = mu_ref[...]
        out_ref[...] = p + (pltpu.roll(p, batch, axis=0) - p) * mu
        head = p[0:batch, :]
        prev_head = jnp.where(first, 0.0, carry_ref[...])
        out_ref[0:batch, :] = head + (prev_head - head) * mu
        carry_ref[...] = p[tm - batch:tm, :]

    shifted(wrkv_ref, mu_rkv_ref, carry_rkv, rkv_ref)
    shifted(wlora_ref, mu_lora_ref, carry_lora, lora_ref)


def _inproj(x2, wqkv, wrkv, wlora, mu_rkv, mu_lora, batch, tm=512):
    m, d = x2.shape
    nq, nr, nl = wqkv.shape[1], wrkv.shape[1], wlora.shape[1]
    tm = min(tm, m)
    full = lambda shape: pl.BlockSpec(shape, lambda i: (0, 0))
    return pl.pallas_call(
        functools.partial(_inproj_kernel, batch=batch),
        grid=(m // tm,),
        in_specs=[pl.BlockSpec((tm, d), lambda i: (i, 0)),
                  full((d, nq)), full((d, nr)), full((d, nl)), full((1, nr)), full((1, nl))],
        out_specs=[pl.BlockSpec((batch, tm // batch, nq), lambda i: (0, i, 0)),
                   pl.BlockSpec((tm, nr), lambda i: (i, 0)),
                   pl.BlockSpec((tm, nl), lambda i: (i, 0))],
        out_shape=[jax.ShapeDtypeStruct((batch, m // batch, nq), F32),
                   jax.ShapeDtypeStruct((m, nr), F32),
                   jax.ShapeDtypeStruct((m, nl), F32)],
        scratch_shapes=[pltpu.VMEM((batch, nr), F32), pltpu.VMEM((batch, nl), F32),
                        pltpu.VMEM((nq // LANES, tm, LANES), F32)],
        compiler_params=_params("arbitrary"),
        name="inproj",
    )(x2, wqkv, wrkv, wlora, mu_rkv, mu_lora)


CLASS_COPY_ROWS = 256


def _attn_block(q, kw, vw, offset):
    win = kw.shape[0]
    dist = (lax.broadcasted_iota(jnp.int32, (ATT_BLOCK, win), 0)
            - lax.broadcasted_iota(jnp.int32, (ATT_BLOCK, win), 1)) + offset
    ok = (dist >= 0) & (dist <= ATT_BLOCK)
    lane = lax.broadcasted_iota(jnp.int32, (ATT_BLOCK, LANES), 1)
    nt = (((1,), (1,)), ((), ()))
    o_acc = m_acc = l_acc = jnp.zeros((ATT_BLOCK, LANES), F32)
    for h in range(LANES // ATT_HEAD_DIM):
        in_head = (lane >= h * ATT_HEAD_DIM) & (lane < (h + 1) * ATT_HEAD_DIM)
        qh = jnp.where(in_head, q, jnp.zeros_like(q))
        sc = jnp.where(ok, lax.dot_general(qh, kw, nt, preferred_element_type=F32), NEG_BIG)
        m = jnp.max(sc, axis=-1, keepdims=True)
        p = jnp.exp(sc - m)
        l = jnp.sum(p, axis=-1, keepdims=True)
        o = jnp.dot(p.astype(BF16), vw, preferred_element_type=F32)
        o_acc = jnp.where(in_head, o, o_acc)
        m_acc = jnp.where(in_head, m, m_acc)
        l_acc = jnp.where(in_head, l, l_acc)
    return o_acc, m_acc, l_acc


def _attn_kernel(q_ref, k_ref, v_ref, out_ref, acc_ref, m_ref, l_ref, qc_ref, kc_ref, vc_ref):
    seq = q_ref.shape[0]
    win = 2 * ATT_BLOCK

    def starts(n):
        q_start = pl.multiple_of(n * ATT_BLOCK, ATT_BLOCK)
        k_start = pl.multiple_of(jnp.maximum(n - 1, 0) * ATT_BLOCK, ATT_BLOCK)
        return q_start, k_start

    def dense_block(n, carry):
        q_start, k_start = starts(n)
        o, m, l = _attn_block(q_ref[pl.ds(q_start, ATT_BLOCK), :].astype(BF16),
                              k_ref[pl.ds(k_start, win), :].astype(BF16),
                              v_ref[pl.ds(k_start, win), :].astype(BF16), q_start - k_start)
        acc_ref[pl.ds(q_start, ATT_BLOCK), :] = o
        m_ref[pl.ds(q_start, ATT_BLOCK), :] = m
        l_ref[pl.ds(q_start, ATT_BLOCK), :] = l
        return carry

    lax.fori_loop(0, seq // ATT_BLOCK, dense_block, 0, unroll=2)

    for dil in DILATIONS[1:]:
        length = seq // dil

        def one_class(c, carry, dil=dil, length=length):
            for j in range(length // CLASS_COPY_ROWS):
                src = pl.ds(c + j * CLASS_COPY_ROWS * dil, CLASS_COPY_ROWS, stride=dil)
                dst = pl.ds(j * CLASS_COPY_ROWS, CLASS_COPY_ROWS)
                qc_ref[dst, :] = q_ref[src, :].astype(BF16)
                kc_ref[dst, :] = k_ref[src, :].astype(BF16)
                vc_ref[dst, :] = v_ref[src, :].astype(BF16)

            def class_block(n, carry2):
                q_start, k_start = starts(n)
                o, m, l = _attn_block(qc_ref[pl.ds(q_start, ATT_BLOCK), :],
                                      kc_ref[pl.ds(k_start, win), :],
                                      vc_ref[pl.ds(k_start, win), :], q_start - k_start)
                rows = pl.ds(c + q_start * dil, ATT_BLOCK, stride=dil)
                m_old = m_ref[rows, :]
                m_new = jnp.maximum(m_old, m)
                w_old = jnp.exp(m_old - m_new)
                w_new = jnp.exp(m - m_new)
                acc_ref[rows, :] = acc_ref[rows, :] * w_old + o * w_new
                l_ref[rows, :] = l_ref[rows, :] * w_old + l * w_new
                m_ref[rows, :] = m_new
                return carry2

            lax.fori_loop(0, length // ATT_BLOCK, class_block, 0, unroll=2)
            return carry

        lax.fori_loop(0, dil, one_class, 0)

    out_ref[...] = acc_ref[...] / l_ref[...]


def _attention(qkv_b):
    b, s, c3 = qkv_b.shape
    a = c3 // 3
    npair = a // LANES
    assert s % (DILATIONS[-1] * 2 * ATT_BLOCK) == 0 and (s // DILATIONS[1]) % CLASS_COPY_ROWS == 0
    class_rows = s // DILATIONS[1]
    blk = lambda off: pl.BlockSpec((None, s, LANES), lambda bi, hp: (bi, 0, off + hp))
    return pl.pallas_call(
        _attn_kernel,
        grid=(b, npair),
        in_specs=[blk(0), blk(npair), blk(2 * npair)],
        out_specs=blk(0),
        out_shape=jax.ShapeDtypeStruct((b, s, a), F32),
        scratch_shapes=[pltpu.VMEM((s, LANES), F32)] * 3 + [pltpu.VMEM((class_rows, LANES), BF16)] * 3,
        compiler_params=_params("arbitrary", "arbitrary"),
        name="dilated_attn",
    )(qkv_b, qkv_b, qkv_b)


def _segment_sum(x, seg_ref):
    return jnp.dot(x, seg_ref[...], preferred_element_type=F32, precision=lax.Precision.HIGHEST)


KEY_HALVES = 2


def _chunk_transpose(parts, chunk):
    n = len(parts)
    lane = lax.broadcasted_iota(jnp.int32, parts[0].shape, 1)
    s = n // 2
    while s >= 1:
        swap = ((lane // chunk) & s) != 0
        new = list(parts)
        for j in range(n):
            if j & s == 0:
                lo, hi = parts[j], parts[j + s]
                new[j] = jnp.where(swap, pltpu.roll(hi, s * chunk, axis=1), lo)
                new[j + s] = jnp.where(swap, hi, pltpu.roll(lo, LANES - s * chunk, axis=1))
        parts = new
        s //= 2
    return parts


def _keys_to_lanes(p, heads, batch):
    q = p.T
    half = RWKV_HEAD_DIM // KEY_HALVES
    parts = [q[(h * KEY_HALVES + kh) * half:(h * KEY_HALVES + kh + 1) * half, :]
             for kh in range(KEY_HALVES) for h in range(heads)]
    return _chunk_transpose(parts, batch)


def _values_to_lanes(p, heads, batch):
    q = p.T
    parts = [q[h * RWKV_HEAD_DIM:(h + 1) * RWKV_HEAD_DIM, :]
             for _ in range(KEY_HALVES) for h in range(heads)]
    return _chunk_transpose(parts, batch)


def _values_from_lanes(parts, heads, batch):
    per_half = _chunk_transpose(parts, batch)
    per_head = [per_half[h] + per_half[heads + h] for h in range(heads)]
    return jnp.concatenate(per_head, axis=0).T


def _rwkv_prep_kernel(rkv_ref, lora_ref, vfirst_ref, decay_up_ref, aaa_up_ref, gate_up_ref,
                      vres_up_ref, vec_ref, seg_ref,
                      r_ref, w_ref, k_ref, a_ref, b_ref, vl_ref, v_ref, g_ref, bonus_ref,
                      *, has_vres, batch):
    wdt = rkv_ref.shape[1] // 3
    r = rkv_ref[:, 0:wdt]
    k = rkv_ref[:, wdt:2 * wdt]
    v = rkv_ref[:, 2 * wdt:3 * wdt]
    wa = lora_ref[:, 0:LANES]
    gd = lora_ref[:, LANES:2 * LANES]
    decay_base, aaa_base, vres_base = vec_ref[0:1, :], vec_ref[1:2, :], vec_ref[2:3, :]
    k_k, k_a, r_k = vec_ref[3:4, :], vec_ref[4:5, :], vec_ref[5:6, :]

    z = -(decay_base + jnp.dot(jnp.tanh(wa).astype(BF16), decay_up_ref[...],
                               preferred_element_type=F32))
    softplus = jnp.maximum(z, 0.0) + jnp.log(1.0 + jnp.exp(-jnp.abs(z)))
    w_log = -softplus - 0.5
    w = jnp.exp(-jnp.exp(w_log))
    a = jax.nn.sigmoid(aaa_base + jnp.dot(wa.astype(BF16), aaa_up_ref[...],
                                          preferred_element_type=F32))
    g_ref[...] = jnp.dot(jax.nn.sigmoid(gd).astype(BF16), gate_up_ref[...],
                         preferred_element_type=F32)
    if has_vres:
        vd = lora_ref[:, 2 * LANES:3 * LANES]
        mix = jax.nn.sigmoid(vres_base + jnp.dot(vd.astype(BF16), vres_up_ref[...],
                                                 preferred_element_type=F32))
        v = v + (vfirst_ref[...] - v) * mix
    kk = k * k_k
    norm = jnp.sqrt(_segment_sum(kk * kk, seg_ref))
    kk = kk / jnp.maximum(norm, 1e-12)
    k2 = k * (1.0 + (a - 1.0) * k_a)
    v_ref[...] = v
    bonus_ref[...] = _segment_sum(r * k2 * r_k, seg_ref) * v

    heads = wdt // RWKV_HEAD_DIM
    group = LANES
    steps = group // batch
    for gi in range(rkv_ref.shape[0] // group):
        rows = slice(gi * group, (gi + 1) * group)
        for ref, val in ((r_ref, r), (w_ref, w), (k_ref, k2), (a_ref, -kk), (b_ref, kk * a)):
            for t, part in enumerate(_keys_to_lanes(val[rows], heads, batch)):
                ref[gi * steps + t] = part
        for t, part in enumerate(_values_to_lanes(v[rows], heads, batch)):
            vl_ref[gi * steps + t] = part


def _rwkv_prep(rkv, lora, vfirst, decay_up, aaa_up, gate_up, vres_up, vecs, seg, has_vres, batch, tm=256):
    m = rkv.shape[0]
    wdt = rkv.shape[1] // 3
    heads = wdt // RWKV_HEAD_DIM
    assert KEY_HALVES * heads * batch == LANES, "recurrence layout needs 2*heads*batch == 128 lanes"
    tm = min(tm, m)
    assert tm % LANES == 0
    steps = tm // batch
    seq = m // batch
    half = RWKV_HEAD_DIM // KEY_HALVES
    rows = lambda n: pl.BlockSpec((tm, n), lambda i: (i, 0))
    full = lambda arr: pl.BlockSpec(arr.shape, lambda i: (0, 0))
    key_spec = pl.BlockSpec((steps, half, LANES), lambda i: (i, 0, 0))
    val_spec = pl.BlockSpec((steps, RWKV_HEAD_DIM, LANES), lambda i: (i, 0, 0))
    key_out = jax.ShapeDtypeStruct((seq, half, LANES), F32)
    val_out = jax.ShapeDtypeStruct((seq, RWKV_HEAD_DIM, LANES), F32)
    nat_out = jax.ShapeDtypeStruct((m, wdt), F32)
    return pl.pallas_call(
        functools.partial(_rwkv_prep_kernel, has_vres=has_vres, batch=batch),
        grid=(m // tm,),
        in_specs=[rows(3 * wdt), rows(LORA_PAD), rows(wdt), full(decay_up), full(aaa_up),
                  full(gate_up), full(vres_up), full(vecs), full(seg)],
        out_specs=[key_spec] * 5 + [val_spec] + [rows(wdt)] * 3,
        out_shape=[key_out] * 5 + [val_out] + [nat_out] * 3,
        compiler_params=_params("arbitrary"),
        name="rwkv_prep",
    )(rkv, lora, vfirst, decay_up, aaa_up, gate_up, vres_up, vecs, seg)


def _rwkv_scan_kernel(a_ref, a_next_ref, w_ref, b_ref, k_ref, r_ref, v_ref, y_ref, state_ref, sa_ref):
    @pl.when(pl.program_id(0) == 0)
    def _():
        state_ref[...] = jnp.zeros_like(state_ref)
        sa_ref[...] = jnp.zeros_like(sa_ref)

    steps, nk, _ = a_ref.shape
    nv = v_ref.shape[1]

    def step(t, sa_half, next_a_row):
        sa = sa_half + pltpu.roll(sa_half, LANES // KEY_HALVES, axis=1)
        row = lambda ref, kx: jnp.broadcast_to(ref[t, pl.ds(kx, 1), :], (nv, LANES))
        v_t = v_ref[t]
        acc_y = None
        acc_sa = None
        for kx in range(nk):
            new = state_ref[kx] * row(w_ref, kx) + v_t * row(k_ref, kx) + sa * row(b_ref, kx)
            state_ref[kx] = new
            term_y = new * row(r_ref, kx)
            term_sa = new * jnp.broadcast_to(next_a_row(kx), (nv, LANES))
            acc_y = term_y if acc_y is None else acc_y + term_y
            acc_sa = term_sa if acc_sa is None else acc_sa + term_sa
        y_ref[t] = acc_y
        return acc_sa

    def body(t, sa_half):
        return step(t, sa_half, lambda kx: a_ref[t + 1, pl.ds(kx, 1), :])

    sa_half = lax.fori_loop(0, steps - 1, body, sa_ref[...])
    sa_ref[...] = step(steps - 1, sa_half, lambda kx: a_next_ref[0, pl.ds(kx, 1), :])


def _rwkv_scan(a_t, w_t, b_t, k_t, r_t, v_t, steps=32):
    s, nk, _ = a_t.shape
    nv = v_t.shape[1]
    steps = min(steps, s)
    n_blocks = s // steps
    kin = pl.BlockSpec((steps, nk, LANES), lambda i: (i, 0, 0))
    knext = pl.BlockSpec((1, nk, LANES), lambda i: (jnp.minimum((i + 1) * steps, s - 1), 0, 0))
    vin = pl.BlockSpec((steps, nv, LANES), lambda i: (i, 0, 0))
    return pl.pallas_call(
        _rwkv_scan_kernel,
        grid=(n_blocks,),
        in_specs=[kin, knext] + [kin] * 4 + [vin],
        out_specs=vin,
        out_shape=jax.ShapeDtypeStruct((s, nv, LANES), F32),
        scratch_shapes=[pltpu.VMEM((nk, nv, LANES), F32), pltpu.VMEM((nv, LANES), F32)],
        compiler_params=_params("arbitrary"),
        name="rwkv_scan",
    )(a_t, a_t, w_t, b_t, k_t, r_t, v_t)


def _layer_norm(z, g, b):
    mu = jnp.mean(z, axis=-1, keepdims=True)
    zc = z - mu
    var = jnp.mean(zc * zc, axis=-1, keepdims=True)
    return zc * lax.rsqrt(var + LN_EPS) * g + b


def _outproj_kernel(att_ref, y_ref, bonus_ref, g_ref, x_ref, wa_ref, wr_ref, vec_ref, ln_ref,
                    seg_ref, out_ref, outb_ref, att_rows, *, alpha, batch):
    n_tiles = att_rows.shape[0]
    for bi in range(batch):
        for j in range(n_tiles):
            att_rows[j, pl.ds(bi, att_ref.shape[1], stride=batch), :] = att_ref[bi, :, j * LANES:(j + 1) * LANES]
    att = jnp.concatenate([att_rows[j] for j in range(n_tiles)], axis=1)

    heads = bonus_ref.shape[1] // RWKV_HEAD_DIM
    steps = LANES // batch
    y = jnp.concatenate(
        [_values_from_lanes([y_ref[gi * steps + t] for t in range(steps)], heads, batch)
         for gi in range(y_ref.shape[0] // steps)], axis=0)
    inv_n = 1.0 / RWKV_HEAD_DIM
    mu = _segment_sum(y, seg_ref) * inv_n
    yc = y - mu
    var = _segment_sum(yc * yc, seg_ref) * inv_n
    yn = yc * lax.rsqrt(var + GN_EPS) * vec_ref[0:1, :] + vec_ref[1:2, :]
    rw = (yn + bonus_ref[...]) * g_ref[...]

    mix = (jnp.dot(att.astype(BF16), wa_ref[...], preferred_element_type=F32)
           + jnp.dot(rw.astype(BF16), wr_ref[...], preferred_element_type=F32))
    out = _layer_norm(alpha * x_ref[...] + mix, ln_ref[0:1, :], ln_ref[1:2, :])
    out_ref[...] = out
    outb_ref[...] = out.astype(BF16)


def _outproj(att_b, y_lanes, bonus, g, x2, wa, wr, gn_vec, ln_vec, seg, alpha, tm=256):
    m, d = x2.shape
    batch, _, a = att_b.shape
    tm = min(tm, m)
    assert tm % LANES == 0
    rows = lambda n: pl.BlockSpec((tm, n), lambda i: (i, 0))
    full = lambda arr: pl.BlockSpec(arr.shape, lambda i: (0, 0))
    att_spec = pl.BlockSpec((batch, tm // batch, a), lambda i: (0, i, 0))
    y_spec = pl.BlockSpec((tm // batch,) + y_lanes.shape[1:], lambda i: (i, 0, 0))
    return pl.pallas_call(
        functools.partial(_outproj_kernel, alpha=alpha, batch=batch),
        grid=(m // tm,),
        in_specs=[att_spec, y_spec] + [rows(a)] * 2
                 + [rows(d), full(wa), full(wr), full(gn_vec), full(ln_vec), full(seg)],
        out_specs=[rows(d), rows(d)],
        out_shape=[jax.ShapeDtypeStruct((m, d), F32), jax.ShapeDtypeStruct((m, d), BF16)],
        scratch_shapes=[pltpu.VMEM((a // LANES, tm, LANES), F32)],
        compiler_params=_params("arbitrary"),
        name="outproj_ln",
    )(att_b, y_lanes, bonus, g, x2, wa, wr, gn_vec, ln_vec, seg)


def _swiglu_kernel(expert_ref, x_ref, wg_ref, wu_ref, wd_ref, out_ref, acc_ref):
    j = pl.program_id(1)

    @pl.when(j == 0)
    def _():
        acc_ref[...] = jnp.zeros_like(acc_ref)

    x = x_ref[...]
    gate = jnp.dot(x, wg_ref[0], preferred_element_type=F32)
    up = jnp.dot(x, wu_ref[0], preferred_element_type=F32)
    h = (gate * jax.nn.sigmoid(gate) * up).astype(BF16)
    acc_ref[...] += jnp.dot(h, wd_ref[0], preferred_element_type=F32)

    @pl.when(j == pl.num_programs(1) - 1)
    def _():
        out_ref[...] = acc_ref[...]


def _swiglu(tile_expert, xb, wg, wu, wd, tm, fc):
    m, d = xb.shape
    f = wg.shape[2]
    return pl.pallas_call(
        _swiglu_kernel,
        grid_spec=pltpu.PrefetchScalarGridSpec(
            num_scalar_prefetch=1,
            grid=(m // tm, f // fc),
            in_specs=[pl.BlockSpec((tm, d), lambda i, j, e: (i, 0)),
                      pl.BlockSpec((1, d, fc), lambda i, j, e: (e[i], 0, j)),
                      pl.BlockSpec((1, d, fc), lambda i, j, e: (e[i], 0, j)),
                      pl.BlockSpec((1, fc, d), lambda i, j, e: (e[i], j, 0))],
            out_specs=pl.BlockSpec((tm, d), lambda i, j, e: (i, 0)),
            scratch_shapes=[pltpu.VMEM((tm, d), F32)]),
        out_shape=jax.ShapeDtypeStruct((m, d), F32),
        compiler_params=_params("arbitrary", "arbitrary"),
        name="swiglu",
    )(tile_expert, xb, wg, wu, wd)


def _ffn_chunk(f):
    for fc in (512, 256, 128):
        if f % fc == 0:
            return fc
    raise ValueError(f"feed-forward width {f} is not a multiple of {LANES}")


def _router_kernel(x_ref, wr_ref, out_ref):
    logits = jnp.dot(x_ref[...], wr_ref[...], preferred_element_type=F32,
                     precision=lax.Precision.HIGHEST)
    lane = lax.broadcasted_iota(jnp.int32, logits.shape, 1)
    logits = jnp.where(lane < N_EXPERTS, logits, NEG_BIG)
    v1 = jnp.max(logits, axis=-1, keepdims=True)
    i1 = jnp.min(jnp.where(logits == v1, lane, LANES), axis=-1, keepdims=True)
    rest = jnp.where(lane == i1, NEG_BIG, logits)
    v2 = jnp.max(rest, axis=-1, keepdims=True)
    i2 = jnp.min(jnp.where(rest == v2, lane, LANES), axis=-1, keepdims=True)
    e2 = jnp.exp(v2 - v1)
    g1 = 1.0 / (1.0 + e2)
    g2 = e2 / (1.0 + e2)
    out_ref[...] = jnp.where(lane == 0, i1.astype(F32),
                             jnp.where(lane == 1, i2.astype(F32),
                                       jnp.where(lane == 2, g1, jnp.where(lane == 3, g2, 0.0))))


def _router(x2, wr_pad, tm=512):
    m, d = x2.shape
    tm = min(tm, m)
    return pl.pallas_call(
        _router_kernel,
        grid=(m // tm,),
        in_specs=[pl.BlockSpec((tm, d), lambda i: (i, 0)), pl.BlockSpec(wr_pad.shape, lambda i: (0, 0))],
        out_specs=pl.BlockSpec((tm, LANES), lambda i: (i, 0)),
        out_shape=jax.ShapeDtypeStruct((m, LANES), F32),
        compiler_params=_params("arbitrary"),
        name="router_top2",
    )(x2, wr_pad)


def _add_ln_kernel(x_ref, f_ref, ln_ref, out_ref, *, alpha):
    out_ref[...] = _layer_norm(alpha * x_ref[...] + f_ref[...], ln_ref[0:1, :], ln_ref[1:2, :])


def _moe_add_ln_kernel(x_ref, y1_ref, y2_ref, route_ref, ln_ref, out_ref, *, alpha):
    g1 = route_ref[:, 2:3]
    g2 = route_ref[:, 3:4]
    f = g1 * y1_ref[...] + g2 * y2_ref[...]
    out_ref[...] = _layer_norm(alpha * x_ref[...] + f, ln_ref[0:1, :], ln_ref[1:2, :])


def _add_ln(x2, f, ln_vec, alpha, tm=512):
    m, d = x2.shape
    tm = min(tm, m)
    rows = pl.BlockSpec((tm, d), lambda i: (i, 0))
    return pl.pallas_call(
        functools.partial(_add_ln_kernel, alpha=alpha),
        grid=(m // tm,),
        in_specs=[rows, rows, pl.BlockSpec(ln_vec.shape, lambda i: (0, 0))],
        out_specs=rows,
        out_shape=jax.ShapeDtypeStruct((m, d), F32),
        compiler_params=_params("arbitrary"),
        name="add_ln",
    )(x2, f, ln_vec)


def _moe_add_ln(x2, y1, y2, route, ln_vec, alpha, tm=512):
    m, d = x2.shape
    tm = min(tm, m)
    rows = pl.BlockSpec((tm, d), lambda i: (i, 0))
    return pl.pallas_call(
        functools.partial(_moe_add_ln_kernel, alpha=alpha),
        grid=(m // tm,),
        in_specs=[rows, rows, rows, pl.BlockSpec((tm, LANES), lambda i: (i, 0)),
                  pl.BlockSpec(ln_vec.shape, lambda i: (0, 0))],
        out_specs=rows,
        out_shape=jax.ShapeDtypeStruct((m, d), F32),
        compiler_params=_params("arbitrary"),
        name="moe_add_ln",
    )(x2, y1, y2, route, ln_vec)


def _moe(x2, xb, router_w, wg, wu, wd, first_expert, n_exp, ln_vec, alpha, tm):
    m, d = x2.shape
    wr_pad = jnp.pad(router_w, ((0, 0), (0, LANES - n_exp)))
    route = _router(x2, wr_pad)
    idx = route[:, 0:2].astype(jnp.int32).reshape(-1)
    experts = jnp.arange(n_exp, dtype=jnp.int32)
    counts = jnp.sum((idx[:, None] == experts[None, :]).astype(jnp.int32), axis=0)
    padded = ((counts + tm - 1) // tm) * tm
    pad_end = jnp.cumsum(padded)
    pad_start = pad_end - padded
    start = jnp.cumsum(counts) - counts
    order = jnp.argsort(idx, stable=True)
    rank = jnp.argsort(order)
    slot_pos = pad_start[idx] + (rank - start[idx])
    rows_total = 2 * m + n_exp * tm
    tile_start = jnp.arange(rows_total // tm, dtype=jnp.int32) * tm
    tile_expert = jnp.minimum(jnp.sum((tile_start[:, None] >= pad_end[None, :]).astype(jnp.int32), axis=1),
                              n_exp - 1)
    row = jnp.arange(rows_total, dtype=jnp.int32)
    row_expert = jnp.repeat(tile_expert, tm)
    sorted_pos = jnp.clip(row - pad_start[row_expert] + start[row_expert], 0, 2 * m - 1)
    src_token = order[sorted_pos] // 2
    xg = jnp.take(xb, src_token, axis=0)
    yg = _swiglu(tile_expert + first_expert, xg, wg, wu, wd, tm, _ffn_chunk(wg.shape[2]))
    pos = slot_pos.reshape(m, 2)
    y1 = jnp.take(yg, pos[:, 0], axis=0)
    y2 = jnp.take(yg, pos[:, 1], axis=0)
    return _moe_add_ln(x2, y1, y2, route, ln_vec, alpha)


def kernel(x, w_in, w_in_vres, shift_mu, shift_mu_vres, decay_up, decay_base, aaa_up, aaa_base, vres_up, vres_base, gate_up, k_k, k_a, r_k, lnx_g, lnx_b, w_out, ln1_g, ln1_b, ln2_g, ln2_b, ffn_w_gate, ffn_w_up, ffn_w_down, router, moe_w_gate, moe_w_up, moe_w_down):
    b, s, d = x.shape
    depth = w_in.shape[0]
    wdt = decay_up.shape[2]
    att = w_out.shape[1] - wdt
    alpha = float((2 * depth) ** 0.25)
    m = b * s
    ffn_tm = min(512, m)

    seg = (jnp.arange(wdt)[:, None] // RWKV_HEAD_DIM == jnp.arange(wdt)[None, :] // RWKV_HEAD_DIM).astype(F32)
    q_scale = jnp.concatenate([jnp.full((att,), ATT_HEAD_DIM ** -0.5, F32), jnp.ones((2 * att,), F32)])
    zero_tiles = jnp.zeros((m // ffn_tm,), jnp.int32)
    ffn_w = [w.astype(BF16) for w in (ffn_w_gate, ffn_w_up, ffn_w_down)]
    n_exp = moe_w_gate.shape[1]
    moe_w = [w.astype(BF16).reshape((-1,) + w.shape[2:]) for w in (moe_w_gate, moe_w_up, moe_w_down)]

    x2 = x.transpose(1, 0, 2).reshape(m, d)
    v_first = None
    for l in range(depth):
        has_vres = l > 0
        w_l = w_in[l]
        c0 = 3 * att
        wqkv = (w_l[:, :c0] * q_scale).astype(BF16)
        wrkv = w_l[:, c0:c0 + 3 * wdt].astype(BF16)
        lora_w = w_l[:, c0 + 3 * wdt:]
        lora_mu = shift_mu[l, 3 * wdt:]
        if has_vres:
            lora_w = jnp.concatenate([lora_w, w_in_vres[l - 1]], axis=1)
            lora_mu = jnp.concatenate([lora_mu, shift_mu_vres[l - 1]])
        n_lora = lora_w.shape[1]
        wlora = jnp.pad(lora_w, ((0, 0), (0, LORA_PAD - n_lora))).astype(BF16)
        mu_lora = jnp.pad(lora_mu, (0, LORA_PAD - n_lora)).reshape(1, LORA_PAD)
        mu_rkv = shift_mu[l, :3 * wdt].reshape(1, 3 * wdt)

        qkv, rkv, lora = _inproj(x2, wqkv, wrkv, wlora, mu_rkv, mu_lora, b)

        att_b = _attention(qkv)

        zpad = lambda w, rows_before: jnp.pad(
            w, ((rows_before, LANES - rows_before - w.shape[0]), (0, 0))).astype(BF16)
        decay_up_p = zpad(decay_up[l], 0)
        aaa_up_p = zpad(aaa_up[l], DECAY_LORA)
        gate_up_p = gate_up[l].astype(BF16)
        if has_vres:
            vres_up_p = zpad(vres_up[l - 1], 0)
            vres_b = vres_base[l - 1]
            vf_in = v_first
        else:
            vres_up_p = jnp.zeros((LANES, wdt), BF16)
            vres_b = jnp.zeros((wdt,), F32)
            vf_in = rkv[:, 2 * wdt:]
        vecs = jnp.stack([decay_base[l], aaa_base[l], vres_b, k_k[l], k_a[l], r_k[l],
                          jnp.zeros((wdt,), F32), jnp.zeros((wdt,), F32)])
        r_, w_, k_, a_, b_, vl_, v_, g_, bonus = _rwkv_prep(
            rkv, lora, vf_in, decay_up_p, aaa_up_p, gate_up_p, vres_up_p, vecs, seg, has_vres, b)
        if not has_vres:
            v_first = v_
        y_lanes = _rwkv_scan(a_, w_, b_, k_, r_, vl_)

        gn_vec = jnp.stack([lnx_g[l], lnx_b[l]] + [jnp.zeros((wdt,), F32)] * 6)
        ln1 = jnp.stack([ln1_g[l], ln1_b[l]] + [jnp.zeros((d,), F32)] * 6)
        ln2 = jnp.stack([ln2_g[l], ln2_b[l]] + [jnp.zeros((d,), F32)] * 6)
        wo = w_out[l].astype(BF16)
        x2, xb = _outproj(att_b, y_lanes, bonus, g_, x2, wo[:att], wo[att:], gn_vec, ln1, seg, alpha)

        i = l // 2
        if l % 2 == 0:
            f = _swiglu(zero_tiles + i, xb, *ffn_w, ffn_tm, _ffn_chunk(ffn_w[0].shape[2]))
            x2 = _add_ln(x2, f, ln2, alpha)
        else:
            x2 = _moe(x2, xb, router[i], *moe_w, i * n_exp, n_exp, ln2, alpha, ffn_tm)
    return x2.reshape(s, b, d).transpose(1, 0, 2)
```

```python
import functools

import jax
import jax.numpy as jnp
from jax import lax
from jax.experimental import pallas as pl
from jax.experimental.pallas import tpu as pltpu

ATT_HEAD_DIM = 64
RWKV_HEAD_DIM = 64
DILATIONS = (1, 4, 16)
ATT_BLOCK = 128
DECAY_LORA = 64
AAA_LORA = 64
GATE_LORA = 128
VRES_LORA = 32
N_EXPERTS = 8
LN_EPS = 1e-5
GN_EPS = 64e-5

LANES = 128
LORA_PAD = 384
VMEM_LIMIT = 56 * 1024 * 1024
NEG_BIG = -1e30

F32 = jnp.float32
BF16 = jnp.bfloat16


def _params(*sem):
    return pltpu.CompilerParams(dimension_semantics=sem, vmem_limit_bytes=VMEM_LIMIT)


def _inproj_kernel(x_ref, wqkv_ref, wrkv_ref, wlora_ref, mu_rkv_ref, mu_lora_ref,
                   qkv_ref, rkv_ref, lora_ref, carry_rkv, carry_lora, qkv_rows, *, batch):
    xb = x_ref[...].astype(BF16)
    qkv = jnp.dot(xb, wqkv_ref[...], preferred_element_type=F32)
    n_tiles, rows, _ = qkv_rows.shape
    for j in range(n_tiles):
        qkv_rows[j] = qkv[:, j * LANES:(j + 1) * LANES]
    for bi in range(batch):
        for j in range(n_tiles):
            qkv_ref[bi, :, j * LANES:(j + 1) * LANES] = qkv_rows[j, pl.ds(bi, rows // batch, stride=batch), :]

    first = pl.program_id(0) == 0

    def shifted(w_ref, mu_ref, carry_ref, out_ref):
        p = jnp.dot(xb, w_ref[...], preferred_element_type=F32)
        tm = p.shape[0]
        mu = mu_ref[...]
        out_ref[...] = p + (pltpu.roll(p, batch, axis=0) - p) * mu
        head = p[0:batch, :]
        prev_head = jnp.where(first, 0.0, carry_ref[...])
        out_ref[0:batch, :] = head + (prev_head - head) * mu
        carry_ref[...] = p[tm - batch:tm, :]

    shifted(wrkv_ref, mu_rkv_ref, carry_rkv, rkv_ref)
    shifted(wlora_ref, mu_lora_ref, carry_lora, lora_ref)


def _inproj(x2, wqkv, wrkv, wlora, mu_rkv, mu_lora, batch, tm=512):
    m, d = x2.shape
    nq, nr, nl = wqkv.shape[1], wrkv.shape[1], wlora.shape[1]
    tm = min(tm, m)
    full = lambda shape: pl.BlockSpec(shape, lambda i: (0, 0))
    return pl.pallas_call(
        functools.partial(_inproj_kernel, batch=batch),
        grid=(m // tm,),
        in_specs=[pl.BlockSpec((tm, d), lambda i: (i, 0)),
                  full((d, nq)), full((d, nr)), full((d, nl)), full((1, nr)), full((1, nl))],
        out_specs=[pl.BlockSpec((batch, tm // batch, nq), lambda i: (0, i, 0)),
                   pl.BlockSpec((tm, nr), lambda i: (i, 0)),
                   pl.BlockSpec((tm, nl), lambda i: (i, 0))],
        out_shape=[jax.ShapeDtypeStruct((batch, m // batch, nq), F32),
                   jax.ShapeDtypeStruct((m, nr), F32),
                   jax.ShapeDtypeStruct((m, nl), F32)],
        scratch_shapes=[pltpu.VMEM((batch, nr), F32), pltpu.VMEM((batch, nl), F32),
                        pltpu.VMEM((nq // LANES, tm, LANES), F32)],
        compiler_params=_params("arbitrary"),
        name="inproj",
    )(x2, wqkv, wrkv, wlora, mu_rkv, mu_lora)


CLASS_COPY_ROWS = 256
ATT_WINDOW = 2 * ATT_BLOCK
ATT_HEADS_PER_TILE = LANES // ATT_HEAD_DIM


def _attn_kernel(q_ref, k_ref, v_ref, out_ref, acc_ref, m_ref, l_ref, qc_ref, kc_ref, vc_ref,
                 s_scr, p_scr, ml_scr, bias_scr):
    seq = q_ref.shape[0]
    n_blocks = seq // ATT_BLOCK
    lane = lax.broadcasted_iota(jnp.int32, (ATT_BLOCK, LANES), 1)
    in_head = [(lane >= h * ATT_HEAD_DIM) & (lane < (h + 1) * ATT_HEAD_DIM)
               for h in range(ATT_HEADS_PER_TILE)]
    rel = (lax.broadcasted_iota(jnp.int32, (ATT_BLOCK, ATT_WINDOW), 0)
           - lax.broadcasted_iota(jnp.int32, (ATT_BLOCK, ATT_WINDOW), 1))
    nt = (((1,), (1,)), ((), ()))

    acc_ref[...] = jnp.zeros_like(acc_ref)
    l_ref[...] = jnp.zeros_like(l_ref)
    m_ref[...] = jnp.full(m_ref.shape, NEG_BIG, F32)
    s_scr[...] = jnp.zeros_like(s_scr)
    p_scr[...] = jnp.zeros_like(p_scr)
    ml_scr[...] = jnp.zeros_like(ml_scr)
    for first_window in range(2):
        dist = rel + first_window * ATT_BLOCK
        bias_scr[first_window] = jnp.where((dist >= 0) & (dist <= ATT_BLOCK), 0.0, NEG_BIG)

    for dil in DILATIONS:
        length = seq // dil
        per_class = length // ATT_BLOCK

        def extract(c, carry, dil=dil, length=length):
            for j in range(length // CLASS_COPY_ROWS):
                src = pl.ds(c + j * CLASS_COPY_ROWS * dil, CLASS_COPY_ROWS, stride=dil)
                dst = pl.ds(pl.multiple_of(c * length, CLASS_COPY_ROWS) + j * CLASS_COPY_ROWS,
                            CLASS_COPY_ROWS)
                qc_ref[dst, :] = q_ref[src, :].astype(BF16)
                kc_ref[dst, :] = k_ref[src, :].astype(BF16)
                vc_ref[dst, :] = v_ref[src, :].astype(BF16)
            return carry

        lax.fori_loop(0, dil, extract, 0)

        def locate(g, per_class=per_class):
            g = jnp.clip(g, 0, n_blocks - 1)
            first = (g % per_class) == 0
            q_start = pl.multiple_of(g * ATT_BLOCK, ATT_BLOCK)
            k_start = pl.multiple_of(jnp.where(first, g, g - 1) * ATT_BLOCK, ATT_BLOCK)
            return g, q_start, k_start

        def scores(g, slot, locate=locate):
            _, q_start, k_start = locate(g)
            q = qc_ref[pl.ds(q_start, ATT_BLOCK), :]
            kw = kc_ref[pl.ds(k_start, ATT_WINDOW), :]
            bias = bias_scr[(q_start - k_start) // ATT_BLOCK]
            for h in range(ATT_HEADS_PER_TILE):
                qh = jnp.where(in_head[h], q, jnp.zeros_like(q))
                s_scr[slot, h] = lax.dot_general(qh, kw, nt, preferred_element_type=F32) + bias

        def softmax(slot):
            m_acc = l_acc = jnp.zeros((ATT_BLOCK, LANES), F32)
            for h in range(ATT_HEADS_PER_TILE):
                sc = s_scr[slot, h]
                m = jnp.max(sc, axis=-1, keepdims=True)
                p = jnp.exp(sc - m)
                p_scr[slot, h] = p.astype(BF16)
                m_acc = jnp.where(in_head[h], m, m_acc)
                l_acc = jnp.where(in_head[h], jnp.sum(p, axis=-1, keepdims=True), l_acc)
            ml_scr[slot, 0] = m_acc
            ml_scr[slot, 1] = l_acc

        def accumulate(g, slot, dil=dil, per_class=per_class, locate=locate):
            valid = g >= 0
            g, _, k_start = locate(g)
            vw = vc_ref[pl.ds(k_start, ATT_WINDOW), :]
            o = jnp.zeros((ATT_BLOCK, LANES), F32)
            for h in range(ATT_HEADS_PER_TILE):
                o = jnp.where(in_head[h], jnp.dot(p_scr[slot, h], vw, preferred_element_type=F32), o)
            rows = pl.ds(g // per_class + (g % per_class) * (ATT_BLOCK * dil), ATT_BLOCK, stride=dil)
            m_blk = jnp.where(valid, ml_scr[slot, 0], NEG_BIG)
            m_old = m_ref[rows, :]
            m_new = jnp.maximum(m_old, m_blk)
            w_old = jnp.exp(m_old - m_new)
            w_new = jnp.where(valid, jnp.exp(m_blk - m_new), 0.0)
            acc_ref[rows, :] = acc_ref[rows, :] * w_old + o * w_new
            l_ref[rows, :] = l_ref[rows, :] * w_old + ml_scr[slot, 1] * w_new
            m_ref[rows, :] = m_new

        def pipeline_pair(it, carry, scores=scores, softmax=softmax, accumulate=accumulate):
            for parity in range(2):
                i = 2 * it + parity
                scores(i, parity)
                softmax(1 - parity)
                accumulate(i - 2, parity)
            return carry

        lax.fori_loop(0, (n_blocks + 2) // 2, pipeline_pair, 0)

    out_ref[...] = acc_ref[...] / l_ref[...]


def _attention(qkv_b):
    b, s, c3 = qkv_b.shape
    a = c3 // 3
    npair = a // LANES
    assert s % (DILATIONS[-1] * ATT_WINDOW) == 0 and (s // DILATIONS[-1]) % CLASS_COPY_ROWS == 0
    blk = lambda off: pl.BlockSpec((None, s, LANES), lambda bi, hp: (bi, 0, off + hp))
    pipe = (2, ATT_HEADS_PER_TILE, ATT_BLOCK, ATT_WINDOW)
    return pl.pallas_call(
        _attn_kernel,
        grid=(b, npair),
        in_specs=[blk(0), blk(npair), blk(2 * npair)],
        out_specs=blk(0),
        out_shape=jax.ShapeDtypeStruct((b, s, a), F32),
        scratch_shapes=[pltpu.VMEM((s, LANES), F32)] * 3 + [pltpu.VMEM((s, LANES), BF16)] * 3
                       + [pltpu.VMEM(pipe, F32), pltpu.VMEM(pipe, BF16),
                          pltpu.VMEM((2, 2, ATT_BLOCK, LANES), F32),
                          pltpu.VMEM((2, ATT_BLOCK, ATT_WINDOW), F32)],
        compiler_params=_params("arbitrary", "arbitrary"),
        name="dilated_attn",
    )(qkv_b, qkv_b, qkv_b)


def _segment_sum(x, seg_ref):
    seg = seg_ref[...]
    hi = x.astype(BF16)
    rest = x - hi.astype(F32)
    mid = rest.astype(BF16)
    lo = (rest - mid.astype(F32)).astype(BF16)
    return (jnp.dot(hi, seg, preferred_element_type=F32) + jnp.dot(mid, seg, preferred_element_type=F32)
            + jnp.dot(lo, seg, preferred_element_type=F32))


KEY_HALVES = 2


def _chunk_transpose(parts, chunk):
    n = len(parts)
    lane = lax.broadcasted_iota(jnp.int32, parts[0].shape, 1)
    s = n // 2
    while s >= 1:
        swap = ((lane // chunk) & s) != 0
        new = list(parts)
        for j in range(n):
            if j & s == 0:
                lo, hi = parts[j], parts[j + s]
                new[j] = jnp.where(swap, pltpu.roll(hi, s * chunk, axis=1), lo)
                new[j + s] = jnp.where(swap, hi, pltpu.roll(lo, LANES - s * chunk, axis=1))
        parts = new
        s //= 2
    return parts


def _keys_to_lanes(p, heads, batch):
    q = p.T
    half = RWKV_HEAD_DIM // KEY_HALVES
    parts = [q[(h * KEY_HALVES + kh) * half:(h * KEY_HALVES + kh + 1) * half, :]
             for kh in range(KEY_HALVES) for h in range(heads)]
    return _chunk_transpose(parts, batch)


def _values_to_lanes(p, heads, batch):
    q = p.T
    parts = [q[h * RWKV_HEAD_DIM:(h + 1) * RWKV_HEAD_DIM, :]
             for _ in range(KEY_HALVES) for h in range(heads)]
    return _chunk_transpose(parts, batch)


def _values_from_lanes(parts, heads, batch):
    per_half = _chunk_transpose(parts, batch)
    per_head = [per_half[h] + per_half[heads + h] for h in range(heads)]
    return jnp.concatenate(per_head, axis=0).T


def _rwkv_prep_kernel(rkv_ref, lora_ref, vfirst_ref, decay_up_ref, aaa_up_ref, gate_up_ref,
                      vres_up_ref, vec_ref, seg_ref,
                      r_ref, w_ref, k_ref, a_ref, b_ref, vl_ref, v_ref, g_ref, bonus_ref,
                      *, has_vres, batch):
    wdt = rkv_ref.shape[1] // 3
    r = rkv_ref[:, 0:wdt]
    k = rkv_ref[:, wdt:2 * wdt]
    v = rkv_ref[:, 2 * wdt:3 * wdt]
    wa = lora_ref[:, 0:LANES]
    gd = lora_ref[:, LANES:2 * LANES]
    decay_base, aaa_base, vres_base = vec_ref[0:1, :], vec_ref[1:2, :], vec_ref[2:3, :]
    k_k, k_a, r_k = vec_ref[3:4, :], vec_ref[4:5, :], vec_ref[5:6, :]

    z = -(decay_base + jnp.dot(jnp.tanh(wa).astype(BF16), decay_up_ref[...],
                               preferred_element_type=F32))
    softplus = jnp.maximum(z, 0.0) + jnp.log(1.0 + jnp.exp(-jnp.abs(z)))
    w_log = -softplus - 0.5
    w = jnp.exp(-jnp.exp(w_log))
    a = jax.nn.sigmoid(aaa_base + jnp.dot(wa.astype(BF16), aaa_up_ref[...],
                                          preferred_element_type=F32))
    g_ref[...] = jnp.dot(jax.nn.sigmoid(gd).astype(BF16), gate_up_ref[...],
                         preferred_element_type=F32)
    if has_vres:
        vd = lora_ref[:, 2 * LANES:3 * LANES]
        mix = jax.nn.sigmoid(vres_base + jnp.dot(vd.astype(BF16), vres_up_ref[...],
                                                 preferred_element_type=F32))
        v = v + (vfirst_ref[...] - v) * mix
    kk = k * k_k
    norm = jnp.sqrt(_segment_sum(kk * kk, seg_ref))
    kk = kk / jnp.maximum(norm, 1e-12)
    k2 = k * (1.0 + (a - 1.0) * k_a)
    v_ref[...] = v
    bonus_ref[...] = _segment_sum(r * k2 * r_k, seg_ref) * v

    heads = wdt // RWKV_HEAD_DIM
    group = LANES
    steps = group // batch
    for gi in range(rkv_ref.shape[0] // group):
        rows = slice(gi * group, (gi + 1) * group)
        for ref, val in ((r_ref, r), (w_ref, w), (k_ref, k2), (a_ref, -kk), (b_ref, kk * a)):
            for t, part in enumerate(_keys_to_lanes(val[rows], heads, batch)):
                ref[gi * steps + t] = part
        for t, part in enumerate(_values_to_lanes(v[rows], heads, batch)):
            vl_ref[gi * steps + t] = part


def _rwkv_prep(rkv, lora, vfirst, decay_up, aaa_up, gate_up, vres_up, vecs, seg, has_vres, batch, tm=256):
    m = rkv.shape[0]
    wdt = rkv.shape[1] // 3
    heads = wdt // RWKV_HEAD_DIM
    assert KEY_HALVES * heads * batch == LANES, "recurrence layout needs 2*heads*batch == 128 lanes"
    tm = min(tm, m)
    assert tm % LANES == 0
    steps = tm // batch
    seq = m // batch
    half = RWKV_HEAD_DIM // KEY_HALVES
    rows = lambda n: pl.BlockSpec((tm, n), lambda i: (i, 0))
    full = lambda arr: pl.BlockSpec(arr.shape, lambda i: (0, 0))
    key_spec = pl.BlockSpec((steps, half, LANES), lambda i: (i, 0, 0))
    val_spec = pl.BlockSpec((steps, RWKV_HEAD_DIM, LANES), lambda i: (i, 0, 0))
    key_out = jax.ShapeDtypeStruct((seq, half, LANES), F32)
    val_out = jax.ShapeDtypeStruct((seq, RWKV_HEAD_DIM, LANES), F32)
    nat_out = jax.ShapeDtypeStruct((m, wdt), F32)
    return pl.pallas_call(
        functools.partial(_rwkv_prep_kernel, has_vres=has_vres, batch=batch),
        grid=(m // tm,),
        in_specs=[rows(3 * wdt), rows(LORA_PAD), rows(wdt), full(decay_up), full(aaa_up),
                  full(gate_up), full(vres_up), full(vecs), full(seg)],
        out_specs=[key_spec] * 5 + [val_spec] + [rows(wdt)] * 3,
        out_shape=[key_out] * 5 + [val_out] + [nat_out] * 3,
        compiler_params=_params("arbitrary"),
        name="rwkv_prep",
    )(rkv, lora, vfirst, decay_up, aaa_up, gate_up, vres_up, vecs, seg)


def _rwkv_scan_kernel(a_ref, a_next_ref, w_ref, b_ref, k_ref, r_ref, v_ref, y_ref, state_ref, sa_ref):
    @pl.when(pl.program_id(0) == 0)
    def _():
        state_ref[...] = jnp.zeros_like(state_ref)
        sa_ref[...] = jnp.zeros_like(sa_ref)

    steps, nk, _ = a_ref.shape
    nv = v_ref.shape[1]

    def step(t, sa_half, next_a_row):
        sa = sa_half + pltpu.roll(sa_half, LANES // KEY_HALVES, axis=1)
        row = lambda ref, kx: jnp.broadcast_to(ref[t, pl.ds(kx, 1), :], (nv, LANES))
        v_t = v_ref[t]
        acc_y = None
        acc_sa = None
        for kx in range(nk):
            new = state_ref[kx] * row(w_ref, kx) + v_t * row(k_ref, kx) + sa * row(b_ref, kx)
            state_ref[kx] = new
            term_y = new * row(r_ref, kx)
            term_sa = new * jnp.broadcast_to(next_a_row(kx), (nv, LANES))
            acc_y = term_y if acc_y is None else acc_y + term_y
            acc_sa = term_sa if acc_sa is None else acc_sa + term_sa
        y_ref[t] = acc_y
        return acc_sa

    def body(t, sa_half):
        return step(t, sa_half, lambda kx: a_ref[t + 1, pl.ds(kx, 1), :])

    sa_half = lax.fori_loop(0, steps - 1, body, sa_ref[...])
    sa_ref[...] = step(steps - 1, sa_half, lambda kx: a_next_ref[0, pl.ds(kx, 1), :])


def _rwkv_scan(a_t, w_t, b_t, k_t, r_t, v_t, steps=32):
    s, nk, _ = a_t.shape
    nv = v_t.shape[1]
    steps = min(steps, s)
    n_blocks = s // steps
    kin = pl.BlockSpec((steps, nk, LANES), lambda i: (i, 0, 0))
    knext = pl.BlockSpec((1, nk, LANES), lambda i: (jnp.minimum((i + 1) * steps, s - 1), 0, 0))
    vin = pl.BlockSpec((steps, nv, LANES), lambda i: (i, 0, 0))
    return pl.pallas_call(
        _rwkv_scan_kernel,
        grid=(n_blocks,),
        in_specs=[kin, knext] + [kin] * 4 + [vin],
        out_specs=vin,
        out_shape=jax.ShapeDtypeStruct((s, nv, LANES), F32),
        scratch_shapes=[pltpu.VMEM((nk, nv, LANES), F32), pltpu.VMEM((nv, LANES), F32)],
        compiler_params=_params("arbitrary"),
        name="rwkv_scan",
    )(a_t, a_t, w_t, b_t, k_t, r_t, v_t)


def _layer_norm(z, g, b):
    mu = jnp.mean(z, axis=-1, keepdims=True)
    zc = z - mu
    var = jnp.mean(zc * zc, axis=-1, keepdims=True)
    return zc * lax.rsqrt(var + LN_EPS) * g + b


def _outproj_kernel(att_ref, y_ref, bonus_ref, g_ref, x_ref, wa_ref, wr_ref, vec_ref, ln_ref,
                    seg_ref, out_ref, outb_ref, att_rows, *, alpha, batch):
    n_tiles = att_rows.shape[0]
    for bi in range(batch):
        for j in range(n_tiles):
            att_rows[j, pl.ds(bi, att_ref.shape[1], stride=batch), :] = att_ref[bi, :, j * LANES:(j + 1) * LANES]
    att = jnp.concatenate([att_rows[j] for j in range(n_tiles)], axis=1)

    heads = bonus_ref.shape[1] // RWKV_HEAD_DIM
    steps = LANES // batch
    y = jnp.concatenate(
        [_values_from_lanes([y_ref[gi * steps + t] for t in range(steps)], heads, batch)
         for gi in range(y_ref.shape[0] // steps)], axis=0)
    inv_n = 1.0 / RWKV_HEAD_DIM
    mu = _segment_sum(y, seg_ref) * inv_n
    yc = y - mu
    var = _segment_sum(yc * yc, seg_ref) * inv_n
    yn = yc * lax.rsqrt(var + GN_EPS) * vec_ref[0:1, :] + vec_ref[1:2, :]
    rw = (yn + bonus_ref[...]) * g_ref[...]

    mix = (jnp.dot(att.astype(BF16), wa_ref[...], preferred_element_type=F32)
           + jnp.dot(rw.astype(BF16), wr_ref[...], preferred_element_type=F32))
    out = _layer_norm(alpha * x_ref[...] + mix, ln_ref[0:1, :], ln_ref[1:2, :])
    out_ref[...] = out
    outb_ref[...] = out.astype(BF16)


def _outproj(att_b, y_lanes, bonus, g, x2, wa, wr, gn_vec, ln_vec, seg, alpha, tm=256):
    m, d = x2.shape
    batch, _, a = att_b.shape
    tm = min(tm, m)
    assert tm % LANES == 0
    rows = lambda n: pl.BlockSpec((tm, n), lambda i: (i, 0))
    full = lambda arr: pl.BlockSpec(arr.shape, lambda i: (0, 0))
    att_spec = pl.BlockSpec((batch, tm // batch, a), lambda i: (0, i, 0))
    y_spec = pl.BlockSpec((tm // batch,) + y_lanes.shape[1:], lambda i: (i, 0, 0))
    return pl.pallas_call(
        functools.partial(_outproj_kernel, alpha=alpha, batch=batch),
        grid=(m // tm,),
        in_specs=[att_spec, y_spec] + [rows(a)] * 2
                 + [rows(d), full(wa), full(wr), full(gn_vec), full(ln_vec), full(seg)],
        out_specs=[rows(d), rows(d)],
        out_shape=[jax.ShapeDtypeStruct((m, d), F32), jax.ShapeDtypeStruct((m, d), BF16)],
        scratch_shapes=[pltpu.VMEM((a // LANES, tm, LANES), F32)],
        compiler_params=_params("arbitrary"),
        name="outproj_ln",
    )(att_b, y_lanes, bonus, g, x2, wa, wr, gn_vec, ln_vec, seg)


def _swiglu_kernel(expert_ref, x_ref, wg_ref, wu_ref, wd_ref, out_ref, acc_ref):
    j = pl.program_id(1)

    @pl.when(j == 0)
    def _():
        acc_ref[...] = jnp.zeros_like(acc_ref)

    x = x_ref[...]
    gate = jnp.dot(x, wg_ref[0], preferred_element_type=F32)
    up = jnp.dot(x, wu_ref[0], preferred_element_type=F32)
    h = (gate * jax.nn.sigmoid(gate) * up).astype(BF16)
    acc_ref[...] += jnp.dot(h, wd_ref[0], preferred_element_type=F32)

    @pl.when(j == pl.num_programs(1) - 1)
    def _():
        out_ref[...] = acc_ref[...]


def _swiglu(tile_expert, xb, wg, wu, wd, tm, fc):
    m, d = xb.shape
    f = wg.shape[2]
    return pl.pallas_call(
        _swiglu_kernel,
        grid_spec=pltpu.PrefetchScalarGridSpec(
            num_scalar_prefetch=1,
            grid=(m // tm, f // fc),
            in_specs=[pl.BlockSpec((tm, d), lambda i, j, e: (i, 0)),
                      pl.BlockSpec((1, d, fc), lambda i, j, e: (e[i], 0, j)),
                      pl.BlockSpec((1, d, fc), lambda i, j, e: (e[i], 0, j)),
                      pl.BlockSpec((1, fc, d), lambda i, j, e: (e[i], j, 0))],
            out_specs=pl.BlockSpec((tm, d), lambda i, j, e: (i, 0)),
            scratch_shapes=[pltpu.VMEM((tm, d), F32)]),
        out_shape=jax.ShapeDtypeStruct((m, d), F32),
        compiler_params=_params("arbitrary", "arbitrary"),
        name="swiglu",
    )(tile_expert, xb, wg, wu, wd)


def _ffn_chunk(f):
    for fc in (512, 256, 128):
        if f % fc == 0:
            return fc
    raise ValueError(f"feed-forward width {f} is not a multiple of {LANES}")


def _router_kernel(x_ref, wr_ref, out_ref):
    logits = jnp.dot(x_ref[...], wr_ref[...], preferred_element_type=F32,
                     precision=lax.Precision.HIGHEST)
    lane = lax.broadcasted_iota(jnp.int32, logits.shape, 1)
    logits = jnp.where(lane < N_EXPERTS, logits, NEG_BIG)
    v1 = jnp.max(logits, axis=-1, keepdims=True)
    i1 = jnp.min(jnp.where(logits == v1, lane, LANES), axis=-1, keepdims=True)
    rest = jnp.where(lane == i1, NEG_BIG, logits)
    v2 = jnp.max(rest, axis=-1, keepdims=True)
    i2 = jnp.min(jnp.where(rest == v2, lane, LANES), axis=-1, keepdims=True)
    e2 = jnp.exp(v2 - v1)
    g1 = 1.0 / (1.0 + e2)
    g2 = e2 / (1.0 + e2)
    out_ref[...] = jnp.where(lane == 0, i1.astype(F32),
                             jnp.where(lane == 1, i2.astype(F32),
                                       jnp.where(lane == 2, g1, jnp.where(lane == 3, g2, 0.0))))


def _router(x2, wr_pad, tm=512):
    m, d = x2.shape
    tm = min(tm, m)
    return pl.pallas_call(
        _router_kernel,
        grid=(m // tm,),
        in_specs=[pl.BlockSpec((tm, d), lambda i: (i, 0)), pl.BlockSpec(wr_pad.shape, lambda i: (0, 0))],
        out_specs=pl.BlockSpec((tm, LANES), lambda i: (i, 0)),
        out_shape=jax.ShapeDtypeStruct((m, LANES), F32),
        compiler_params=_params("arbitrary"),
        name="router_top2",
    )(x2, wr_pad)


def _add_ln_kernel(x_ref, f_ref, ln_ref, out_ref, *, alpha):
    out_ref[...] = _layer_norm(alpha * x_ref[...] + f_ref[...], ln_ref[0:1, :], ln_ref[1:2, :])


def _moe_add_ln_kernel(x_ref, y1_ref, y2_ref, route_ref, ln_ref, out_ref, *, alpha):
    g1 = route_ref[:, 2:3]
    g2 = route_ref[:, 3:4]
    f = g1 * y1_ref[...] + g2 * y2_ref[...]
    out_ref[...] = _layer_norm(alpha * x_ref[...] + f, ln_ref[0:1, :], ln_ref[1:2, :])


def _add_ln(x2, f, ln_vec, alpha, tm=512):
    m, d = x2.shape
    tm = min(tm, m)
    rows = pl.BlockSpec((tm, d), lambda i: (i, 0))
    return pl.pallas_call(
        functools.partial(_add_ln_kernel, alpha=alpha),
        grid=(m // tm,),
        in_specs=[rows, rows, pl.BlockSpec(ln_vec.shape, lambda i: (0, 0))],
        out_specs=rows,
        out_shape=jax.ShapeDtypeStruct((m, d), F32),
        compiler_params=_params("arbitrary"),
        name="add_ln",
    )(x2, f, ln_vec)


def _moe_add_ln(x2, y1, y2, route, ln_vec, alpha, tm=512):
    m, d = x2.shape
    tm = min(tm, m)
    rows = pl.BlockSpec((tm, d), lambda i: (i, 0))
    return pl.pallas_call(
        functools.partial(_moe_add_ln_kernel, alpha=alpha),
        grid=(m // tm,),
        in_specs=[rows, rows, rows, pl.BlockSpec((tm, LANES), lambda i: (i, 0)),
                  pl.BlockSpec(ln_vec.shape, lambda i: (0, 0))],
        out_specs=rows,
        out_shape=jax.ShapeDtypeStruct((m, d), F32),
        compiler_params=_params("arbitrary"),
        name="moe_add_ln",
    )(x2, y1, y2, route, ln_vec)


def _moe(x2, xb, router_w, wg, wu, wd, first_expert, n_exp, ln_vec, alpha, tm):
    m, d = x2.shape
    wr_pad = jnp.pad(router_w, ((0, 0), (0, LANES - n_exp)))
    route = _router(x2, wr_pad)
    idx = route[:, 0:2].astype(jnp.int32).reshape(-1)
    experts = jnp.arange(n_exp, dtype=jnp.int32)
    counts = jnp.sum((idx[:, None] == experts[None, :]).astype(jnp.int32), axis=0)
    padded = ((counts + tm - 1) // tm) * tm
    pad_end = jnp.cumsum(padded)
    pad_start = pad_end - padded
    start = jnp.cumsum(counts) - counts
    order = jnp.argsort(idx, stable=True)
    rank = jnp.argsort(order)
    slot_pos = pad_start[idx] + (rank - start[idx])
    rows_total = 2 * m + n_exp * tm
    tile_start = jnp.arange(rows_total // tm, dtype=jnp.int32) * tm
    tile_expert = jnp.minimum(jnp.sum((tile_start[:, None] >= pad_end[None, :]).astype(jnp.int32), axis=1),
                              n_exp - 1)
    row = jnp.arange(rows_total, dtype=jnp.int32)
    row_expert = jnp.repeat(tile_expert, tm)
    sorted_pos = jnp.clip(row - pad_start[row_expert] + start[row_expert], 0, 2 * m - 1)
    src_token = order[sorted_pos] // 2
    xg = jnp.take(xb, src_token, axis=0, mode="clip")
    yg = _swiglu(tile_expert + first_expert, xg, wg, wu, wd, tm, _ffn_chunk(wg.shape[2]))
    pos = slot_pos.reshape(m, 2)
    y1 = jnp.take(yg, pos[:, 0], axis=0, mode="clip")
    y2 = jnp.take(yg, pos[:, 1], axis=0, mode="clip")
    return _moe_add_ln(x2, y1, y2, route, ln_vec, alpha)


def kernel(x, w_in, w_in_vres, shift_mu, shift_mu_vres, decay_up, decay_base, aaa_up, aaa_base, vres_up, vres_base, gate_up, k_k, k_a, r_k, lnx_g, lnx_b, w_out, ln1_g, ln1_b, ln2_g, ln2_b, ffn_w_gate, ffn_w_up, ffn_w_down, router, moe_w_gate, moe_w_up, moe_w_down):
    b, s, d = x.shape
    depth = w_in.shape[0]
    wdt = decay_up.shape[2]
    att = w_out.shape[1] - wdt
    alpha = float((2 * depth) ** 0.25)
    m = b * s
    ffn_tm = min(512, m)

    seg = (jnp.arange(wdt)[:, None] // RWKV_HEAD_DIM == jnp.arange(wdt)[None, :] // RWKV_HEAD_DIM).astype(BF16)
    q_scale = jnp.concatenate([jnp.full((att,), ATT_HEAD_DIM ** -0.5, F32), jnp.ones((2 * att,), F32)])
    zero_tiles = jnp.zeros((m // ffn_tm,), jnp.int32)
    ffn_w = [w.astype(BF16) for w in (ffn_w_gate, ffn_w_up, ffn_w_down)]
    n_exp = moe_w_gate.shape[1]
    moe_w = [w.astype(BF16).reshape((-1,) + w.shape[2:]) for w in (moe_w_gate, moe_w_up, moe_w_down)]

    x2 = x.transpose(1, 0, 2).reshape(m, d)
    v_first = None
    for l in range(depth):
        has_vres = l > 0
        w_l = w_in[l]
        c0 = 3 * att
        wqkv = (w_l[:, :c0] * q_scale).astype(BF16)
        wrkv = w_l[:, c0:c0 + 3 * wdt].astype(BF16)
        lora_w = w_l[:, c0 + 3 * wdt:]
        lora_mu = shift_mu[l, 3 * wdt:]
        if has_vres:
            lora_w = jnp.concatenate([lora_w, w_in_vres[l - 1]], axis=1)
            lora_mu = jnp.concatenate([lora_mu, shift_mu_vres[l - 1]])
        n_lora = lora_w.shape[1]
        wlora = jnp.pad(lora_w, ((0, 0), (0, LORA_PAD - n_lora))).astype(BF16)
        mu_lora = jnp.pad(lora_mu, (0, LORA_PAD - n_lora)).reshape(1, LORA_PAD)
        mu_rkv = shift_mu[l, :3 * wdt].reshape(1, 3 * wdt)

        qkv, rkv, lora = _inproj(x2, wqkv, wrkv, wlora, mu_rkv, mu_lora, b)

        att_b = _attention(qkv)

        zpad = lambda w, rows_before: jnp.pad(
            w, ((rows_before, LANES - rows_before - w.shape[0]), (0, 0))).astype(BF16)
        decay_up_p = zpad(decay_up[l], 0)
        aaa_up_p = zpad(aaa_up[l], DECAY_LORA)
        gate_up_p = gate_up[l].astype(BF16)
        if has_vres:
            vres_up_p = zpad(vres_up[l - 1], 0)
            vres_b = vres_base[l - 1]
            vf_in = v_first
        else:
            vres_up_p = jnp.zeros((LANES, wdt), BF16)
            vres_b = jnp.zeros((wdt,), F32)
            vf_in = rkv[:, 2 * wdt:]
        vecs = jnp.stack([decay_base[l], aaa_base[l], vres_b, k_k[l], k_a[l], r_k[l],
                          jnp.zeros((wdt,), F32), jnp.zeros((wdt,), F32)])
        r_, w_, k_, a_, b_, vl_, v_, g_, bonus = _rwkv_prep(
            rkv, lora, vf_in, decay_up_p, aaa_up_p, gate_up_p, vres_up_p, vecs, seg, has_vres, b)
        if not has_vres:
            v_first = v_
        y_lanes = _rwkv_scan(a_, w_, b_, k_, r_, vl_)

        gn_vec = jnp.stack([lnx_g[l], lnx_b[l]] + [jnp.zeros((wdt,), F32)] * 6)
        ln1 = jnp.stack([ln1_g[l], ln1_b[l]] + [jnp.zeros((d,), F32)] * 6)
        ln2 = jnp.stack([ln2_g[l], ln2_b[l]] + [jnp.zeros((d,), F32)] * 6)
        wo = w_out[l].astype(BF16)
        x2, xb = _outproj(att_b, y_lanes, bonus, g_, x2, wo[:att], wo[att:], gn_vec, ln1, seg, alpha)

        i = l // 2
        if l % 2 == 0:
            f = _swiglu(zero_tiles + i, xb, *ffn_w, ffn_tm, _ffn_chunk(ffn_w[0].shape[2]))
            x2 = _add_ln(x2, f, ln2, alpha)
        else:
            x2 = _moe(x2, xb, router[i], *moe_w, i * n_exp, n_exp, ln2, alpha, ffn_tm)
    return x2.reshape(s, b, d).transpose(1, 0, 2)
```

```python
import functools

import jax
import jax.numpy as jnp
from jax import lax
from jax.experimental import pallas as pl
from jax.experimental.pallas import tpu as pltpu

ATT_HEAD_DIM = 64
RWKV_HEAD_DIM = 64
DILATIONS = (1, 4, 16)
ATT_BLOCK = 128
DECAY_LORA = 64
AAA_LORA = 64
GATE_LORA = 128
VRES_LORA = 32
N_EXPERTS = 8
LN_EPS = 1e-5
GN_EPS = 64e-5

LANES = 128
LORA_PAD = 384
VMEM_LIMIT = 56 * 1024 * 1024
NEG_BIG = -1e30

F32 = jnp.float32
BF16 = jnp.bfloat16


def _params(*sem):
    return pltpu.CompilerParams(dimension_semantics=sem, vmem_limit_bytes=VMEM_LIMIT)


def _inproj_kernel(x_ref, wqkv_ref, wrkv_ref, wlora_ref, mu_rkv_ref, mu_lora_ref,
                   qkv_ref, rkv_ref, lora_ref, carry_rkv, carry_lora, qkv_rows, *, batch):
    xb = x_ref[...].astype(BF16)
    qkv = jnp.dot(xb, wqkv_ref[...], preferred_element_type=F32)
    n_tiles, rows, _ = qkv_rows.shape
    for j in range(n_tiles):
        qkv_rows[j] = qkv[:, j * LANES:(j + 1) * LANES]
    for bi in range(batch):
        for j in range(n_tiles):
            qkv_ref[bi, :, j * LANES:(j + 1) * LANES] = qkv_rows[j, pl.ds(bi, rows // batch, stride=batch), :]

    first = pl.program_id(0) == 0

    def shifted(w_ref, mu_ref, carry_ref, out_ref):
        p = jnp.dot(xb, w_ref[...], preferred_element_type=F32)
        tm = p.shape[0]
        mu = mu_ref[...]
        out_ref[...] = p + (pltpu.roll(p, batch, axis=0) - p) * mu
        head = p[0:batch, :]
        prev_head = jnp.where(first, 0.0, carry_ref[...])
        out_ref[0:batch, :] = head + (prev_head - head) * mu
        carry_ref[...] = p[tm - batch:tm, :]

    shifted(wrkv_ref, mu_rkv_ref, carry_rkv, rkv_ref)
    shifted(wlora_ref, mu_lora_ref, carry_lora, lora_ref)


def _inproj(x2, wqkv, wrkv, wlora, mu_rkv, mu_lora, batch, tm=512):
    m, d = x2.shape
    nq, nr, nl = wqkv.shape[1], wrkv.shape[1], wlora.shape[1]
    tm = min(tm, m)
    full = lambda shape: pl.BlockSpec(shape, lambda i: (0, 0))
    return pl.pallas_call(
        functools.partial(_inproj_kernel, batch=batch),
        grid=(m // tm,),
        in_specs=[pl.BlockSpec((tm, d), lambda i: (i, 0)),
                  full((d, nq)), full((d, nr)), full((d, nl)), full((1, nr)), full((1, nl))],
        out_specs=[pl.BlockSpec((batch, tm // batch, nq), lambda i: (0, i, 0)),
                   pl.BlockSpec((tm, nr), lambda i: (i, 0)),
                   pl.BlockSpec((tm, nl), lambda i: (i, 0))],
        out_shape=[jax.ShapeDtypeStruct((batch, m // batch, nq), F32),
                   jax.ShapeDtypeStruct((m, nr), F32),
                   jax.ShapeDtypeStruct((m, nl), F32)],
        scratch_shapes=[pltpu.VMEM((batch, nr), F32), pltpu.VMEM((batch, nl), F32),
                        pltpu.VMEM((nq // LANES, tm, LANES), F32)],
        compiler_params=_params("arbitrary"),
        name="inproj",
    )(x2, wqkv, wrkv, wlora, mu_rkv, mu_lora)


CLASS_COPY_ROWS = 256
ATT_WINDOW = 2 * ATT_BLOCK
ATT_HEADS_PER_TILE = LANES // ATT_HEAD_DIM


def _attn_kernel(q_ref, k_ref, v_ref, out_ref, acc_ref, m_ref, l_ref, qc_ref, kc_ref, vc_ref,
                 s0_scr, s1_scr, p0_scr, p1_scr, ml0_scr, ml1_scr, bias_scr):
    seq = q_ref.shape[0]
    n_blocks = seq // ATT_BLOCK
    lane = lax.broadcasted_iota(jnp.int32, (ATT_BLOCK, LANES), 1)
    in_head = [(lane >= h * ATT_HEAD_DIM) & (lane < (h + 1) * ATT_HEAD_DIM)
               for h in range(ATT_HEADS_PER_TILE)]
    rel = (lax.broadcasted_iota(jnp.int32, (ATT_BLOCK, ATT_WINDOW), 0)
           - lax.broadcasted_iota(jnp.int32, (ATT_BLOCK, ATT_WINDOW), 1))
    nt = (((1,), (1,)), ((), ()))

    acc_ref[...] = jnp.zeros_like(acc_ref)
    l_ref[...] = jnp.zeros_like(l_ref)
    m_ref[...] = jnp.full(m_ref.shape, NEG_BIG, F32)
    s_scr, p_scr, ml_scr = (s0_scr, s1_scr), (p0_scr, p1_scr), (ml0_scr, ml1_scr)
    for ref in s_scr + p_scr + ml_scr:
        ref[...] = jnp.zeros_like(ref)
    for first_window in range(2):
        dist = rel + first_window * ATT_BLOCK
        bias_scr[first_window] = jnp.where((dist >= 0) & (dist <= ATT_BLOCK), 0.0, NEG_BIG)

    for dil in DILATIONS:
        length = seq // dil
        per_class = length // ATT_BLOCK

        def extract(c, carry, dil=dil, length=length):
            for j in range(length // CLASS_COPY_ROWS):
                src = pl.ds(c + j * CLASS_COPY_ROWS * dil, CLASS_COPY_ROWS, stride=dil)
                dst = pl.ds(pl.multiple_of(c * length, CLASS_COPY_ROWS) + j * CLASS_COPY_ROWS,
                            CLASS_COPY_ROWS)
                qc_ref[dst, :] = q_ref[src, :].astype(BF16)
                kc_ref[dst, :] = k_ref[src, :].astype(BF16)
                vc_ref[dst, :] = v_ref[src, :].astype(BF16)
            return carry

        lax.fori_loop(0, dil, extract, 0)

        def locate(g, per_class=per_class):
            g = jnp.clip(g, 0, n_blocks - 1)
            first = (g % per_class) == 0
            q_start = pl.multiple_of(g * ATT_BLOCK, ATT_BLOCK)
            k_start = pl.multiple_of(jnp.where(first, g, g - 1) * ATT_BLOCK, ATT_BLOCK)
            return g, q_start, k_start

        def scores(g, slot, locate=locate):
            _, q_start, k_start = locate(g)
            q = qc_ref[pl.ds(q_start, ATT_BLOCK), :]
            kw = kc_ref[pl.ds(k_start, ATT_WINDOW), :]
            bias = bias_scr[(q_start - k_start) // ATT_BLOCK]
            for h in range(ATT_HEADS_PER_TILE):
                qh = jnp.where(in_head[h], q, jnp.zeros_like(q))
                s_scr[slot][h] = lax.dot_general(qh, kw, nt, preferred_element_type=F32) + bias

        def softmax(slot):
            m_acc = l_acc = jnp.zeros((ATT_BLOCK, LANES), F32)
            for h in range(ATT_HEADS_PER_TILE):
                sc = s_scr[slot][h]
                m = jnp.max(sc, axis=-1, keepdims=True)
                p = jnp.exp(sc - m)
                p_scr[slot][h] = p.astype(BF16)
                m_acc = jnp.where(in_head[h], m, m_acc)
                l_acc = jnp.where(in_head[h], jnp.sum(p, axis=-1, keepdims=True), l_acc)
            ml_scr[slot][0] = m_acc
            ml_scr[slot][1] = l_acc

        def accumulate(g, slot, dil=dil, per_class=per_class, locate=locate):
            valid = g >= 0
            g, _, k_start = locate(g)
            vw = vc_ref[pl.ds(k_start, ATT_WINDOW), :]
            o = jnp.zeros((ATT_BLOCK, LANES), F32)
            for h in range(ATT_HEADS_PER_TILE):
                o = jnp.where(in_head[h], jnp.dot(p_scr[slot][h], vw, preferred_element_type=F32), o)
            rows = pl.ds(g // per_class + (g % per_class) * (ATT_BLOCK * dil), ATT_BLOCK, stride=dil)
            m_blk = jnp.where(valid, ml_scr[slot][0], NEG_BIG)
            m_old = m_ref[rows, :]
            m_new = jnp.maximum(m_old, m_blk)
            w_old = jnp.exp(m_old - m_new)
            w_new = jnp.where(valid, jnp.exp(m_blk - m_new), 0.0)
            acc_ref[rows, :] = acc_ref[rows, :] * w_old + o * w_new
            l_ref[rows, :] = l_ref[rows, :] * w_old + ml_scr[slot][1] * w_new
            m_ref[rows, :] = m_new

        def pipeline_pair(it, carry, scores=scores, softmax=softmax, accumulate=accumulate):
            for parity in range(2):
                i = 2 * it + parity
                accumulate(i - 4, parity)
                softmax(parity)
                scores(i, parity)
            return carry

        lax.fori_loop(0, (n_blocks + 4) // 2, pipeline_pair, 0)

    out_ref[...] = acc_ref[...] / l_ref[...]


def _attention(qkv_b):
    b, s, c3 = qkv_b.shape
    a = c3 // 3
    npair = a // LANES
    assert s % (DILATIONS[-1] * ATT_WINDOW) == 0 and (s // DILATIONS[-1]) % CLASS_COPY_ROWS == 0
    blk = lambda off: pl.BlockSpec((None, s, LANES), lambda bi, hp: (bi, 0, off + hp))
    pipe = (ATT_HEADS_PER_TILE, ATT_BLOCK, ATT_WINDOW)
    return pl.pallas_call(
        _attn_kernel,
        grid=(b, npair),
        in_specs=[blk(0), blk(npair), blk(2 * npair)],
        out_specs=blk(0),
        out_shape=jax.ShapeDtypeStruct((b, s, a), F32),
        scratch_shapes=[pltpu.VMEM((s, LANES), F32)] * 3 + [pltpu.VMEM((s, LANES), BF16)] * 3
                       + [pltpu.VMEM(pipe, F32)] * 2 + [pltpu.VMEM(pipe, BF16)] * 2
                       + [pltpu.VMEM((2, ATT_BLOCK, LANES), F32)] * 2
                       + [pltpu.VMEM((2, ATT_BLOCK, ATT_WINDOW), F32)],
        compiler_params=_params("arbitrary", "arbitrary"),
        name="dilated_attn",
    )(qkv_b, qkv_b, qkv_b)


def _segment_sum(x, seg_ref):
    seg = seg_ref[...]
    hi = x.astype(BF16)
    rest = x - hi.astype(F32)
    mid = rest.astype(BF16)
    lo = (rest - mid.astype(F32)).astype(BF16)
    return (jnp.dot(hi, seg, preferred_element_type=F32) + jnp.dot(mid, seg, preferred_element_type=F32)
            + jnp.dot(lo, seg, preferred_element_type=F32))


KEY_HALVES = 2


def _chunk_transpose(parts, chunk):
    n = len(parts)
    lane = lax.broadcasted_iota(jnp.int32, parts[0].shape, 1)
    s = n // 2
    while s >= 1:
        swap = ((lane // chunk) & s) != 0
        new = list(parts)
        for j in range(n):
            if j & s == 0:
                lo, hi = parts[j], parts[j + s]
                new[j] = jnp.where(swap, pltpu.roll(hi, s * chunk, axis=1), lo)
                new[j + s] = jnp.where(swap, hi, pltpu.roll(lo, LANES - s * chunk, axis=1))
        parts = new
        s //= 2
    return parts


def _keys_to_lanes(p, heads, batch):
    q = p.T
    half = RWKV_HEAD_DIM // KEY_HALVES
    parts = [q[(h * KEY_HALVES + kh) * half:(h * KEY_HALVES + kh + 1) * half, :]
             for kh in range(KEY_HALVES) for h in range(heads)]
    return _chunk_transpose(parts, batch)


def _values_to_lanes(p, heads, batch):
    q = p.T
    parts = [q[h * RWKV_HEAD_DIM:(h + 1) * RWKV_HEAD_DIM, :]
             for _ in range(KEY_HALVES) for h in range(heads)]
    return _chunk_transpose(parts, batch)


def _values_from_lanes(parts, heads, batch):
    per_half = _chunk_transpose(parts, batch)
    per_head = [per_half[h] + per_half[heads + h] for h in range(heads)]
    return jnp.concatenate(per_head, axis=0).T


def _rwkv_prep_kernel(rkv_ref, lora_ref, vfirst_ref, decay_up_ref, aaa_up_ref, gate_up_ref,
                      vres_up_ref, vec_ref, seg_ref,
                      r_ref, w_ref, k_ref, a_ref, b_ref, vl_ref, v_ref, g_ref, bonus_ref,
                      *, has_vres, batch):
    wdt = rkv_ref.shape[1] // 3
    r = rkv_ref[:, 0:wdt]
    k = rkv_ref[:, wdt:2 * wdt]
    v = rkv_ref[:, 2 * wdt:3 * wdt]
    wa = lora_ref[:, 0:LANES]
    gd = lora_ref[:, LANES:2 * LANES]
    decay_base, aaa_base, vres_base = vec_ref[0:1, :], vec_ref[1:2, :], vec_ref[2:3, :]
    k_k, k_a, r_k = vec_ref[3:4, :], vec_ref[4:5, :], vec_ref[5:6, :]

    z = -(decay_base + jnp.dot(jnp.tanh(wa).astype(BF16), decay_up_ref[...],
                               preferred_element_type=F32))
    softplus = jnp.maximum(z, 0.0) + jnp.log(1.0 + jnp.exp(-jnp.abs(z)))
    w_log = -softplus - 0.5
    w = jnp.exp(-jnp.exp(w_log))
    a = jax.nn.sigmoid(aaa_base + jnp.dot(wa.astype(BF16), aaa_up_ref[...],
                                          preferred_element_type=F32))
    g_ref[...] = jnp.dot(jax.nn.sigmoid(gd).astype(BF16), gate_up_ref[...],
                         preferred_element_type=F32)
    if has_vres:
        vd = lora_ref[:, 2 * LANES:3 * LANES]
        mix = jax.nn.sigmoid(vres_base + jnp.dot(vd.astype(BF16), vres_up_ref[...],
                                                 preferred_element_type=F32))
        v = v + (vfirst_ref[...] - v) * mix
    kk = k * k_k
    norm = jnp.sqrt(_segment_sum(kk * kk, seg_ref))
    kk = kk / jnp.maximum(norm, 1e-12)
    k2 = k * (1.0 + (a - 1.0) * k_a)
    v_ref[...] = v
    bonus_ref[...] = _segment_sum(r * k2 * r_k, seg_ref) * v

    heads = wdt // RWKV_HEAD_DIM
    group = LANES
    steps = group // batch
    for gi in range(rkv_ref.shape[0] // group):
        rows = slice(gi * group, (gi + 1) * group)
        for ref, val in ((r_ref, r), (w_ref, w), (k_ref, k2), (a_ref, -kk), (b_ref, kk * a)):
            for t, part in enumerate(_keys_to_lanes(val[rows], heads, batch)):
                ref[gi * steps + t] = part
        for t, part in enumerate(_values_to_lanes(v[rows], heads, batch)):
            vl_ref[gi * steps + t] = part


def _rwkv_prep(rkv, lora, vfirst, decay_up, aaa_up, gate_up, vres_up, vecs, seg, has_vres, batch, tm=256):
    m = rkv.shape[0]
    wdt = rkv.shape[1] // 3
    heads = wdt // RWKV_HEAD_DIM
    assert KEY_HALVES * heads * batch == LANES, "recurrence layout needs 2*heads*batch == 128 lanes"
    tm = min(tm, m)
    assert tm % LANES == 0
    steps = tm // batch
    seq = m // batch
    half = RWKV_HEAD_DIM // KEY_HALVES
    rows = lambda n: pl.BlockSpec((tm, n), lambda i: (i, 0))
    full = lambda arr: pl.BlockSpec(arr.shape, lambda i: (0, 0))
    key_spec = pl.BlockSpec((steps, half, LANES), lambda i: (i, 0, 0))
    val_spec = pl.BlockSpec((steps, RWKV_HEAD_DIM, LANES), lambda i: (i, 0, 0))
    key_out = jax.ShapeDtypeStruct((seq, half, LANES), F32)
    val_out = jax.ShapeDtypeStruct((seq, RWKV_HEAD_DIM, LANES), F32)
    nat_out = jax.ShapeDtypeStruct((m, wdt), F32)
    return pl.pallas_call(
        functools.partial(_rwkv_prep_kernel, has_vres=has_vres, batch=batch),
        grid=(m // tm,),
        in_specs=[rows(3 * wdt), rows(LORA_PAD), rows(wdt), full(decay_up), full(aaa_up),
                  full(gate_up), full(vres_up), full(vecs), full(seg)],
        out_specs=[key_spec] * 5 + [val_spec] + [rows(wdt)] * 3,
        out_shape=[key_out] * 5 + [val_out] + [nat_out] * 3,
        compiler_params=_params("arbitrary"),
        name="rwkv_prep",
    )(rkv, lora, vfirst, decay_up, aaa_up, gate_up, vres_up, vecs, seg)


def _rwkv_scan_kernel(a_ref, a_next_ref, w_ref, b_ref, k_ref, r_ref, v_ref, y_ref, state_ref, sa_ref):
    @pl.when(pl.program_id(0) == 0)
    def _():
        state_ref[...] = jnp.zeros_like(state_ref)
        sa_ref[...] = jnp.zeros_like(sa_ref)

    steps, nk, _ = a_ref.shape
    nv = v_ref.shape[1]

    def step(t, sa_half, next_a_row):
        sa = sa_half + pltpu.roll(sa_half, LANES // KEY_HALVES, axis=1)
        row = lambda ref, kx: jnp.broadcast_to(ref[t, pl.ds(kx, 1), :], (nv, LANES))
        v_t = v_ref[t]
        acc_y = None
        acc_sa = None
        for kx in range(nk):
            new = state_ref[kx] * row(w_ref, kx) + v_t * row(k_ref, kx) + sa * row(b_ref, kx)
            state_ref[kx] = new
            term_y = new * row(r_ref, kx)
            term_sa = new * jnp.broadcast_to(next_a_row(kx), (nv, LANES))
            acc_y = term_y if acc_y is None else acc_y + term_y
            acc_sa = term_sa if acc_sa is None else acc_sa + term_sa
        y_ref[t] = acc_y
        return acc_sa

    def body(t, sa_half):
        return step(t, sa_half, lambda kx: a_ref[t + 1, pl.ds(kx, 1), :])

    sa_half = lax.fori_loop(0, steps - 1, body, sa_ref[...])
    sa_ref[...] = step(steps - 1, sa_half, lambda kx: a_next_ref[0, pl.ds(kx, 1), :])


def _rwkv_scan(a_t, w_t, b_t, k_t, r_t, v_t, steps=32):
    s, nk, _ = a_t.shape
    nv = v_t.shape[1]
    steps = min(steps, s)
    n_blocks = s // steps
    kin = pl.BlockSpec((steps, nk, LANES), lambda i: (i, 0, 0))
    knext = pl.BlockSpec((1, nk, LANES), lambda i: (jnp.minimum((i + 1) * steps, s - 1), 0, 0))
    vin = pl.BlockSpec((steps, nv, LANES), lambda i: (i, 0, 0))
    return pl.pallas_call(
        _rwkv_scan_kernel,
        grid=(n_blocks,),
        in_specs=[kin, knext] + [kin] * 4 + [vin],
        out_specs=vin,
        out_shape=jax.ShapeDtypeStruct((s, nv, LANES), F32),
        scratch_shapes=[pltpu.VMEM((nk, nv, LANES), F32), pltpu.VMEM((nv, LANES), F32)],
        compiler_params=_params("arbitrary"),
        name="rwkv_scan",
    )(a_t, a_t, w_t, b_t, k_t, r_t, v_t)


def _layer_norm(z, g, b):
    mu = jnp.mean(z, axis=-1, keepdims=True)
    zc = z - mu
    var = jnp.mean(zc * zc, axis=-1, keepdims=True)
    return zc * lax.rsqrt(var + LN_EPS) * g + b


def _outproj_kernel(att_ref, y_ref, bonus_ref, g_ref, x_ref, wa_ref, wr_ref, vec_ref, ln_ref,
                    seg_ref, out_ref, outb_ref, att_rows, *, alpha, batch):
    n_tiles = att_rows.shape[0]
    for bi in range(batch):
        for j in range(n_tiles):
            att_rows[j, pl.ds(bi, att_ref.shape[1], stride=batch), :] = att_ref[bi, :, j * LANES:(j + 1) * LANES]
    att = jnp.concatenate([att_rows[j] for j in range(n_tiles)], axis=1)

    heads = bonus_ref.shape[1] // RWKV_HEAD_DIM
    steps = LANES // batch
    y = jnp.concatenate(
        [_values_from_lanes([y_ref[gi * steps + t] for t in range(steps)], heads, batch)
         for gi in range(y_ref.shape[0] // steps)], axis=0)
    inv_n = 1.0 / RWKV_HEAD_DIM
    mu = _segment_sum(y, seg_ref) * inv_n
    yc = y - mu
    var = _segment_sum(yc * yc, seg_ref) * inv_n
    yn = yc * lax.rsqrt(var + GN_EPS) * vec_ref[0:1, :] + vec_ref[1:2, :]
    rw = (yn + bonus_ref[...]) * g_ref[...]

    mix = (jnp.dot(att.astype(BF16), wa_ref[...], preferred_element_type=F32)
           + jnp.dot(rw.astype(BF16), wr_ref[...], preferred_element_type=F32))
    out = _layer_norm(alpha * x_ref[...] + mix, ln_ref[0:1, :], ln_ref[1:2, :])
    out_ref[...] = out
    outb_ref[...] = out.astype(BF16)


def _outproj(att_b, y_lanes, bonus, g, x2, wa, wr, gn_vec, ln_vec, seg, alpha, tm=256):
    m, d = x2.shape
    batch, _, a = att_b.shape
    tm = min(tm, m)
    assert tm % LANES == 0
    rows = lambda n: pl.BlockSpec((tm, n), lambda i: (i, 0))
    full = lambda arr: pl.BlockSpec(arr.shape, lambda i: (0, 0))
    att_spec = pl.BlockSpec((batch, tm // batch, a), lambda i: (0, i, 0))
    y_spec = pl.BlockSpec((tm // batch,) + y_lanes.shape[1:], lambda i: (i, 0, 0))
    return pl.pallas_call(
        functools.partial(_outproj_kernel, alpha=alpha, batch=batch),
        grid=(m // tm,),
        in_specs=[att_spec, y_spec] + [rows(a)] * 2
                 + [rows(d), full(wa), full(wr), full(gn_vec), full(ln_vec), full(seg)],
        out_specs=[rows(d), rows(d)],
        out_shape=[jax.ShapeDtypeStruct((m, d), F32), jax.ShapeDtypeStruct((m, d), BF16)],
        scratch_shapes=[pltpu.VMEM((a // LANES, tm, LANES), F32)],
        compiler_params=_params("arbitrary"),
        name="outproj_ln",
    )(att_b, y_lanes, bonus, g, x2, wa, wr, gn_vec, ln_vec, seg)


def _swiglu_kernel(expert_ref, x_ref, wg_ref, wu_ref, wd_ref, out_ref, acc_ref):
    j = pl.program_id(1)

    @pl.when(j == 0)
    def _():
        acc_ref[...] = jnp.zeros_like(acc_ref)

    x = x_ref[...]
    gate = jnp.dot(x, wg_ref[0], preferred_element_type=F32)
    up = jnp.dot(x, wu_ref[0], preferred_element_type=F32)
    h = (gate * jax.nn.sigmoid(gate) * up).astype(BF16)
    acc_ref[...] += jnp.dot(h, wd_ref[0], preferred_element_type=F32)

    @pl.when(j == pl.num_programs(1) - 1)
    def _():
        out_ref[...] = acc_ref[...]


def _swiglu(tile_expert, xb, wg, wu, wd, tm, fc):
    m, d = xb.shape
    f = wg.shape[2]
    return pl.pallas_call(
        _swiglu_kernel,
        grid_spec=pltpu.PrefetchScalarGridSpec(
            num_scalar_prefetch=1,
            grid=(m // tm, f // fc),
            in_specs=[pl.BlockSpec((tm, d), lambda i, j, e: (i, 0)),
                      pl.BlockSpec((1, d, fc), lambda i, j, e: (e[i], 0, j)),
                      pl.BlockSpec((1, d, fc), lambda i, j, e: (e[i], 0, j)),
                      pl.BlockSpec((1, fc, d), lambda i, j, e: (e[i], j, 0))],
            out_specs=pl.BlockSpec((tm, d), lambda i, j, e: (i, 0)),
            scratch_shapes=[pltpu.VMEM((tm, d), F32)]),
        out_shape=jax.ShapeDtypeStruct((m, d), F32),
        compiler_params=_params("arbitrary", "arbitrary"),
        name="swiglu",
    )(tile_expert, xb, wg, wu, wd)


MAX_FFN_CHUNK = 2048


def _ffn_chunk(f):
    if f % LANES:
        raise ValueError(f"feed-forward width {f} is not a multiple of {LANES}")
    tiles = f // LANES
    return LANES * max(n for n in range(1, tiles + 1) if tiles % n == 0 and n * LANES <= MAX_FFN_CHUNK)


def _router_kernel(x_ref, wr_ref, out_ref):
    logits = jnp.dot(x_ref[...], wr_ref[...], preferred_element_type=F32,
                     precision=lax.Precision.HIGHEST)
    lane = lax.broadcasted_iota(jnp.int32, logits.shape, 1)
    logits = jnp.where(lane < N_EXPERTS, logits, NEG_BIG)
    v1 = jnp.max(logits, axis=-1, keepdims=True)
    i1 = jnp.min(jnp.where(logits == v1, lane, LANES), axis=-1, keepdims=True)
    rest = jnp.where(lane == i1, NEG_BIG, logits)
    v2 = jnp.max(rest, axis=-1, keepdims=True)
    i2 = jnp.min(jnp.where(rest == v2, lane, LANES), axis=-1, keepdims=True)
    e2 = jnp.exp(v2 - v1)
    g1 = 1.0 / (1.0 + e2)
    g2 = e2 / (1.0 + e2)
    out_ref[...] = jnp.where(lane == 0, i1.astype(F32),
                             jnp.where(lane == 1, i2.astype(F32),
                                       jnp.where(lane == 2, g1, jnp.where(lane == 3, g2, 0.0))))


def _router(x2, wr_pad, tm=512):
    m, d = x2.shape
    tm = min(tm, m)
    return pl.pallas_call(
        _router_kernel,
        grid=(m // tm,),
        in_specs=[pl.BlockSpec((tm, d), lambda i: (i, 0)), pl.BlockSpec(wr_pad.shape, lambda i: (0, 0))],
        out_specs=pl.BlockSpec((tm, LANES), lambda i: (i, 0)),
        out_shape=jax.ShapeDtypeStruct((m, LANES), F32),
        compiler_params=_params("arbitrary"),
        name="router_top2",
    )(x2, wr_pad)


def _add_ln_kernel(x_ref, f_ref, ln_ref, out_ref, *, alpha):
    out_ref[...] = _layer_norm(alpha * x_ref[...] + f_ref[...], ln_ref[0:1, :], ln_ref[1:2, :])


def _moe_add_ln_kernel(x_ref, y1_ref, y2_ref, route_ref, ln_ref, out_ref, *, alpha):
    g1 = route_ref[:, 2:3]
    g2 = route_ref[:, 3:4]
    f = g1 * y1_ref[...] + g2 * y2_ref[...]
    out_ref[...] = _layer_norm(alpha * x_ref[...] + f, ln_ref[0:1, :], ln_ref[1:2, :])


def _add_ln(x2, f, ln_vec, alpha, tm=512):
    m, d = x2.shape
    tm = min(tm, m)
    rows = pl.BlockSpec((tm, d), lambda i: (i, 0))
    return pl.pallas_call(
        functools.partial(_add_ln_kernel, alpha=alpha),
        grid=(m // tm,),
        in_specs=[rows, rows, pl.BlockSpec(ln_vec.shape, lambda i: (0, 0))],
        out_specs=rows,
        out_shape=jax.ShapeDtypeStruct((m, d), F32),
        compiler_params=_params("arbitrary"),
        name="add_ln",
    )(x2, f, ln_vec)


def _moe_add_ln(x2, y1, y2, route, ln_vec, alpha, tm=512):
    m, d = x2.shape
    tm = min(tm, m)
    rows = pl.BlockSpec((tm, d), lambda i: (i, 0))
    return pl.pallas_call(
        functools.partial(_moe_add_ln_kernel, alpha=alpha),
        grid=(m // tm,),
        in_specs=[rows, rows, rows, pl.BlockSpec((tm, LANES), lambda i: (i, 0)),
                  pl.BlockSpec(ln_vec.shape, lambda i: (0, 0))],
        out_specs=rows,
        out_shape=jax.ShapeDtypeStruct((m, d), F32),
        compiler_params=_params("arbitrary"),
        name="moe_add_ln",
    )(x2, y1, y2, route, ln_vec)


def _moe(x2, xb, router_w, wg, wu, wd, first_expert, n_exp, ln_vec, alpha, tm):
    m, d = x2.shape
    wr_pad = jnp.pad(router_w, ((0, 0), (0, LANES - n_exp)))
    route = _router(x2, wr_pad)
    idx = route[:, 0:2].astype(jnp.int32).reshape(-1)
    experts = jnp.arange(n_exp, dtype=jnp.int32)
    counts = jnp.sum((idx[:, None] == experts[None, :]).astype(jnp.int32), axis=0)
    padded = ((counts + tm - 1) // tm) * tm
    pad_end = jnp.cumsum(padded)
    pad_start = pad_end - padded
    start = jnp.cumsum(counts) - counts
    order = jnp.argsort(idx, stable=True)
    rank = jnp.argsort(order)
    slot_pos = pad_start[idx] + (rank - start[idx])
    rows_total = 2 * m + n_exp * tm
    tile_start = jnp.arange(rows_total // tm, dtype=jnp.int32) * tm
    tile_expert = jnp.minimum(jnp.sum((tile_start[:, None] >= pad_end[None, :]).astype(jnp.int32), axis=1),
                              n_exp - 1)
    row = jnp.arange(rows_total, dtype=jnp.int32)
    row_expert = jnp.repeat(tile_expert, tm)
    sorted_pos = jnp.clip(row - pad_start[row_expert] + start[row_expert], 0, 2 * m - 1)
    src_token = order[sorted_pos] // 2
    xg = jnp.take(xb, src_token, axis=0, mode="clip")
    yg = _swiglu(tile_expert + first_expert, xg, wg, wu, wd, tm, _ffn_chunk(wg.shape[2]))
    pos = slot_pos.reshape(m, 2)
    y1 = jnp.take(yg, pos[:, 0], axis=0, mode="clip")
    y2 = jnp.take(yg, pos[:, 1], axis=0, mode="clip")
    return _moe_add_ln(x2, y1, y2, route, ln_vec, alpha)


def kernel(x, w_in, w_in_vres, shift_mu, shift_mu_vres, decay_up, decay_base, aaa_up, aaa_base, vres_up, vres_base, gate_up, k_k, k_a, r_k, lnx_g, lnx_b, w_out, ln1_g, ln1_b, ln2_g, ln2_b, ffn_w_gate, ffn_w_up, ffn_w_down, router, moe_w_gate, moe_w_up, moe_w_down):
    b, s, d = x.shape
    depth = w_in.shape[0]
    wdt = decay_up.shape[2]
    att = w_out.shape[1] - wdt
    alpha = float((2 * depth) ** 0.25)
    m = b * s
    ffn_tm = min(512, m)

    seg = (jnp.arange(wdt)[:, None] // RWKV_HEAD_DIM == jnp.arange(wdt)[None, :] // RWKV_HEAD_DIM).astype(BF16)
    q_scale = jnp.concatenate([jnp.full((att,), ATT_HEAD_DIM ** -0.5, F32), jnp.ones((2 * att,), F32)])
    zero_tiles = jnp.zeros((m // ffn_tm,), jnp.int32)
    ffn_w = [w.astype(BF16) for w in (ffn_w_gate, ffn_w_up, ffn_w_down)]
    n_exp = moe_w_gate.shape[1]
    moe_w = [w.astype(BF16).reshape((-1,) + w.shape[2:]) for w in (moe_w_gate, moe_w_up, moe_w_down)]

    x2 = x.transpose(1, 0, 2).reshape(m, d)
    v_first = None
    for l in range(depth):
        has_vres = l > 0
        w_l = w_in[l]
        c0 = 3 * att
        wqkv = (w_l[:, :c0] * q_scale).astype(BF16)
        wrkv = w_l[:, c0:c0 + 3 * wdt].astype(BF16)
        lora_w = w_l[:, c0 + 3 * wdt:]
        lora_mu = shift_mu[l, 3 * wdt:]
        if has_vres:
            lora_w = jnp.concatenate([lora_w, w_in_vres[l - 1]], axis=1)
            lora_mu = jnp.concatenate([lora_mu, shift_mu_vres[l - 1]])
        n_lora = lora_w.shape[1]
        wlora = jnp.pad(lora_w, ((0, 0), (0, LORA_PAD - n_lora))).astype(BF16)
        mu_lora = jnp.pad(lora_mu, (0, LORA_PAD - n_lora)).reshape(1, LORA_PAD)
        mu_rkv = shift_mu[l, :3 * wdt].reshape(1, 3 * wdt)

        qkv, rkv, lora = _inproj(x2, wqkv, wrkv, wlora, mu_rkv, mu_lora, b)

        att_b = _attention(qkv)

        zpad = lambda w, rows_before: jnp.pad(
            w, ((rows_before, LANES - rows_before - w.shape[0]), (0, 0))).astype(BF16)
        decay_up_p = zpad(decay_up[l], 0)
        aaa_up_p = zpad(aaa_up[l], DECAY_LORA)
        gate_up_p = gate_up[l].astype(BF16)
        if has_vres:
            vres_up_p = zpad(vres_up[l - 1], 0)
            vres_b = vres_base[l - 1]
            vf_in = v_first
        else:
            vres_up_p = jnp.zeros((LANES, wdt), BF16)
            vres_b = jnp.zeros((wdt,), F32)
            vf_in = rkv[:, 2 * wdt:]
        vecs = jnp.stack([decay_base[l], aaa_base[l], vres_b, k_k[l], k_a[l], r_k[l],
                          jnp.zeros((wdt,), F32), jnp.zeros((wdt,), F32)])
        r_, w_, k_, a_, b_, vl_, v_, g_, bonus = _rwkv_prep(
            rkv, lora, vf_in, decay_up_p, aaa_up_p, gate_up_p, vres_up_p, vecs, seg, has_vres, b)
        if not has_vres:
            v_first = v_
        y_lanes = _rwkv_scan(a_, w_, b_, k_, r_, vl_)

        gn_vec = jnp.stack([lnx_g[l], lnx_b[l]] + [jnp.zeros((wdt,), F32)] * 6)
        ln1 = jnp.stack([ln1_g[l], ln1_b[l]] + [jnp.zeros((d,), F32)] * 6)
        ln2 = jnp.stack([ln2_g[l], ln2_b[l]] + [jnp.zeros((d,), F32)] * 6)
        wo = w_out[l].astype(BF16)
        x2, xb = _outproj(att_b, y_lanes, bonus, g_, x2, wo[:att], wo[att:], gn_vec, ln1, seg, alpha)

        i = l // 2
        if l % 2 == 0:
            f = _swiglu(zero_tiles + i, xb, *ffn_w, ffn_tm, _ffn_chunk(ffn_w[0].shape[2]))
            x2 = _add_ln(x2, f, ln2, alpha)
        else:
            x2 = _moe(x2, xb, router[i], *moe_w, i * n_exp, n_exp, ln2, alpha, ffn_tm)
    return x2.reshape(s, b, d).transpose(1, 0, 2)
```

```python
import functools

import jax
import jax.numpy as jnp
from jax import lax
from jax.experimental import pallas as pl
from jax.experimental.pallas import tpu as pltpu

ATT_HEAD_DIM = 64
RWKV_HEAD_DIM = 64
DILATIONS = (1, 4, 16)
ATT_BLOCK = 128
DECAY_LORA = 64
AAA_LORA = 64
GATE_LORA = 128
VRES_LORA = 32
N_EXPERTS = 8
LN_EPS = 1e-5
GN_EPS = 64e-5

LANES = 128
LORA_PAD = 384
VMEM_LIMIT = 56 * 1024 * 1024
NEG_BIG = -1e30

F32 = jnp.float32
BF16 = jnp.bfloat16


def _params(*sem):
    return pltpu.CompilerParams(dimension_semantics=sem, vmem_limit_bytes=VMEM_LIMIT)


def _inproj_kernel(x_ref, wqkv_ref, wrkv_ref, wlora_ref, mu_rkv_ref, mu_lora_ref,
                   qkv_ref, rkv_ref, lora_ref, carry_rkv, carry_lora, qkv_rows, *, batch):
    xb = x_ref[...].astype(BF16)
    qkv = jnp.dot(xb, wqkv_ref[...], preferred_element_type=F32)
    n_tiles, rows, _ = qkv_rows.shape
    for j in range(n_tiles):
        qkv_rows[j] = qkv[:, j * LANES:(j + 1) * LANES]
    for bi in range(batch):
        for j in range(n_tiles):
            qkv_ref[bi, :, j * LANES:(j + 1) * LANES] = qkv_rows[j, pl.ds(bi, rows // batch, stride=batch), :]

    first = pl.program_id(0) == 0
    nt = (((1,), (1,)), ((), ()))
    lane = lax.broadcasted_iota(jnp.int32, (1, LANES), 1)

    def shifted(wt_ref, mu_ref, carry_ref, out_ref):
        p = lax.dot_general(wt_ref[...], xb, nt, preferred_element_type=F32)
        tm = p.shape[1]
        mu = mu_ref[...]
        prev = pltpu.roll(p, batch, axis=1)
        from_carry = jnp.where(first, 0.0, pltpu.roll(carry_ref[...], batch, axis=1))
        for j in range(tm // LANES):
            cols = slice(j * LANES, (j + 1) * LANES)
            prev_j = prev[:, cols]
            if j == 0:
                prev_j = jnp.where(lane < batch, from_carry, prev_j)
            out_ref[:, cols] = p[:, cols] + (prev_j - p[:, cols]) * mu
        carry_ref[...] = p[:, tm - LANES:tm]

    shifted(wrkv_ref, mu_rkv_ref, carry_rkv, rkv_ref)
    shifted(wlora_ref, mu_lora_ref, carry_lora, lora_ref)


def _inproj(x2, wqkv, wrkv_t, wlora_t, mu_rkv, mu_lora, batch, tm=512):
    m, d = x2.shape
    nq, nr, nl = wqkv.shape[1], wrkv_t.shape[0], wlora_t.shape[0]
    tm = min(tm, m)
    full = lambda shape: pl.BlockSpec(shape, lambda i: (0, 0))
    return pl.pallas_call(
        functools.partial(_inproj_kernel, batch=batch),
        grid=(m // tm,),
        in_specs=[pl.BlockSpec((tm, d), lambda i: (i, 0)),
                  full((d, nq)), full((nr, d)), full((nl, d)), full((nr, LANES)), full((nl, LANES))],
        out_specs=[pl.BlockSpec((batch, tm // batch, nq), lambda i: (0, i, 0)),
                   pl.BlockSpec((nr, tm), lambda i: (0, i)),
                   pl.BlockSpec((nl, tm), lambda i: (0, i))],
        out_shape=[jax.ShapeDtypeStruct((batch, m // batch, nq), F32),
                   jax.ShapeDtypeStruct((nr, m), F32),
                   jax.ShapeDtypeStruct((nl, m), F32)],
        scratch_shapes=[pltpu.VMEM((nr, LANES), F32), pltpu.VMEM((nl, LANES), F32),
                        pltpu.VMEM((nq // LANES, tm, LANES), F32)],
        compiler_params=_params("arbitrary"),
        name="inproj",
    )(x2, wqkv, wrkv_t, wlora_t, mu_rkv, mu_lora)


CLASS_COPY_ROWS = 256
ATT_WINDOW = 2 * ATT_BLOCK
ATT_HEADS_PER_TILE = LANES // ATT_HEAD_DIM


def _attn_kernel(q_ref, k_ref, v_ref, out_ref, acc_ref, m_ref, l_ref, qc_ref, kc_ref, vc_ref,
                 s0_scr, s1_scr, p0_scr, p1_scr, ml0_scr, ml1_scr, bias_scr):
    seq = q_ref.shape[0]
    n_blocks = seq // ATT_BLOCK
    lane = lax.broadcasted_iota(jnp.int32, (ATT_BLOCK, LANES), 1)
    in_head = [(lane >= h * ATT_HEAD_DIM) & (lane < (h + 1) * ATT_HEAD_DIM)
               for h in range(ATT_HEADS_PER_TILE)]
    rel = (lax.broadcasted_iota(jnp.int32, (ATT_BLOCK, ATT_WINDOW), 0)
           - lax.broadcasted_iota(jnp.int32, (ATT_BLOCK, ATT_WINDOW), 1))
    nt = (((1,), (1,)), ((), ()))

    acc_ref[...] = jnp.zeros_like(acc_ref)
    l_ref[...] = jnp.zeros_like(l_ref)
    m_ref[...] = jnp.full(m_ref.shape, NEG_BIG, F32)
    s_scr, p_scr, ml_scr = (s0_scr, s1_scr), (p0_scr, p1_scr), (ml0_scr, ml1_scr)
    for ref in s_scr + p_scr + ml_scr:
        ref[...] = jnp.zeros_like(ref)
    for first_window in range(2):
        dist = rel + first_window * ATT_BLOCK
        bias_scr[first_window] = jnp.where((dist >= 0) & (dist <= ATT_BLOCK), 0.0, NEG_BIG)

    for dil in DILATIONS:
        length = seq // dil
        per_class = length // ATT_BLOCK

        def extract(c, carry, dil=dil, length=length):
            for j in range(length // CLASS_COPY_ROWS):
                src = pl.ds(c + j * CLASS_COPY_ROWS * dil, CLASS_COPY_ROWS, stride=dil)
                dst = pl.ds(pl.multiple_of(c * length, CLASS_COPY_ROWS) + j * CLASS_COPY_ROWS,
                            CLASS_COPY_ROWS)
                qc_ref[dst, :] = q_ref[src, :].astype(BF16)
                kc_ref[dst, :] = k_ref[src, :].astype(BF16)
                vc_ref[dst, :] = v_ref[src, :].astype(BF16)
            return carry

        lax.fori_loop(0, dil, extract, 0)

        def locate(g, per_class=per_class):
            g = jnp.clip(g, 0, n_blocks - 1)
            first = (g % per_class) == 0
            q_start = pl.multiple_of(g * ATT_BLOCK, ATT_BLOCK)
            k_start = pl.multiple_of(jnp.where(first, g, g - 1) * ATT_BLOCK, ATT_BLOCK)
            return g, q_start, k_start

        def scores(g, slot, locate=locate):
            _, q_start, k_start = locate(g)
            q = qc_ref[pl.ds(q_start, ATT_BLOCK), :]
            kw = kc_ref[pl.ds(k_start, ATT_WINDOW), :]
            bias = bias_scr[(q_start - k_start) // ATT_BLOCK]
            for h in range(ATT_HEADS_PER_TILE):
                qh = jnp.where(in_head[h], q, jnp.zeros_like(q))
                s_scr[slot][h] = lax.dot_general(qh, kw, nt, preferred_element_type=F32) + bias

        def softmax(slot):
            m_acc = l_acc = jnp.zeros((ATT_BLOCK, LANES), F32)
            for h in range(ATT_HEADS_PER_TILE):
                sc = s_scr[slot][h]
                m = jnp.max(sc, axis=-1, keepdims=True)
                p = jnp.exp(sc - m)
                p_scr[slot][h] = p.astype(BF16)
                m_acc = jnp.where(in_head[h], m, m_acc)
                l_acc = jnp.where(in_head[h], jnp.sum(p, axis=-1, keepdims=True), l_acc)
            ml_scr[slot][0] = m_acc
            ml_scr[slot][1] = l_acc

        def accumulate(g, slot, dil=dil, per_class=per_class, locate=locate):
            valid = g >= 0
            g, _, k_start = locate(g)
            vw = vc_ref[pl.ds(k_start, ATT_WINDOW), :]
            o = jnp.zeros((ATT_BLOCK, LANES), F32)
            for h in range(ATT_HEADS_PER_TILE):
                o = jnp.where(in_head[h], jnp.dot(p_scr[slot][h], vw, preferred_element_type=F32), o)
            rows = pl.ds(g // per_class + (g % per_class) * (ATT_BLOCK * dil), ATT_BLOCK, stride=dil)
            m_blk = jnp.where(valid, ml_scr[slot][0], NEG_BIG)
            m_old = m_ref[rows, :]
            m_new = jnp.maximum(m_old, m_blk)
            w_old = jnp.exp(m_old - m_new)
            w_new = jnp.where(valid, jnp.exp(m_blk - m_new), 0.0)
            acc_ref[rows, :] = acc_ref[rows, :] * w_old + o * w_new
            l_ref[rows, :] = l_ref[rows, :] * w_old + ml_scr[slot][1] * w_new
            m_ref[rows, :] = m_new

        def pipeline_pair(it, carry, scores=scores, softmax=softmax, accumulate=accumulate):
            for parity in range(2):
                i = 2 * it + parity
                accumulate(i - 4, parity)
                softmax(parity)
                scores(i, parity)
            return carry

        lax.fori_loop(0, (n_blocks + 4) // 2, pipeline_pair, 0)

    out_ref[...] = acc_ref[...] / l_ref[...]


def _attention(qkv_b):
    b, s, c3 = qkv_b.shape
    a = c3 // 3
    npair = a // LANES
    assert s % (DILATIONS[-1] * ATT_WINDOW) == 0 and (s // DILATIONS[-1]) % CLASS_COPY_ROWS == 0
    blk = lambda off: pl.BlockSpec((None, s, LANES), lambda bi, hp: (bi, 0, off + hp))
    pipe = (ATT_HEADS_PER_TILE, ATT_BLOCK, ATT_WINDOW)
    return pl.pallas_call(
        _attn_kernel,
        grid=(b, npair),
        in_specs=[blk(0), blk(npair), blk(2 * npair)],
        out_specs=blk(0),
        out_shape=jax.ShapeDtypeStruct((b, s, a), F32),
        scratch_shapes=[pltpu.VMEM((s, LANES), F32)] * 3 + [pltpu.VMEM((s, LANES), BF16)] * 3
                       + [pltpu.VMEM(pipe, F32)] * 2 + [pltpu.VMEM(pipe, BF16)] * 2
                       + [pltpu.VMEM((2, ATT_BLOCK, LANES), F32)] * 2
                       + [pltpu.VMEM((2, ATT_BLOCK, ATT_WINDOW), F32)],
        compiler_params=_params("arbitrary", "arbitrary"),
        name="dilated_attn",
    )(qkv_b, qkv_b, qkv_b)


KEY_HALVES = 2
SEG_ROWS = 256


def _chunk_transpose(parts, chunk):
    n = len(parts)
    lane = lax.broadcasted_iota(jnp.int32, parts[0].shape, 1)
    s = n // 2
    while s >= 1:
        swap = ((lane // chunk) & s) != 0
        new = list(parts)
        for j in range(n):
            if j & s == 0:
                lo, hi = parts[j], parts[j + s]
                new[j] = jnp.where(swap, pltpu.roll(hi, s * chunk, axis=1), lo)
                new[j + s] = jnp.where(swap, hi, pltpu.roll(lo, LANES - s * chunk, axis=1))
        parts = new
        s //= 2
    return parts


def _keys_to_lanes(q, heads, batch):
    half = RWKV_HEAD_DIM // KEY_HALVES
    parts = [q[(h * KEY_HALVES + kh) * half:(h * KEY_HALVES + kh + 1) * half, :]
             for kh in range(KEY_HALVES) for h in range(heads)]
    return _chunk_transpose(parts, batch)


def _values_to_lanes(q, heads, batch):
    parts = [q[h * RWKV_HEAD_DIM:(h + 1) * RWKV_HEAD_DIM, :]
             for _ in range(KEY_HALVES) for h in range(heads)]
    return _chunk_transpose(parts, batch)


def _values_from_lanes(parts, heads, batch):
    per_half = _chunk_transpose(parts, batch)
    return jnp.concatenate([per_half[h] + per_half[heads + h] for h in range(heads)], axis=0)


def _segment_sum_t(x, seg_ref):
    seg = seg_ref[...]
    n = seg.shape[0]
    hi = x.astype(BF16)
    rest = x - hi.astype(F32)
    mid = rest.astype(BF16)
    lo = (rest - mid.astype(F32)).astype(BF16)
    slabs = []
    for s0 in range(0, x.shape[0], n):
        rows = slice(s0, s0 + n)
        slabs.append(jnp.dot(seg, hi[rows], preferred_element_type=F32)
                     + jnp.dot(seg, mid[rows], preferred_element_type=F32)
                     + jnp.dot(seg, lo[rows], preferred_element_type=F32))
    return jnp.concatenate(slabs, axis=0)


def _rwkv_prep_kernel(rkv_ref, lora_ref, vfirst_ref, decay_up_ref, aaa_up_ref, gate_up_ref,
                      vres_up_ref, vec_ref, seg_ref,
                      r_ref, w_ref, k_ref, a_ref, b_ref, vl_ref, v_ref, g_ref, bonus_ref,
                      *, has_vres, batch):
    wdt = rkv_ref.shape[0] // 3
    heads = wdt // RWKV_HEAD_DIM
    steps = LANES // batch
    decay_base, aaa_base, vres_base = vec_ref[0], vec_ref[1], vec_ref[2]
    k_k, k_a, r_k = vec_ref[3], vec_ref[4], vec_ref[5]
    for gi in range(rkv_ref.shape[1] // LANES):
        cols = slice(gi * LANES, (gi + 1) * LANES)
        r = rkv_ref[0:wdt, cols]
        k = rkv_ref[wdt:2 * wdt, cols]
        v = rkv_ref[2 * wdt:3 * wdt, cols]
        wa = lora_ref[0:LANES, cols]
        gd = lora_ref[LANES:2 * LANES, cols]

        z = -(decay_base + jnp.dot(decay_up_ref[...], jnp.tanh(wa).astype(BF16),
                                   preferred_element_type=F32))
        softplus = jnp.maximum(z, 0.0) + jnp.log(1.0 + jnp.exp(-jnp.abs(z)))
        w = jnp.exp(-jnp.exp(-softplus - 0.5))
        a = jax.nn.sigmoid(aaa_base + jnp.dot(aaa_up_ref[...], wa.astype(BF16),
                                              preferred_element_type=F32))
        g_ref[:, cols] = jnp.dot(gate_up_ref[...], jax.nn.sigmoid(gd).astype(BF16),
                                 preferred_element_type=F32)
        if has_vres:
            vd = lora_ref[2 * LANES:3 * LANES, cols]
            mix = jax.nn.sigmoid(vres_base + jnp.dot(vres_up_ref[...], vd.astype(BF16),
                                                     preferred_element_type=F32))
            v = v + (vfirst_ref[:, cols] - v) * mix
        kk = k * k_k
        norm = jnp.sqrt(_segment_sum_t(kk * kk, seg_ref))
        kk = kk / jnp.maximum(norm, 1e-12)
        k2 = k * (1.0 + (a - 1.0) * k_a)
        v_ref[:, cols] = v
        bonus_ref[:, cols] = _segment_sum_t(r * k2 * r_k, seg_ref) * v

        for ref, val in ((r_ref, r), (w_ref, w), (k_ref, k2), (a_ref, -kk), (b_ref, kk * a)):
            for t, part in enumerate(_keys_to_lanes(val, heads, batch)):
                ref[gi * steps + t] = part
        for t, part in enumerate(_values_to_lanes(v, heads, batch)):
            vl_ref[gi * steps + t] = part


def _rwkv_prep(rkv_t, lora_t, vfirst_t, decay_up_t, aaa_up_t, gate_up_t, vres_up_t, vecs, seg,
               has_vres, batch, tm=256):
    m = rkv_t.shape[1]
    wdt = rkv_t.shape[0] // 3
    heads = wdt // RWKV_HEAD_DIM
    assert KEY_HALVES * heads * batch == LANES, "recurrence layout needs 2*heads*batch == 128 lanes"
    tm = min(tm, m)
    assert tm % LANES == 0
    steps = tm // batch
    seq = m // batch
    half = RWKV_HEAD_DIM // KEY_HALVES
    cols = lambda n: pl.BlockSpec((n, tm), lambda i: (0, i))
    full = lambda arr: pl.BlockSpec(arr.shape, lambda i: (0,) * arr.ndim)
    key_spec = pl.BlockSpec((steps, half, LANES), lambda i: (i, 0, 0))
    val_spec = pl.BlockSpec((steps, RWKV_HEAD_DIM, LANES), lambda i: (i, 0, 0))
    key_out = jax.ShapeDtypeStruct((seq, half, LANES), F32)
    val_out = jax.ShapeDtypeStruct((seq, RWKV_HEAD_DIM, LANES), F32)
    nat_out = jax.ShapeDtypeStruct((wdt, m), F32)
    return pl.pallas_call(
        functools.partial(_rwkv_prep_kernel, has_vres=has_vres, batch=batch),
        grid=(m // tm,),
        in_specs=[cols(3 * wdt), cols(LORA_PAD), cols(wdt), full(decay_up_t), full(aaa_up_t),
                  full(gate_up_t), full(vres_up_t), full(vecs), full(seg)],
        out_specs=[key_spec] * 5 + [val_spec] + [cols(wdt)] * 3,
        out_shape=[key_out] * 5 + [val_out] + [nat_out] * 3,
        compiler_params=_params("arbitrary"),
        name="rwkv_prep",
    )(rkv_t, lora_t, vfirst_t, decay_up_t, aaa_up_t, gate_up_t, vres_up_t, vecs, seg)


def _rwkv_scan_kernel(a_ref, a_next_ref, w_ref, b_ref, k_ref, r_ref, v_ref, y_ref, state_ref, sa_ref):
    @pl.when(pl.program_id(0) == 0)
    def _():
        state_ref[...] = jnp.zeros_like(state_ref)
        sa_ref[...] = jnp.zeros_like(sa_ref)

    steps, nk, _ = a_ref.shape
    nv = v_ref.shape[1]

    def step(t, sa_half, next_a_row):
        sa = sa_half + pltpu.roll(sa_half, LANES // KEY_HALVES, axis=1)
        row = lambda ref, kx: jnp.broadcast_to(ref[t, pl.ds(kx, 1), :], (nv, LANES))
        v_t = v_ref[t]
        acc_y = None
        acc_sa = None
        for kx in range(nk):
            new = state_ref[kx] * row(w_ref, kx) + v_t * row(k_ref, kx) + sa * row(b_ref, kx)
            state_ref[kx] = new
            term_y = new * row(r_ref, kx)
            term_sa = new * jnp.broadcast_to(next_a_row(kx), (nv, LANES))
            acc_y = term_y if acc_y is None else acc_y + term_y
            acc_sa = term_sa if acc_sa is None else acc_sa + term_sa
        y_ref[t] = acc_y
        return acc_sa

    def body(t, sa_half):
        return step(t, sa_half, lambda kx: a_ref[t + 1, pl.ds(kx, 1), :])

    sa_half = lax.fori_loop(0, steps - 1, body, sa_ref[...])
    sa_ref[...] = step(steps - 1, sa_half, lambda kx: a_next_ref[0, pl.ds(kx, 1), :])


def _rwkv_scan(a_t, w_t, b_t, k_t, r_t, v_t, steps=32):
    s, nk, _ = a_t.shape
    nv = v_t.shape[1]
    steps = min(steps, s)
    n_blocks = s // steps
    kin = pl.BlockSpec((steps, nk, LANES), lambda i: (i, 0, 0))
    knext = pl.BlockSpec((1, nk, LANES), lambda i: (jnp.minimum((i + 1) * steps, s - 1), 0, 0))
    vin = pl.BlockSpec((steps, nv, LANES), lambda i: (i, 0, 0))
    return pl.pallas_call(
        _rwkv_scan_kernel,
        grid=(n_blocks,),
        in_specs=[kin, knext] + [kin] * 4 + [vin],
        out_specs=vin,
        out_shape=jax.ShapeDtypeStruct((s, nv, LANES), F32),
        scratch_shapes=[pltpu.VMEM((nk, nv, LANES), F32), pltpu.VMEM((nv, LANES), F32)],
        compiler_params=_params("arbitrary"),
        name="rwkv_scan",
    )(a_t, a_t, w_t, b_t, k_t, r_t, v_t)


def _layer_norm(z, g, b):
    mu = jnp.mean(z, axis=-1, keepdims=True)
    zc = z - mu
    var = jnp.mean(zc * zc, axis=-1, keepdims=True)
    return zc * lax.rsqrt(var + LN_EPS) * g + b


def _outproj_kernel(att_ref, y_ref, bonus_ref, g_ref, x_ref, wa_ref, wr_ref, vec_ref, ln_ref,
                    seg_ref, out_ref, outb_ref, att_rows, *, alpha, batch):
    n_tiles = att_rows.shape[0]
    for bi in range(batch):
        for j in range(n_tiles):
            att_rows[j, pl.ds(bi, att_ref.shape[1], stride=batch), :] = att_ref[bi, :, j * LANES:(j + 1) * LANES]
    att = jnp.concatenate([att_rows[j] for j in range(n_tiles)], axis=1)

    heads = bonus_ref.shape[0] // RWKV_HEAD_DIM
    steps = LANES // batch
    inv_n = 1.0 / RWKV_HEAD_DIM
    n_groups = y_ref.shape[0] // steps
    wide = lambda col: jnp.concatenate([col] * n_groups, axis=1)
    y = jnp.concatenate(
        [_values_from_lanes([y_ref[gi * steps + t] for t in range(steps)], heads, batch)
         for gi in range(n_groups)], axis=1)
    mu = _segment_sum_t(y, seg_ref) * inv_n
    yc = y - mu
    var = _segment_sum_t(yc * yc, seg_ref) * inv_n
    yn = yc * lax.rsqrt(var + GN_EPS) * wide(vec_ref[0]) + wide(vec_ref[1])
    rw = ((yn + bonus_ref[...]) * g_ref[...]).T

    mix = (jnp.dot(att.astype(BF16), wa_ref[...], preferred_element_type=F32)
           + jnp.dot(rw.astype(BF16), wr_ref[...], preferred_element_type=F32))
    out = _layer_norm(alpha * x_ref[...] + mix, ln_ref[0:1, :], ln_ref[1:2, :])
    out_ref[...] = out
    outb_ref[...] = out.astype(BF16)


def _outproj(att_b, y_lanes, bonus, g, x2, wa, wr, gn_vec, ln_vec, seg, alpha, tm=256):
    m, d = x2.shape
    batch, _, a = att_b.shape
    tm = min(tm, m)
    assert tm % LANES == 0
    rows = lambda n: pl.BlockSpec((tm, n), lambda i: (i, 0))
    cols = lambda arr: pl.BlockSpec((arr.shape[0], tm), lambda i: (0, i))
    full = lambda arr: pl.BlockSpec(arr.shape, lambda i: (0,) * arr.ndim)
    att_spec = pl.BlockSpec((batch, tm // batch, a), lambda i: (0, i, 0))
    y_spec = pl.BlockSpec((tm // batch,) + y_lanes.shape[1:], lambda i: (i, 0, 0))
    return pl.pallas_call(
        functools.partial(_outproj_kernel, alpha=alpha, batch=batch),
        grid=(m // tm,),
        in_specs=[att_spec, y_spec, cols(bonus), cols(g)]
                 + [rows(d), full(wa), full(wr), full(gn_vec), full(ln_vec), full(seg)],
        out_specs=[rows(d), rows(d)],
        out_shape=[jax.ShapeDtypeStruct((m, d), F32), jax.ShapeDtypeStruct((m, d), BF16)],
        scratch_shapes=[pltpu.VMEM((a // LANES, tm, LANES), F32)],
        compiler_params=_params("arbitrary"),
        name="outproj_ln",
    )(att_b, y_lanes, bonus, g, x2, wa, wr, gn_vec, ln_vec, seg)


def _swiglu_kernel(expert_ref, x_ref, wg_ref, wu_ref, wd_ref, out_ref, acc_ref):
    j = pl.program_id(1)

    @pl.when(j == 0)
    def _():
        acc_ref[...] = jnp.zeros_like(acc_ref)

    x = x_ref[...]
    gate = jnp.dot(x, wg_ref[0], preferred_element_type=F32)
    up = jnp.dot(x, wu_ref[0], preferred_element_type=F32)
    h = (gate * jax.nn.sigmoid(gate) * up).astype(BF16)
    acc_ref[...] += jnp.dot(h, wd_ref[0], preferred_element_type=F32)

    @pl.when(j == pl.num_programs(1) - 1)
    def _():
        out_ref[...] = acc_ref[...]


def _swiglu(tile_expert, xb, wg, wu, wd, tm, fc):
    m, d = xb.shape
    f = wg.shape[2]
    return pl.pallas_call(
        _swiglu_kernel,
        grid_spec=pltpu.PrefetchScalarGridSpec(
            num_scalar_prefetch=1,
            grid=(m // tm, f // fc),
            in_specs=[pl.BlockSpec((tm, d), lambda i, j, e: (i, 0)),
                      pl.BlockSpec((1, d, fc), lambda i, j, e: (e[i], 0, j)),
                      pl.BlockSpec((1, d, fc), lambda i, j, e: (e[i], 0, j)),
                      pl.BlockSpec((1, fc, d), lambda i, j, e: (e[i], j, 0))],
            out_specs=pl.BlockSpec((tm, d), lambda i, j, e: (i, 0)),
            scratch_shapes=[pltpu.VMEM((tm, d), F32)]),
        out_shape=jax.ShapeDtypeStruct((m, d), F32),
        compiler_params=_params("arbitrary", "arbitrary"),
        name="swiglu",
    )(tile_expert, xb, wg, wu, wd)


MAX_FFN_CHUNK = 2048


def _ffn_chunk(f):
    if f % LANES:
        raise ValueError(f"feed-forward width {f} is not a multiple of {LANES}")
    tiles = f // LANES
    return LANES * max(n for n in range(1, tiles + 1) if tiles % n == 0 and n * LANES <= MAX_FFN_CHUNK)


def _router_kernel(x_ref, wr_ref, out_ref):
    logits = jnp.dot(x_ref[...], wr_ref[...], preferred_element_type=F32,
                     precision=lax.Precision.HIGHEST)
    lane = lax.broadcasted_iota(jnp.int32, logits.shape, 1)
    logits = jnp.where(lane < N_EXPERTS, logits, NEG_BIG)
    v1 = jnp.max(logits, axis=-1, keepdims=True)
    i1 = jnp.min(jnp.where(logits == v1, lane, LANES), axis=-1, keepdims=True)
    rest = jnp.where(lane == i1, NEG_BIG, logits)
    v2 = jnp.max(rest, axis=-1, keepdims=True)
    i2 = jnp.min(jnp.where(rest == v2, lane, LANES), axis=-1, keepdims=True)
    e2 = jnp.exp(v2 - v1)
    g1 = 1.0 / (1.0 + e2)
    g2 = e2 / (1.0 + e2)
    out_ref[...] = jnp.where(lane == 0, i1.astype(F32),
                             jnp.where(lane == 1, i2.astype(F32),
                                       jnp.where(lane == 2, g1, jnp.where(lane == 3, g2, 0.0))))


def _router(x2, wr_pad, tm=512):
    m, d = x2.shape
    tm = min(tm, m)
    return pl.pallas_call(
        _router_kernel,
        grid=(m // tm,),
        in_specs=[pl.BlockSpec((tm, d), lambda i: (i, 0)), pl.BlockSpec(wr_pad.shape, lambda i: (0, 0))],
        out_specs=pl.BlockSpec((tm, LANES), lambda i: (i, 0)),
        out_shape=jax.ShapeDtypeStruct((m, LANES), F32),
        compiler_params=_params("arbitrary"),
        name="router_top2",
    )(x2, wr_pad)


def _add_ln_kernel(x_ref, f_ref, ln_ref, out_ref, *, alpha):
    out_ref[...] = _layer_norm(alpha * x_ref[...] + f_ref[...], ln_ref[0:1, :], ln_ref[1:2, :])


def _moe_add_ln_kernel(x_ref, y1_ref, y2_ref, route_ref, ln_ref, out_ref, *, alpha):
    g1 = route_ref[:, 2:3]
    g2 = route_ref[:, 3:4]
    f = g1 * y1_ref[...] + g2 * y2_ref[...]
    out_ref[...] = _layer_norm(alpha * x_ref[...] + f, ln_ref[0:1, :], ln_ref[1:2, :])


def _add_ln(x2, f, ln_vec, alpha, tm=512):
    m, d = x2.shape
    tm = min(tm, m)
    rows = pl.BlockSpec((tm, d), lambda i: (i, 0))
    return pl.pallas_call(
        functools.partial(_add_ln_kernel, alpha=alpha),
        grid=(m // tm,),
        in_specs=[rows, rows, pl.BlockSpec(ln_vec.shape, lambda i: (0, 0))],
        out_specs=rows,
        out_shape=jax.ShapeDtypeStruct((m, d), F32),
        compiler_params=_params("arbitrary"),
        name="add_ln",
    )(x2, f, ln_vec)


def _moe_add_ln(x2, y1, y2, route, ln_vec, alpha, tm=512):
    m, d = x2.shape
    tm = min(tm, m)
    rows = pl.BlockSpec((tm, d), lambda i: (i, 0))
    return pl.pallas_call(
        functools.partial(_moe_add_ln_kernel, alpha=alpha),
        grid=(m // tm,),
        in_specs=[rows, rows, rows, pl.BlockSpec((tm, LANES), lambda i: (i, 0)),
                  pl.BlockSpec(ln_vec.shape, lambda i: (0, 0))],
        out_specs=rows,
        out_shape=jax.ShapeDtypeStruct((m, d), F32),
        compiler_params=_params("arbitrary"),
        name="moe_add_ln",
    )(x2, y1, y2, route, ln_vec)


def _moe(x2, xb, router_w, wg, wu, wd, first_expert, n_exp, ln_vec, alpha, tm):
    m, d = x2.shape
    wr_pad = jnp.pad(router_w, ((0, 0), (0, LANES - n_exp)))
    route = _router(x2, wr_pad)
    idx = route[:, 0:2].astype(jnp.int32).reshape(-1)
    experts = jnp.arange(n_exp, dtype=jnp.int32)
    counts = jnp.sum((idx[:, None] == experts[None, :]).astype(jnp.int32), axis=0)
    padded = ((counts + tm - 1) // tm) * tm
    pad_end = jnp.cumsum(padded)
    pad_start = pad_end - padded
    start = jnp.cumsum(counts) - counts
    order = jnp.argsort(idx, stable=True)
    rank = jnp.argsort(order)
    slot_pos = pad_start[idx] + (rank - start[idx])
    rows_total = 2 * m + n_exp * tm
    tile_start = jnp.arange(rows_total // tm, dtype=jnp.int32) * tm
    tile_expert = jnp.minimum(jnp.sum((tile_start[:, None] >= pad_end[None, :]).astype(jnp.int32), axis=1),
                              n_exp - 1)
    row = jnp.arange(rows_total, dtype=jnp.int32)
    row_expert = jnp.repeat(tile_expert, tm)
    sorted_pos = jnp.clip(row - pad_start[row_expert] + start[row_expert], 0, 2 * m - 1)
    src_token = order[sorted_pos] // 2
    xg = jnp.take(xb, src_token, axis=0, mode="clip")
    yg = _swiglu(tile_expert + first_expert, xg, wg, wu, wd, tm, _ffn_chunk(wg.shape[2]))
    pos = slot_pos.reshape(m, 2)
    y1 = jnp.take(yg, pos[:, 0], axis=0, mode="clip")
    y2 = jnp.take(yg, pos[:, 1], axis=0, mode="clip")
    return _moe_add_ln(x2, y1, y2, route, ln_vec, alpha)


def kernel(x, w_in, w_in_vres, shift_mu, shift_mu_vres, decay_up, decay_base, aaa_up, aaa_base, vres_up, vres_base, gate_up, k_k, k_a, r_k, lnx_g, lnx_b, w_out, ln1_g, ln1_b, ln2_g, ln2_b, ffn_w_gate, ffn_w_up, ffn_w_down, router, moe_w_gate, moe_w_up, moe_w_down):
    b, s, d = x.shape
    depth = w_in.shape[0]
    wdt = decay_up.shape[2]
    att = w_out.shape[1] - wdt
    alpha = float((2 * depth) ** 0.25)
    m = b * s
    ffn_tm = min(512, m)

    seg_head = jnp.arange(SEG_ROWS) // RWKV_HEAD_DIM
    seg = (seg_head[:, None] == seg_head[None, :]).astype(BF16)
    q_scale = jnp.concatenate([jnp.full((att,), ATT_HEAD_DIM ** -0.5, F32), jnp.ones((2 * att,), F32)])
    zero_tiles = jnp.zeros((m // ffn_tm,), jnp.int32)
    ffn_w = [w.astype(BF16) for w in (ffn_w_gate, ffn_w_up, ffn_w_down)]
    n_exp = moe_w_gate.shape[1]
    moe_w = [w.astype(BF16).reshape((-1,) + w.shape[2:]) for w in (moe_w_gate, moe_w_up, moe_w_down)]

    x2 = x.transpose(1, 0, 2).reshape(m, d)
    v_first = None
    for l in range(depth):
        has_vres = l > 0
        w_l = w_in[l]
        c0 = 3 * att
        wqkv = (w_l[:, :c0] * q_scale).astype(BF16)
        wrkv_t = w_l[:, c0:c0 + 3 * wdt].T.astype(BF16)
        lora_w = w_l[:, c0 + 3 * wdt:]
        lora_mu = shift_mu[l, 3 * wdt:]
        if has_vres:
            lora_w = jnp.concatenate([lora_w, w_in_vres[l - 1]], axis=1)
            lora_mu = jnp.concatenate([lora_mu, shift_mu_vres[l - 1]])
        n_lora = lora_w.shape[1]
        wlora_t = jnp.pad(lora_w, ((0, 0), (0, LORA_PAD - n_lora))).T.astype(BF16)
        columns = lambda vec: jnp.broadcast_to(vec[..., None], vec.shape + (LANES,))
        mu_lora = columns(jnp.pad(lora_mu, (0, LORA_PAD - n_lora)))
        mu_rkv = columns(shift_mu[l, :3 * wdt])

        qkv, rkv_t, lora_t = _inproj(x2, wqkv, wrkv_t, wlora_t, mu_rkv, mu_lora, b)

        att_b = _attention(qkv)

        zpad_t = lambda w, rows_before: jnp.pad(
            w, ((rows_before, LANES - rows_before - w.shape[0]), (0, 0))).T.astype(BF16)
        decay_up_p = zpad_t(decay_up[l], 0)
        aaa_up_p = zpad_t(aaa_up[l], DECAY_LORA)
        gate_up_p = gate_up[l].T.astype(BF16)
        if has_vres:
            vres_up_p = zpad_t(vres_up[l - 1], 0)
            vres_b = vres_base[l - 1]
            vf_in = v_first
        else:
            vres_up_p = jnp.zeros((wdt, LANES), BF16)
            vres_b = jnp.zeros((wdt,), F32)
            vf_in = rkv_t
        vecs = columns(jnp.stack([decay_base[l], aaa_base[l], vres_b, k_k[l], k_a[l], r_k[l]]))
        r_, w_, k_, a_, b_, vl_, v_, g_, bonus = _rwkv_prep(
            rkv_t, lora_t, vf_in, decay_up_p, aaa_up_p, gate_up_p, vres_up_p, vecs, seg, has_vres, b)
        if not has_vres:
            v_first = v_
        y_lanes = _rwkv_scan(a_, w_, b_, k_, r_, vl_)

        gn_vec = columns(jnp.stack([lnx_g[l], lnx_b[l]]))
        ln1 = jnp.stack([ln1_g[l], ln1_b[l]] + [jnp.zeros((d,), F32)] * 6)
        ln2 = jnp.stack([ln2_g[l], ln2_b[l]] + [jnp.zeros((d,), F32)] * 6)
        wo = w_out[l].astype(BF16)
        x2, xb = _outproj(att_b, y_lanes, bonus, g_, x2, wo[:att], wo[att:], gn_vec, ln1, seg, alpha)

        i = l // 2
        if l % 2 == 0:
            f = _swiglu(zero_tiles + i, xb, *ffn_w, ffn_tm, _ffn_chunk(ffn_w[0].shape[2]))
            x2 = _add_ln(x2, f, ln2, alpha)
        else:
            x2 = _moe(x2, xb, router[i], *moe_w, i * n_exp, n_exp, ln2, alpha, ffn_tm)
    return x2.reshape(s, b, d).transpose(1, 0, 2)
```

```python
import functools

import jax
import jax.numpy as jnp
from jax import lax
from jax.experimental import pallas as pl
from jax.experimental.pallas import tpu as pltpu

ATT_HEAD_DIM = 64
RWKV_HEAD_DIM = 64
DILATIONS = (1, 4, 16)
ATT_BLOCK = 128
DECAY_LORA = 64
AAA_LORA = 64
GATE_LORA = 128
VRES_LORA = 32
N_EXPERTS = 8
LN_EPS = 1e-5
GN_EPS = 64e-5

LANES = 128
LORA_PAD = 384
VMEM_LIMIT = 56 * 1024 * 1024
NEG_BIG = -1e30

F32 = jnp.float32
BF16 = jnp.bfloat16


def _params(*sem):
    return pltpu.CompilerParams(dimension_semantics=sem, vmem_limit_bytes=VMEM_LIMIT)


def _inproj_kernel(x_ref, wqkv_ref, wrkv_ref, wlora_ref, mu_rkv_ref, mu_lora_ref,
                   qkv_ref, rkv_ref, lora_ref, carry_rkv, carry_lora, qkv_rows, *, batch):
    xb = x_ref[...].astype(BF16)
    qkv = jnp.dot(xb, wqkv_ref[...], preferred_element_type=F32)
    n_tiles, rows, _ = qkv_rows.shape
    for j in range(n_tiles):
        qkv_rows[j] = qkv[:, j * LANES:(j + 1) * LANES]
    for bi in range(batch):
        for j in range(n_tiles):
            qkv_ref[bi, :, j * LANES:(j + 1) * LANES] = qkv_rows[j, pl.ds(bi, rows // batch, stride=batch), :]

    first = pl.program_id(0) == 0
    nt = (((1,), (1,)), ((), ()))
    lane = lax.broadcasted_iota(jnp.int32, (1, LANES), 1)

    def shifted(wt_ref, mu_ref, carry_ref, out_ref):
        p = lax.dot_general(wt_ref[...], xb, nt, preferred_element_type=F32)
        tm = p.shape[1]
        mu = mu_ref[...]
        prev = pltpu.roll(p, batch, axis=1)
        from_carry = jnp.where(first, 0.0, pltpu.roll(carry_ref[...], batch, axis=1))
        for j in range(tm // LANES):
            cols = slice(j * LANES, (j + 1) * LANES)
            prev_j = prev[:, cols]
            if j == 0:
                prev_j = jnp.where(lane < batch, from_carry, prev_j)
            out_ref[:, cols] = p[:, cols] + (prev_j - p[:, cols]) * mu
        carry_ref[...] = p[:, tm - LANES:tm]

    shifted(wrkv_ref, mu_rkv_ref, carry_rkv, rkv_ref)
    shifted(wlora_ref, mu_lora_ref, carry_lora, lora_ref)


def _inproj(x2, wqkv, wrkv_t, wlora_t, mu_rkv, mu_lora, batch, tm=512):
    m, d = x2.shape
    nq, nr, nl = wqkv.shape[1], wrkv_t.shape[0], wlora_t.shape[0]
    tm = min(tm, m)
    full = lambda shape: pl.BlockSpec(shape, lambda i: (0, 0))
    return pl.pallas_call(
        functools.partial(_inproj_kernel, batch=batch),
        grid=(m // tm,),
        in_specs=[pl.BlockSpec((tm, d), lambda i: (i, 0)),
                  full((d, nq)), full((nr, d)), full((nl, d)), full((nr, LANES)), full((nl, LANES))],
        out_specs=[pl.BlockSpec((batch, tm // batch, nq), lambda i: (0, i, 0)),
                   pl.BlockSpec((nr, tm), lambda i: (0, i)),
                   pl.BlockSpec((nl, tm), lambda i: (0, i))],
        out_shape=[jax.ShapeDtypeStruct((batch, m // batch, nq), F32),
                   jax.ShapeDtypeStruct((nr, m), F32),
                   jax.ShapeDtypeStruct((nl, m), F32)],
        scratch_shapes=[pltpu.VMEM((nr, LANES), F32), pltpu.VMEM((nl, LANES), F32),
                        pltpu.VMEM((nq // LANES, tm, LANES), F32)],
        compiler_params=_params("arbitrary"),
        name="inproj",
    )(x2, wqkv, wrkv_t, wlora_t, mu_rkv, mu_lora)


CLASS_COPY_ROWS = 256
ATT_WINDOW = 2 * ATT_BLOCK
ATT_HEADS_PER_TILE = LANES // ATT_HEAD_DIM


def _attn_kernel(q_ref, k_ref, v_ref, out_ref, acc_ref, m_ref, l_ref, qc_ref, kc_ref, vc_ref,
                 q4_ref, k4_ref, v4_ref, tmp_ref,
                 s0_scr, s1_scr, p0_scr, p1_scr, ml0_scr, ml1_scr, bias_scr):
    seq = q_ref.shape[0]
    n_blocks = seq // ATT_BLOCK
    base = DILATIONS[1]
    sub = DILATIONS[2] // base
    qlen = seq // base
    lane = lax.broadcasted_iota(jnp.int32, (ATT_BLOCK, LANES), 1)
    in_head = [(lane >= h * ATT_HEAD_DIM) & (lane < (h + 1) * ATT_HEAD_DIM)
               for h in range(ATT_HEADS_PER_TILE)]
    rel = (lax.broadcasted_iota(jnp.int32, (ATT_BLOCK, ATT_WINDOW), 0)
           - lax.broadcasted_iota(jnp.int32, (ATT_BLOCK, ATT_WINDOW), 1))
    nt = (((1,), (1,)), ((), ()))
    sources = ((q_ref, q4_ref, qc_ref), (k_ref, k4_ref, kc_ref), (v_ref, v4_ref, vc_ref))

    s_scr, p_scr, ml_scr = (s0_scr, s1_scr), (p0_scr, p1_scr), (ml0_scr, ml1_scr)
    for ref in s_scr + p_scr + ml_scr:
        ref[...] = jnp.zeros_like(ref)
    for first_window in range(2):
        dist = rel + first_window * ATT_BLOCK
        bias_scr[first_window] = jnp.where((dist >= 0) & (dist <= ATT_BLOCK), 0.0, NEG_BIG)

    def split_by_base(c, carry):
        for j in range(qlen // CLASS_COPY_ROWS):
            src = pl.ds(c + j * CLASS_COPY_ROWS * base, CLASS_COPY_ROWS, stride=base)
            dst = pl.ds(pl.multiple_of(c * qlen, CLASS_COPY_ROWS) + j * CLASS_COPY_ROWS, CLASS_COPY_ROWS)
            for full_ref, split_ref, _ in sources:
                split_ref[dst, :] = full_ref[src, :]
        return carry

    lax.fori_loop(0, base, split_by_base, 0)

    def run_pattern(per_class, commit):
        def locate(g):
            g = jnp.clip(g, 0, n_blocks - 1)
            first = (g % per_class) == 0
            q_start = pl.multiple_of(g * ATT_BLOCK, ATT_BLOCK)
            k_start = pl.multiple_of(jnp.where(first, g, g - 1) * ATT_BLOCK, ATT_BLOCK)
            return g, q_start, k_start

        def scores(g, slot):
            _, q_start, k_start = locate(g)
            q = qc_ref[pl.ds(q_start, ATT_BLOCK), :]
            kw = kc_ref[pl.ds(k_start, ATT_WINDOW), :]
            bias = bias_scr[(q_start - k_start) // ATT_BLOCK]
            for h in range(ATT_HEADS_PER_TILE):
                qh = jnp.where(in_head[h], q, jnp.zeros_like(q))
                s_scr[slot][h] = lax.dot_general(qh, kw, nt, preferred_element_type=F32) + bias

        def softmax(slot):
            m_acc = l_acc = jnp.zeros((ATT_BLOCK, LANES), F32)
            for h in range(ATT_HEADS_PER_TILE):
                sc = s_scr[slot][h]
                m = jnp.max(sc, axis=-1, keepdims=True)
                p = jnp.exp(sc - m)
                p_scr[slot][h] = p.astype(BF16)
                m_acc = jnp.where(in_head[h], m, m_acc)
                l_acc = jnp.where(in_head[h], jnp.sum(p, axis=-1, keepdims=True), l_acc)
            ml_scr[slot][0] = m_acc
            ml_scr[slot][1] = l_acc

        def accumulate(g, slot):
            valid = g >= 0
            g, _, k_start = locate(g)
            vw = vc_ref[pl.ds(k_start, ATT_WINDOW), :]
            o = jnp.zeros((ATT_BLOCK, LANES), F32)
            for h in range(ATT_HEADS_PER_TILE):
                o = jnp.where(in_head[h], jnp.dot(p_scr[slot][h], vw, preferred_element_type=F32), o)
            commit(g, valid, o, ml_scr[slot][0], ml_scr[slot][1])

        def pipeline_pair(it, carry):
            for parity in range(2):
                i = 2 * it + parity
                accumulate(i - 4, parity)
                softmax(parity)
                scores(i, parity)
            return carry

        lax.fori_loop(0, (n_blocks + 4) // 2, pipeline_pair, 0)

    def merge(rows, valid, o, m_blk, l_blk):
        m_blk = jnp.where(valid, m_blk, NEG_BIG)
        m_old = m_ref[rows, :]
        m_new = jnp.maximum(m_old, m_blk)
        w_old = jnp.exp(m_old - m_new)
        w_new = jnp.where(valid, jnp.exp(m_blk - m_new), 0.0)
        acc_ref[rows, :] = acc_ref[rows, :] * w_old + o * w_new
        l_ref[rows, :] = l_ref[rows, :] * w_old + l_blk * w_new
        m_ref[rows, :] = m_new

    for j in range(seq // CLASS_COPY_ROWS):
        rows = pl.ds(j * CLASS_COPY_ROWS, CLASS_COPY_ROWS)
        for full_ref, _, class_ref in sources:
            class_ref[rows, :] = full_ref[rows, :].astype(BF16)

    def commit_dense(g, valid, o, m_blk, l_blk):
        del valid
        per_base = ATT_BLOCK // base
        for idx, val in enumerate((o, m_blk, l_blk)):
            tmp_ref[idx] = val
        for c in range(base):
            dst = pl.ds(pl.multiple_of(c * qlen + g * per_base, per_base), per_base)
            for idx, state_ref in enumerate((acc_ref, m_ref, l_ref)):
                state_ref[dst, :] = tmp_ref[idx, pl.ds(c, per_base, stride=base), :]

    run_pattern(n_blocks, commit_dense)

    for j in range(seq // CLASS_COPY_ROWS):
        rows = pl.ds(j * CLASS_COPY_ROWS, CLASS_COPY_ROWS)
        for _, split_ref, class_ref in sources:
            class_ref[rows, :] = split_ref[rows, :].astype(BF16)

    def commit_base(g, valid, o, m_blk, l_blk):
        merge(pl.ds(pl.multiple_of(g * ATT_BLOCK, ATT_BLOCK), ATT_BLOCK), valid, o, m_blk, l_blk)

    run_pattern(qlen // ATT_BLOCK, commit_base)

    sub_len = qlen // sub
    per_sub = sub_len // ATT_BLOCK

    def split_again(cls, carry):
        src = pl.ds((cls // sub) * qlen + cls % sub, sub_len, stride=sub)
        dst = pl.ds(pl.multiple_of(cls * sub_len, CLASS_COPY_ROWS), sub_len)
        for _, split_ref, class_ref in sources:
            class_ref[dst, :] = split_ref[src, :].astype(BF16)
        return carry

    lax.fori_loop(0, base * sub, split_again, 0)

    def commit_sub(g, valid, o, m_blk, l_blk):
        cls = g // per_sub
        start = (cls // sub) * qlen + cls % sub + (g % per_sub) * (ATT_BLOCK * sub)
        merge(pl.ds(start, ATT_BLOCK, stride=sub), valid, o, m_blk, l_blk)

    run_pattern(per_sub, commit_sub)

    for c in range(base):
        for j in range(qlen // CLASS_COPY_ROWS):
            rows = pl.ds(c * qlen + j * CLASS_COPY_ROWS, CLASS_COPY_ROWS)
            out_ref[pl.ds(c + j * CLASS_COPY_ROWS * base, CLASS_COPY_ROWS, stride=base), :] = (
                acc_ref[rows, :] / l_ref[rows, :])


def _attention(qkv_b):
    b, s, c3 = qkv_b.shape
    a = c3 // 3
    npair = a // LANES
    assert s % (DILATIONS[-1] * ATT_WINDOW) == 0 and (s // DILATIONS[-1]) % CLASS_COPY_ROWS == 0
    blk = lambda off: pl.BlockSpec((None, s, LANES), lambda bi, hp: (bi, 0, off + hp))
    pipe = (ATT_HEADS_PER_TILE, ATT_BLOCK, ATT_WINDOW)
    return pl.pallas_call(
        _attn_kernel,
        grid=(b, npair),
        in_specs=[blk(0), blk(npair), blk(2 * npair)],
        out_specs=blk(0),
        out_shape=jax.ShapeDtypeStruct((b, s, a), F32),
        scratch_shapes=[pltpu.VMEM((s, LANES), F32)] * 3 + [pltpu.VMEM((s, LANES), BF16)] * 3
                       + [pltpu.VMEM((s, LANES), F32)] * 3 + [pltpu.VMEM((3, ATT_BLOCK, LANES), F32)]
                       + [pltpu.VMEM(pipe, F32)] * 2 + [pltpu.VMEM(pipe, BF16)] * 2
                       + [pltpu.VMEM((2, ATT_BLOCK, LANES), F32)] * 2
                       + [pltpu.VMEM((2, ATT_BLOCK, ATT_WINDOW), F32)],
        compiler_params=_params("arbitrary", "arbitrary"),
        name="dilated_attn",
    )(qkv_b, qkv_b, qkv_b)


KEY_HALVES = 2
SEG_ROWS = 256


def _chunk_transpose(parts, chunk):
    n = len(parts)
    lane = lax.broadcasted_iota(jnp.int32, parts[0].shape, 1)
    s = n // 2
    while s >= 1:
        swap = ((lane // chunk) & s) != 0
        new = list(parts)
        for j in range(n):
            if j & s == 0:
                lo, hi = parts[j], parts[j + s]
                new[j] = jnp.where(swap, pltpu.roll(hi, s * chunk, axis=1), lo)
                new[j + s] = jnp.where(swap, hi, pltpu.roll(lo, LANES - s * chunk, axis=1))
        parts = new
        s //= 2
    return parts


def _keys_to_lanes(q, heads, batch):
    half = RWKV_HEAD_DIM // KEY_HALVES
    parts = [q[(h * KEY_HALVES + kh) * half:(h * KEY_HALVES + kh + 1) * half, :]
             for kh in range(KEY_HALVES) for h in range(heads)]
    return _chunk_transpose(parts, batch)


def _values_to_lanes(q, heads, batch):
    parts = [q[h * RWKV_HEAD_DIM:(h + 1) * RWKV_HEAD_DIM, :]
             for _ in range(KEY_HALVES) for h in range(heads)]
    return _chunk_transpose(parts, batch)


def _values_from_lanes(parts, heads, batch):
    per_half = _chunk_transpose(parts, batch)
    return jnp.concatenate([per_half[h] + per_half[heads + h] for h in range(heads)], axis=0)


def _segment_sum_t(x, seg_ref):
    seg = seg_ref[...]
    n = seg.shape[0]
    hi = x.astype(BF16)
    rest = x - hi.astype(F32)
    mid = rest.astype(BF16)
    lo = (rest - mid.astype(F32)).astype(BF16)
    slabs = []
    for s0 in range(0, x.shape[0], n):
        rows = slice(s0, s0 + n)
        slabs.append(jnp.dot(seg, hi[rows], preferred_element_type=F32)
                     + jnp.dot(seg, mid[rows], preferred_element_type=F32)
                     + jnp.dot(seg, lo[rows], preferred_element_type=F32))
    return jnp.concatenate(slabs, axis=0)


def _rwkv_prep_kernel(rkv_ref, lora_ref, vfirst_ref, decay_up_ref, aaa_up_ref, gate_up_ref,
                      vres_up_ref, vec_ref, seg_ref,
                      r_ref, w_ref, k_ref, a_ref, b_ref, vl_ref, v_ref, g_ref, bonus_ref,
                      *, has_vres, batch):
    wdt = rkv_ref.shape[0] // 3
    heads = wdt // RWKV_HEAD_DIM
    steps = LANES // batch
    decay_base, aaa_base, vres_base = vec_ref[0], vec_ref[1], vec_ref[2]
    k_k, k_a, r_k = vec_ref[3], vec_ref[4], vec_ref[5]
    for gi in range(rkv_ref.shape[1] // LANES):
        cols = slice(gi * LANES, (gi + 1) * LANES)
        r = rkv_ref[0:wdt, cols]
        k = rkv_ref[wdt:2 * wdt, cols]
        v = rkv_ref[2 * wdt:3 * wdt, cols]
        wa = lora_ref[0:LANES, cols]
        gd = lora_ref[LANES:2 * LANES, cols]

        z = -(decay_base + jnp.dot(decay_up_ref[...], jnp.tanh(wa).astype(BF16),
                                   preferred_element_type=F32))
        softplus = jnp.maximum(z, 0.0) + jnp.log(1.0 + jnp.exp(-jnp.abs(z)))
        w = jnp.exp(-jnp.exp(-softplus - 0.5))
        a = jax.nn.sigmoid(aaa_base + jnp.dot(aaa_up_ref[...], wa.astype(BF16),
                                              preferred_element_type=F32))
        g_ref[:, cols] = jnp.dot(gate_up_ref[...], jax.nn.sigmoid(gd).astype(BF16),
                                 preferred_element_type=F32)
        if has_vres:
            vd = lora_ref[2 * LANES:3 * LANES, cols]
            mix = jax.nn.sigmoid(vres_base + jnp.dot(vres_up_ref[...], vd.astype(BF16),
                                                     preferred_element_type=F32))
            v = v + (vfirst_ref[:, cols] - v) * mix
        kk = k * k_k
        norm = jnp.sqrt(_segment_sum_t(kk * kk, seg_ref))
        kk = kk / jnp.maximum(norm, 1e-12)
        k2 = k * (1.0 + (a - 1.0) * k_a)
        v_ref[:, cols] = v
        bonus_ref[:, cols] = _segment_sum_t(r * k2 * r_k, seg_ref) * v

        for ref, val in ((r_ref, r), (w_ref, w), (k_ref, k2), (a_ref, -kk), (b_ref, kk * a)):
            for t, part in enumerate(_keys_to_lanes(val, heads, batch)):
                ref[gi * steps + t] = part
        for t, part in enumerate(_values_to_lanes(v, heads, batch)):
            vl_ref[gi * steps + t] = part


def _rwkv_prep(rkv_t, lora_t, vfirst_t, decay_up_t, aaa_up_t, gate_up_t, vres_up_t, vecs, seg,
               has_vres, batch, tm=256):
    m = rkv_t.shape[1]
    wdt = rkv_t.shape[0] // 3
    heads = wdt // RWKV_HEAD_DIM
    assert KEY_HALVES * heads * batch == LANES, "recurrence layout needs 2*heads*batch == 128 lanes"
    tm = min(tm, m)
    assert tm % LANES == 0
    steps = tm // batch
    seq = m // batch
    half = RWKV_HEAD_DIM // KEY_HALVES
    cols = lambda n: pl.BlockSpec((n, tm), lambda i: (0, i))
    full = lambda arr: pl.BlockSpec(arr.shape, lambda i: (0,) * arr.ndim)
    key_spec = pl.BlockSpec((steps, half, LANES), lambda i: (i, 0, 0))
    val_spec = pl.BlockSpec((steps, RWKV_HEAD_DIM, LANES), lambda i: (i, 0, 0))
    key_out = jax.ShapeDtypeStruct((seq, half, LANES), F32)
    val_out = jax.ShapeDtypeStruct((seq, RWKV_HEAD_DIM, LANES), F32)
    nat_out = jax.ShapeDtypeStruct((wdt, m), F32)
    return pl.pallas_call(
        functools.partial(_rwkv_prep_kernel, has_vres=has_vres, batch=batch),
        grid=(m // tm,),
        in_specs=[cols(3 * wdt), cols(LORA_PAD), cols(wdt), full(decay_up_t), full(aaa_up_t),
                  full(gate_up_t), full(vres_up_t), full(vecs), full(seg)],
        out_specs=[key_spec] * 5 + [val_spec] + [cols(wdt)] * 3,
        out_shape=[key_out] * 5 + [val_out] + [nat_out] * 3,
        compiler_params=_params("arbitrary"),
        name="rwkv_prep",
    )(rkv_t, lora_t, vfirst_t, decay_up_t, aaa_up_t, gate_up_t, vres_up_t, vecs, seg)


def _rwkv_scan_kernel(a_ref, a_next_ref, w_ref, b_ref, k_ref, r_ref, v_ref, y_ref, state_ref, sa_ref):
    @pl.when(pl.program_id(0) == 0)
    def _():
        state_ref[...] = jnp.zeros_like(state_ref)
        sa_ref[...] = jnp.zeros_like(sa_ref)

    steps, nk, _ = a_ref.shape
    nv = v_ref.shape[1]

    def step(t, sa_half, next_a_row):
        sa = sa_half + pltpu.roll(sa_half, LANES // KEY_HALVES, axis=1)
        row = lambda ref, kx: jnp.broadcast_to(ref[t, pl.ds(kx, 1), :], (nv, LANES))
        v_t = v_ref[t]
        acc_y = None
        acc_sa = None
        for kx in range(nk):
            new = state_ref[kx] * row(w_ref, kx) + v_t * row(k_ref, kx) + sa * row(b_ref, kx)
            state_ref[kx] = new
            term_y = new * row(r_ref, kx)
            term_sa = new * jnp.broadcast_to(next_a_row(kx), (nv, LANES))
            acc_y = term_y if acc_y is None else acc_y + term_y
            acc_sa = term_sa if acc_sa is None else acc_sa + term_sa
        y_ref[t] = acc_y
        return acc_sa

    def body(t, sa_half):
        return step(t, sa_half, lambda kx: a_ref[t + 1, pl.ds(kx, 1), :])

    sa_half = lax.fori_loop(0, steps - 1, body, sa_ref[...], unroll=2)
    sa_ref[...] = step(steps - 1, sa_half, lambda kx: a_next_ref[0, pl.ds(kx, 1), :])


def _rwkv_scan(a_t, w_t, b_t, k_t, r_t, v_t, steps=32):
    s, nk, _ = a_t.shape
    nv = v_t.shape[1]
    steps = min(steps, s)
    n_blocks = s // steps
    kin = pl.BlockSpec((steps, nk, LANES), lambda i: (i, 0, 0))
    knext = pl.BlockSpec((1, nk, LANES), lambda i: (jnp.minimum((i + 1) * steps, s - 1), 0, 0))
    vin = pl.BlockSpec((steps, nv, LANES), lambda i: (i, 0, 0))
    return pl.pallas_call(
        _rwkv_scan_kernel,
        grid=(n_blocks,),
        in_specs=[kin, knext] + [kin] * 4 + [vin],
        out_specs=vin,
        out_shape=jax.ShapeDtypeStruct((s, nv, LANES), F32),
        scratch_shapes=[pltpu.VMEM((nk, nv, LANES), F32), pltpu.VMEM((nv, LANES), F32)],
        compiler_params=_params("arbitrary"),
        name="rwkv_scan",
    )(a_t, a_t, w_t, b_t, k_t, r_t, v_t)


def _layer_norm(z, g, b):
    mu = jnp.mean(z, axis=-1, keepdims=True)
    zc = z - mu
    var = jnp.mean(zc * zc, axis=-1, keepdims=True)
    return zc * lax.rsqrt(var + LN_EPS) * g + b


def _outproj_kernel(att_ref, y_ref, bonus_ref, g_ref, x_ref, wa_ref, wr_ref, vec_ref, ln_ref,
                    seg_ref, out_ref, outb_ref, att_rows, *, alpha, batch):
    n_tiles = att_rows.shape[0]
    for bi in range(batch):
        for j in range(n_tiles):
            att_rows[j, pl.ds(bi, att_ref.shape[1], stride=batch), :] = att_ref[bi, :, j * LANES:(j + 1) * LANES]
    att = jnp.concatenate([att_rows[j] for j in range(n_tiles)], axis=1)

    heads = bonus_ref.shape[0] // RWKV_HEAD_DIM
    steps = LANES // batch
    inv_n = 1.0 / RWKV_HEAD_DIM
    n_groups = y_ref.shape[0] // steps
    wide = lambda col: jnp.concatenate([col] * n_groups, axis=1)
    y = jnp.concatenate(
        [_values_from_lanes([y_ref[gi * steps + t] for t in range(steps)], heads, batch)
         for gi in range(n_groups)], axis=1)
    mu = _segment_sum_t(y, seg_ref) * inv_n
    yc = y - mu
    var = _segment_sum_t(yc * yc, seg_ref) * inv_n
    yn = yc * lax.rsqrt(var + GN_EPS) * wide(vec_ref[0]) + wide(vec_ref[1])
    rw = ((yn + bonus_ref[...]) * g_ref[...]).T

    mix = (jnp.dot(att.astype(BF16), wa_ref[...], preferred_element_type=F32)
           + jnp.dot(rw.astype(BF16), wr_ref[...], preferred_element_type=F32))
    out = _layer_norm(alpha * x_ref[...] + mix, ln_ref[0:1, :], ln_ref[1:2, :])
    out_ref[...] = out
    outb_ref[...] = out.astype(BF16)


def _outproj(att_b, y_lanes, bonus, g, x2, wa, wr, gn_vec, ln_vec, seg, alpha, tm=256):
    m, d = x2.shape
    batch, _, a = att_b.shape
    tm = min(tm, m)
    assert tm % LANES == 0
    rows = lambda n: pl.BlockSpec((tm, n), lambda i: (i, 0))
    cols = lambda arr: pl.BlockSpec((arr.shape[0], tm), lambda i: (0, i))
    full = lambda arr: pl.BlockSpec(arr.shape, lambda i: (0,) * arr.ndim)
    att_spec = pl.BlockSpec((batch, tm // batch, a), lambda i: (0, i, 0))
    y_spec = pl.BlockSpec((tm // batch,) + y_lanes.shape[1:], lambda i: (i, 0, 0))
    return pl.pallas_call(
        functools.partial(_outproj_kernel, alpha=alpha, batch=batch),
        grid=(m // tm,),
        in_specs=[att_spec, y_spec, cols(bonus), cols(g)]
                 + [rows(d), full(wa), full(wr), full(gn_vec), full(ln_vec), full(seg)],
        out_specs=[rows(d), rows(d)],
        out_shape=[jax.ShapeDtypeStruct((m, d), F32), jax.ShapeDtypeStruct((m, d), BF16)],
        scratch_shapes=[pltpu.VMEM((a // LANES, tm, LANES), F32)],
        compiler_params=_params("arbitrary"),
        name="outproj_ln",
    )(att_b, y_lanes, bonus, g, x2, wa, wr, gn_vec, ln_vec, seg)


def _swiglu_kernel(expert_ref, x_ref, wg_ref, wu_ref, wd_ref, *rest, alpha):
    if alpha is None:
        out_ref, acc_ref = rest
    else:
        res_ref, ln_ref, out_ref, acc_ref = rest
    j = pl.program_id(1)

    @pl.when(j == 0)
    def _():
        acc_ref[...] = jnp.zeros_like(acc_ref)

    x = x_ref[...]
    gate = jnp.dot(x, wg_ref[0], preferred_element_type=F32)
    up = jnp.dot(x, wu_ref[0], preferred_element_type=F32)
    h = (gate * jax.nn.sigmoid(gate) * up).astype(BF16)
    acc_ref[...] += jnp.dot(h, wd_ref[0], preferred_element_type=F32)

    @pl.when(j == pl.num_programs(1) - 1)
    def _():
        if alpha is None:
            out_ref[...] = acc_ref[...]
        else:
            out_ref[...] = _layer_norm(alpha * res_ref[...] + acc_ref[...], ln_ref[0:1, :], ln_ref[1:2, :])


def _swiglu(tile_expert, xb, wg, wu, wd, tm, fc, residual=None, ln_vec=None, alpha=None):
    m, d = xb.shape
    f = wg.shape[2]
    rows = pl.BlockSpec((tm, d), lambda i, j, e: (i, 0))
    extra_specs, extra_args = [], []
    if residual is not None:
        extra_specs = [rows, pl.BlockSpec(ln_vec.shape, lambda i, j, e: (0, 0))]
        extra_args = [residual, ln_vec]
    return pl.pallas_call(
        functools.partial(_swiglu_kernel, alpha=alpha),
        grid_spec=pltpu.PrefetchScalarGridSpec(
            num_scalar_prefetch=1,
            grid=(m // tm, f // fc),
            in_specs=[rows,
                      pl.BlockSpec((1, d, fc), lambda i, j, e: (e[i], 0, j)),
                      pl.BlockSpec((1, d, fc), lambda i, j, e: (e[i], 0, j)),
                      pl.BlockSpec((1, fc, d), lambda i, j, e: (e[i], j, 0))] + extra_specs,
            out_specs=rows,
            scratch_shapes=[pltpu.VMEM((tm, d), F32)]),
        out_shape=jax.ShapeDtypeStruct((m, d), F32),
        compiler_params=_params("arbitrary", "arbitrary"),
        name="swiglu",
    )(tile_expert, xb, wg, wu, wd, *extra_args)


MAX_FFN_CHUNK = 2048


def _ffn_chunk(f):
    if f % LANES:
        raise ValueError(f"feed-forward width {f} is not a multiple of {LANES}")
    tiles = f // LANES
    return LANES * max(n for n in range(1, tiles + 1) if tiles % n == 0 and n * LANES <= MAX_FFN_CHUNK)


def _router_kernel(x_ref, wr_ref, out_ref):
    logits = jnp.dot(x_ref[...], wr_ref[...], preferred_element_type=F32,
                     precision=lax.Precision.HIGHEST)
    lane = lax.broadcasted_iota(jnp.int32, logits.shape, 1)
    logits = jnp.where(lane < N_EXPERTS, logits, NEG_BIG)
    v1 = jnp.max(logits, axis=-1, keepdims=True)
    i1 = jnp.min(jnp.where(logits == v1, lane, LANES), axis=-1, keepdims=True)
    rest = jnp.where(lane == i1, NEG_BIG, logits)
    v2 = jnp.max(rest, axis=-1, keepdims=True)
    i2 = jnp.min(jnp.where(rest == v2, lane, LANES), axis=-1, keepdims=True)
    e2 = jnp.exp(v2 - v1)
    g1 = 1.0 / (1.0 + e2)
    g2 = e2 / (1.0 + e2)
    out_ref[...] = jnp.where(lane == 0, i1.astype(F32),
                             jnp.where(lane == 1, i2.astype(F32),
                                       jnp.where(lane == 2, g1, jnp.where(lane == 3, g2, 0.0))))


def _router(x2, wr_pad, tm=512):
    m, d = x2.shape
    tm = min(tm, m)
    return pl.pallas_call(
        _router_kernel,
        grid=(m // tm,),
        in_specs=[pl.BlockSpec((tm, d), lambda i: (i, 0)), pl.BlockSpec(wr_pad.shape, lambda i: (0, 0))],
        out_specs=pl.BlockSpec((tm, LANES), lambda i: (i, 0)),
        out_shape=jax.ShapeDtypeStruct((m, LANES), F32),
        compiler_params=_params("arbitrary"),
        name="router_top2",
    )(x2, wr_pad)


def _moe_add_ln_kernel(x_ref, y1_ref, y2_ref, route_ref, ln_ref, out_ref, *, alpha):
    g1 = route_ref[:, 2:3]
    g2 = route_ref[:, 3:4]
    f = g1 * y1_ref[...] + g2 * y2_ref[...]
    out_ref[...] = _layer_norm(alpha * x_ref[...] + f, ln_ref[0:1, :], ln_ref[1:2, :])


def _moe_add_ln(x2, y1, y2, route, ln_vec, alpha, tm=512):
    m, d = x2.shape
    tm = min(tm, m)
    rows = pl.BlockSpec((tm, d), lambda i: (i, 0))
    return pl.pallas_call(
        functools.partial(_moe_add_ln_kernel, alpha=alpha),
        grid=(m // tm,),
        in_specs=[rows, rows, rows, pl.BlockSpec((tm, LANES), lambda i: (i, 0)),
                  pl.BlockSpec(ln_vec.shape, lambda i: (0, 0))],
        out_specs=rows,
        out_shape=jax.ShapeDtypeStruct((m, d), F32),
        compiler_params=_params("arbitrary"),
        name="moe_add_ln",
    )(x2, y1, y2, route, ln_vec)


def _moe(x2, xb, router_w, wg, wu, wd, first_expert, n_exp, ln_vec, alpha, tm):
    m, d = x2.shape
    wr_pad = jnp.pad(router_w, ((0, 0), (0, LANES - n_exp)))
    route = _router(x2, wr_pad)
    idx = route[:, 0:2].astype(jnp.int32).reshape(-1)
    experts = jnp.arange(n_exp, dtype=jnp.int32)
    counts = jnp.sum((idx[:, None] == experts[None, :]).astype(jnp.int32), axis=0)
    padded = ((counts + tm - 1) // tm) * tm
    pad_end = jnp.cumsum(padded)
    pad_start = pad_end - padded
    start = jnp.cumsum(counts) - counts
    order = jnp.argsort(idx, stable=True)
    rank = jnp.argsort(order)
    slot_pos = pad_start[idx] + (rank - start[idx])
    rows_total = 2 * m + n_exp * tm
    tile_start = jnp.arange(rows_total // tm, dtype=jnp.int32) * tm
    tile_expert = jnp.minimum(jnp.sum((tile_start[:, None] >= pad_end[None, :]).astype(jnp.int32), axis=1),
                              n_exp - 1)
    row = jnp.arange(rows_total, dtype=jnp.int32)
    row_expert = jnp.repeat(tile_expert, tm)
    sorted_pos = jnp.clip(row - pad_start[row_expert] + start[row_expert], 0, 2 * m - 1)
    src_token = order[sorted_pos] // 2
    xg = jnp.take(xb, src_token, axis=0, mode="clip")
    yg = _swiglu(tile_expert + first_expert, xg, wg, wu, wd, tm, _ffn_chunk(wg.shape[2]))
    pos = slot_pos.reshape(m, 2)
    y1 = jnp.take(yg, pos[:, 0], axis=0, mode="clip")
    y2 = jnp.take(yg, pos[:, 1], axis=0, mode="clip")
    return _moe_add_ln(x2, y1, y2, route, ln_vec, alpha)


def kernel(x, w_in, w_in_vres, shift_mu, shift_mu_vres, decay_up, decay_base, aaa_up, aaa_base, vres_up, vres_base, gate_up, k_k, k_a, r_k, lnx_g, lnx_b, w_out, ln1_g, ln1_b, ln2_g, ln2_b, ffn_w_gate, ffn_w_up, ffn_w_down, router, moe_w_gate, moe_w_up, moe_w_down):
    b, s, d = x.shape
    depth = w_in.shape[0]
    wdt = decay_up.shape[2]
    att = w_out.shape[1] - wdt
    alpha = float((2 * depth) ** 0.25)
    m = b * s
    ffn_tm = min(512, m)

    seg_head = jnp.arange(SEG_ROWS) // RWKV_HEAD_DIM
    seg = (seg_head[:, None] == seg_head[None, :]).astype(BF16)
    q_scale = jnp.concatenate([jnp.full((att,), ATT_HEAD_DIM ** -0.5, F32), jnp.ones((2 * att,), F32)])
    zero_tiles = jnp.zeros((m // ffn_tm,), jnp.int32)
    ffn_w = [w.astype(BF16) for w in (ffn_w_gate, ffn_w_up, ffn_w_down)]
    n_exp = moe_w_gate.shape[1]
    moe_w = [w.astype(BF16).reshape((-1,) + w.shape[2:]) for w in (moe_w_gate, moe_w_up, moe_w_down)]

    x2 = x.transpose(1, 0, 2).reshape(m, d)
    v_first = None
    for l in range(depth):
        has_vres = l > 0
        w_l = w_in[l]
        c0 = 3 * att
        wqkv = (w_l[:, :c0] * q_scale).astype(BF16)
        wrkv_t = w_l[:, c0:c0 + 3 * wdt].T.astype(BF16)
        lora_w = w_l[:, c0 + 3 * wdt:]
        lora_mu = shift_mu[l, 3 * wdt:]
        if has_vres:
            lora_w = jnp.concatenate([lora_w, w_in_vres[l - 1]], axis=1)
            lora_mu = jnp.concatenate([lora_mu, shift_mu_vres[l - 1]])
        n_lora = lora_w.shape[1]
        wlora_t = jnp.pad(lora_w, ((0, 0), (0, LORA_PAD - n_lora))).T.astype(BF16)
        columns = lambda vec: jnp.broadcast_to(vec[..., None], vec.shape + (LANES,))
        mu_lora = columns(jnp.pad(lora_mu, (0, LORA_PAD - n_lora)))
        mu_rkv = columns(shift_mu[l, :3 * wdt])

        qkv, rkv_t, lora_t = _inproj(x2, wqkv, wrkv_t, wlora_t, mu_rkv, mu_lora, b)

        att_b = _attention(qkv)

        zpad_t = lambda w, rows_before: jnp.pad(
            w, ((rows_before, LANES - rows_before - w.shape[0]), (0, 0))).T.astype(BF16)
        decay_up_p = zpad_t(decay_up[l], 0)
        aaa_up_p = zpad_t(aaa_up[l], DECAY_LORA)
        gate_up_p = gate_up[l].T.astype(BF16)
        if has_vres:
            vres_up_p = zpad_t(vres_up[l - 1], 0)
            vres_b = vres_base[l - 1]
            vf_in = v_first
        else:
            vres_up_p = jnp.zeros((wdt, LANES), BF16)
            vres_b = jnp.zeros((wdt,), F32)
            vf_in = rkv_t
        vecs = columns(jnp.stack([decay_base[l], aaa_base[l], vres_b, k_k[l], k_a[l], r_k[l]]))
        r_, w_, k_, a_, b_, vl_, v_, g_, bonus = _rwkv_prep(
            rkv_t, lora_t, vf_in, decay_up_p, aaa_up_p, gate_up_p, vres_up_p, vecs, seg, has_vres, b)
        if not has_vres:
            v_first = v_
        y_lanes = _rwkv_scan(a_, w_, b_, k_, r_, vl_)

        gn_vec = columns(jnp.stack([lnx_g[l], lnx_b[l]]))
        ln1 = jnp.stack([ln1_g[l], ln1_b[l]] + [jnp.zeros((d,), F32)] * 6)
        ln2 = jnp.stack([ln2_g[l], ln2_b[l]] + [jnp.zeros((d,), F32)] * 6)
        wo = w_out[l].astype(BF16)
        x2, xb = _outproj(att_b, y_lanes, bonus, g_, x2, wo[:att], wo[att:], gn_vec, ln1, seg, alpha)

        i = l // 2
        if l % 2 == 0:
            x2 = _swiglu(zero_tiles + i, xb, *ffn_w, ffn_tm, _ffn_chunk(ffn_w[0].shape[2]),
                         residual=x2, ln_vec=ln2, alpha=alpha)
        else:
            x2 = _moe(x2, xb, router[i], *moe_w, i * n_exp, n_exp, ln2, alpha, ffn_tm)
    return x2.reshape(s, b, d).transpose(1, 0, 2)
```

```python
import functools

import jax
import jax.numpy as jnp
from jax import lax
from jax.experimental import pallas as pl
from jax.experimental.pallas import tpu as pltpu

ATT_HEAD_DIM = 64
RWKV_HEAD_DIM = 64
DILATIONS = (1, 4, 16)
ATT_BLOCK = 128
DECAY_LORA = 64
N_EXPERTS = 8
LN_EPS = 1e-5
GN_EPS = 64e-5

LANES = 128
LORA_PAD = 384
VMEM_LIMIT = 56 * 1024 * 1024
NEG_BIG = -1e30

F32 = jnp.float32
BF16 = jnp.bfloat16


def _params(*sem):
    return pltpu.CompilerParams(dimension_semantics=sem, vmem_limit_bytes=VMEM_LIMIT)


def _inproj_kernel(x_ref, wqkv_ref, wrkv_ref, wlora_ref, mu_rkv_ref, mu_lora_ref,
                   qkv_ref, rkv_ref, lora_ref, carry_rkv, carry_lora, qkv_rows, *, batch):
    xb = x_ref[...].astype(BF16)
    qkv = jnp.dot(xb, wqkv_ref[...], preferred_element_type=F32)
    n_tiles, rows, _ = qkv_rows.shape
    for j in range(n_tiles):
        qkv_rows[j] = qkv[:, j * LANES:(j + 1) * LANES]
    for bi in range(batch):
        for j in range(n_tiles):
            qkv_ref[bi, :, j * LANES:(j + 1) * LANES] = qkv_rows[j, pl.ds(bi, rows // batch, stride=batch), :]

    @pl.when(pl.program_id(0) == 0)
    def _():
        carry_rkv[...] = jnp.zeros_like(carry_rkv)
        carry_lora[...] = jnp.zeros_like(carry_lora)

    nt = (((1,), (1,)), ((), ()))
    lane = lax.broadcasted_iota(jnp.int32, (1, LANES), 1)

    def shifted(wt_ref, mu_ref, carry_ref, out_ref):
        p = lax.dot_general(wt_ref[...], xb, nt, preferred_element_type=F32)
        tm = p.shape[1]
        mu = mu_ref[...]
        prev = pltpu.roll(p, batch, axis=1)
        from_carry = pltpu.roll(carry_ref[...], batch, axis=1)
        for j in range(tm // LANES):
            cols = slice(j * LANES, (j + 1) * LANES)
            prev_j = prev[:, cols]
            if j == 0:
                prev_j = jnp.where(lane < batch, from_carry, prev_j)
            out_ref[:, cols] = p[:, cols] + (prev_j - p[:, cols]) * mu
        carry_ref[...] = p[:, tm - LANES:tm]

    shifted(wrkv_ref, mu_rkv_ref, carry_rkv, rkv_ref)
    shifted(wlora_ref, mu_lora_ref, carry_lora, lora_ref)


def _inproj(x2, wqkv, wrkv_t, wlora_t, mu_rkv, mu_lora, batch, tm=512):
    m, d = x2.shape
    nq, nr, nl = wqkv.shape[1], wrkv_t.shape[0], wlora_t.shape[0]
    tm = min(tm, m)
    full = lambda shape: pl.BlockSpec(shape, lambda i: (0, 0))
    return pl.pallas_call(
        functools.partial(_inproj_kernel, batch=batch),
        grid=(m // tm,),
        in_specs=[pl.BlockSpec((tm, d), lambda i: (i, 0)),
                  full((d, nq)), full((nr, d)), full((nl, d)), full((nr, LANES)), full((nl, LANES))],
        out_specs=[pl.BlockSpec((batch, tm // batch, nq), lambda i: (0, i, 0)),
                   pl.BlockSpec((nr, tm), lambda i: (0, i)),
                   pl.BlockSpec((nl, tm), lambda i: (0, i))],
        out_shape=[jax.ShapeDtypeStruct((batch, m // batch, nq), F32),
                   jax.ShapeDtypeStruct((nr, m), F32),
                   jax.ShapeDtypeStruct((nl, m), F32)],
        scratch_shapes=[pltpu.VMEM((nr, LANES), F32), pltpu.VMEM((nl, LANES), F32),
                        pltpu.VMEM((nq // LANES, tm, LANES), F32)],
        compiler_params=_params("arbitrary"),
        name="inproj",
    )(x2, wqkv, wrkv_t, wlora_t, mu_rkv, mu_lora)


CLASS_COPY_ROWS = 256
ATT_WINDOW = 2 * ATT_BLOCK
ATT_HEADS_PER_TILE = LANES // ATT_HEAD_DIM


def _attn_kernel(q_ref, k_ref, v_ref, out_ref, acc_ref, m_ref, l_ref, qc_ref, kc_ref, vc_ref,
                 q4_ref, k4_ref, v4_ref, tmp_ref,
                 s0_scr, s1_scr, p0_scr, p1_scr, ml0_scr, ml1_scr, bias_scr):
    seq = q_ref.shape[0]
    n_blocks = seq // ATT_BLOCK
    base = DILATIONS[1]
    sub = DILATIONS[2] // base
    qlen = seq // base
    lane = lax.broadcasted_iota(jnp.int32, (ATT_BLOCK, LANES), 1)
    in_head = [(lane >= h * ATT_HEAD_DIM) & (lane < (h + 1) * ATT_HEAD_DIM)
               for h in range(ATT_HEADS_PER_TILE)]
    rel = (lax.broadcasted_iota(jnp.int32, (ATT_BLOCK, ATT_WINDOW), 0)
           - lax.broadcasted_iota(jnp.int32, (ATT_BLOCK, ATT_WINDOW), 1))
    nt = (((1,), (1,)), ((), ()))
    sources = ((q_ref, q4_ref, qc_ref), (k_ref, k4_ref, kc_ref), (v_ref, v4_ref, vc_ref))

    s_scr, p_scr, ml_scr = (s0_scr, s1_scr), (p0_scr, p1_scr), (ml0_scr, ml1_scr)
    for ref in s_scr + p_scr + ml_scr:
        ref[...] = jnp.zeros_like(ref)
    for first_window in range(2):
        dist = rel + first_window * ATT_BLOCK
        bias_scr[first_window] = jnp.where((dist >= 0) & (dist <= ATT_BLOCK), 0.0, NEG_BIG)

    def split_by_base(c, carry):
        for j in range(qlen // CLASS_COPY_ROWS):
            src = pl.ds(c + j * CLASS_COPY_ROWS * base, CLASS_COPY_ROWS, stride=base)
            dst = pl.ds(pl.multiple_of(c * qlen, CLASS_COPY_ROWS) + j * CLASS_COPY_ROWS, CLASS_COPY_ROWS)
            for full_ref, split_ref, _ in sources:
                split_ref[dst, :] = full_ref[src, :]
        return carry

    lax.fori_loop(0, base, split_by_base, 0)

    def run_pattern(per_class, commit):
        def locate(g):
            g = jnp.clip(g, 0, n_blocks - 1)
            first = (g % per_class) == 0
            q_start = pl.multiple_of(g * ATT_BLOCK, ATT_BLOCK)
            k_start = pl.multiple_of(jnp.where(first, g, g - 1) * ATT_BLOCK, ATT_BLOCK)
            return g, q_start, k_start

        def scores(g, slot):
            _, q_start, k_start = locate(g)
            q = qc_ref[pl.ds(q_start, ATT_BLOCK), :]
            kw = kc_ref[pl.ds(k_start, ATT_WINDOW), :]
            bias = bias_scr[(q_start - k_start) // ATT_BLOCK]
            for h in range(ATT_HEADS_PER_TILE):
                qh = jnp.where(in_head[h], q, jnp.zeros_like(q))
                s_scr[slot][h] = lax.dot_general(qh, kw, nt, preferred_element_type=F32) + bias

        def softmax(slot):
            m_acc = l_acc = jnp.zeros((ATT_BLOCK, LANES), F32)
            for h in range(ATT_HEADS_PER_TILE):
                sc = s_scr[slot][h]
                m = jnp.max(sc, axis=-1, keepdims=True)
                p = jnp.exp(sc - m)
                p_scr[slot][h] = p.astype(BF16)
                m_acc = jnp.where(in_head[h], m, m_acc)
                l_acc = jnp.where(in_head[h], jnp.sum(p, axis=-1, keepdims=True), l_acc)
            ml_scr[slot][0] = m_acc
            ml_scr[slot][1] = l_acc

        def accumulate(g, slot):
            valid = g >= 0
            g, _, k_start = locate(g)
            vw = vc_ref[pl.ds(k_start, ATT_WINDOW), :]
            o = jnp.zeros((ATT_BLOCK, LANES), F32)
            for h in range(ATT_HEADS_PER_TILE):
                o = jnp.where(in_head[h], jnp.dot(p_scr[slot][h], vw, preferred_element_type=F32), o)
            commit(g, valid, o, ml_scr[slot][0], ml_scr[slot][1])

        def pipeline_pair(it, carry):
            for parity in range(2):
                i = 2 * it + parity
                accumulate(i - 4, parity)
                softmax(parity)
                scores(i, parity)
            return carry

        lax.fori_loop(0, (n_blocks + 4) // 2, pipeline_pair, 0)

    def merge(rows, valid, o, m_blk, l_blk):
        m_blk = jnp.where(valid, m_blk, NEG_BIG)
        m_old = m_ref[rows, :]
        m_new = jnp.maximum(m_old, m_blk)
        w_old = jnp.exp(m_old - m_new)
        w_new = jnp.where(valid, jnp.exp(m_blk - m_new), 0.0)
        acc_ref[rows, :] = acc_ref[rows, :] * w_old + o * w_new
        l_ref[rows, :] = l_ref[rows, :] * w_old + l_blk * w_new
        m_ref[rows, :] = m_new

    for j in range(seq // CLASS_COPY_ROWS):
        rows = pl.ds(j * CLASS_COPY_ROWS, CLASS_COPY_ROWS)
        for full_ref, _, class_ref in sources:
            class_ref[rows, :] = full_ref[rows, :].astype(BF16)

    def commit_dense(g, valid, o, m_blk, l_blk):
        del valid
        per_base = ATT_BLOCK // base
        for idx, val in enumerate((o, m_blk, l_blk)):
            tmp_ref[idx] = val
        for c in range(base):
            dst = pl.ds(pl.multiple_of(c * qlen + g * per_base, per_base), per_base)
            for idx, state_ref in enumerate((acc_ref, m_ref, l_ref)):
                state_ref[dst, :] = tmp_ref[idx, pl.ds(c, per_base, stride=base), :]

    run_pattern(n_blocks, commit_dense)

    for j in range(seq // CLASS_COPY_ROWS):
        rows = pl.ds(j * CLASS_COPY_ROWS, CLASS_COPY_ROWS)
        for _, split_ref, class_ref in sources:
            class_ref[rows, :] = split_ref[rows, :].astype(BF16)

    def commit_base(g, valid, o, m_blk, l_blk):
        merge(pl.ds(pl.multiple_of(g * ATT_BLOCK, ATT_BLOCK), ATT_BLOCK), valid, o, m_blk, l_blk)

    run_pattern(qlen // ATT_BLOCK, commit_base)

    sub_len = qlen // sub
    per_sub = sub_len // ATT_BLOCK

    def split_again(cls, carry):
        src = pl.ds((cls // sub) * qlen + cls % sub, sub_len, stride=sub)
        dst = pl.ds(pl.multiple_of(cls * sub_len, CLASS_COPY_ROWS), sub_len)
        for _, split_ref, class_ref in sources:
            class_ref[dst, :] = split_ref[src, :].astype(BF16)
        return carry

    lax.fori_loop(0, base * sub, split_again, 0)

    def commit_sub(g, valid, o, m_blk, l_blk):
        cls = g // per_sub
        start = (cls // sub) * qlen + cls % sub + (g % per_sub) * (ATT_BLOCK * sub)
        merge(pl.ds(start, ATT_BLOCK, stride=sub), valid, o, m_blk, l_blk)

    run_pattern(per_sub, commit_sub)

    for c in range(base):
        for j in range(qlen // CLASS_COPY_ROWS):
            rows = pl.ds(c * qlen + j * CLASS_COPY_ROWS, CLASS_COPY_ROWS)
            out_ref[pl.ds(c + j * CLASS_COPY_ROWS * base, CLASS_COPY_ROWS, stride=base), :] = (
                acc_ref[rows, :] / l_ref[rows, :])


def _attention(qkv_b):
    b, s, c3 = qkv_b.shape
    a = c3 // 3
    npair = a // LANES
    assert s % (DILATIONS[-1] * ATT_WINDOW) == 0 and (s // DILATIONS[-1]) % CLASS_COPY_ROWS == 0
    blk = lambda off: pl.BlockSpec((None, s, LANES), lambda bi, hp: (bi, 0, off + hp))
    pipe = (ATT_HEADS_PER_TILE, ATT_BLOCK, ATT_WINDOW)
    return pl.pallas_call(
        _attn_kernel,
        grid=(b, npair),
        in_specs=[blk(0), blk(npair), blk(2 * npair)],
        out_specs=blk(0),
        out_shape=jax.ShapeDtypeStruct((b, s, a), F32),
        scratch_shapes=[pltpu.VMEM((s, LANES), F32)] * 3 + [pltpu.VMEM((s, LANES), BF16)] * 3
                       + [pltpu.VMEM((s, LANES), F32)] * 3 + [pltpu.VMEM((3, ATT_BLOCK, LANES), F32)]
                       + [pltpu.VMEM(pipe, F32)] * 2 + [pltpu.VMEM(pipe, BF16)] * 2
                       + [pltpu.VMEM((2, ATT_BLOCK, LANES), F32)] * 2
                       + [pltpu.VMEM((2, ATT_BLOCK, ATT_WINDOW), F32)],
        compiler_params=_params("arbitrary", "arbitrary"),
        name="dilated_attn",
    )(qkv_b, qkv_b, qkv_b)


KEY_HALVES = 2
SEG_ROWS = 256
RECURRENCE_CHUNK = 32


def _chunk_transpose(parts, chunk):
    n = len(parts)
    lane = lax.broadcasted_iota(jnp.int32, parts[0].shape, 1)
    s = n // 2
    while s >= 1:
        swap = ((lane // chunk) & s) != 0
        new = list(parts)
        for j in range(n):
            if j & s == 0:
                lo, hi = parts[j], parts[j + s]
                new[j] = jnp.where(swap, pltpu.roll(hi, s * chunk, axis=1), lo)
                new[j + s] = jnp.where(swap, hi, pltpu.roll(lo, LANES - s * chunk, axis=1))
        parts = new
        s //= 2
    return parts


def _keys_to_lanes(q, heads, batch):
    half = RWKV_HEAD_DIM // KEY_HALVES
    parts = [q[(h * KEY_HALVES + kh) * half:(h * KEY_HALVES + kh + 1) * half, :]
             for kh in range(KEY_HALVES) for h in range(heads)]
    return _chunk_transpose(parts, batch)


def _values_to_lanes(q, heads, batch):
    parts = [q[h * RWKV_HEAD_DIM:(h + 1) * RWKV_HEAD_DIM, :]
             for _ in range(KEY_HALVES) for h in range(heads)]
    return _chunk_transpose(parts, batch)


def _values_from_lanes(parts, heads, batch):
    per_half = _chunk_transpose(parts, batch)
    return jnp.concatenate([per_half[h] + per_half[heads + h] for h in range(heads)], axis=0)


def _segment_sum_t(x, seg_ref):
    seg = seg_ref[...]
    n = seg.shape[0]
    pieces = _bf16_pieces(x)
    slabs = []
    for s0 in range(0, x.shape[0], n):
        rows = slice(s0, s0 + n)
        slabs.append(sum(jnp.dot(seg, piece[rows], preferred_element_type=F32) for piece in pieces))
    return jnp.concatenate(slabs, axis=0)


def _bf16_pieces(x):
    hi = x.astype(BF16)
    rest = x - hi.astype(F32)
    mid = rest.astype(BF16)
    lo = (rest - mid.astype(F32)).astype(BF16)
    return hi, mid, lo


def _times_01_matrix(x, mat_ref):
    mat = mat_ref[...]
    return sum(jnp.dot(piece, mat, preferred_element_type=F32) for piece in _bf16_pieces(x))


def _last_step_to_lanes(q, heads, batch):
    half = RWKV_HEAD_DIM // KEY_HALVES
    n = LANES // batch
    lane = lax.broadcasted_iota(jnp.int32, (half, LANES), 1)
    out = jnp.zeros((half, LANES), F32)
    for kh in range(KEY_HALVES):
        for h in range(heads):
            j = kh * heads + h
            slab = q[(h * KEY_HALVES + kh) * half:(h * KEY_HALVES + kh + 1) * half, :]
            moved = pltpu.roll(slab, ((j - (n - 1)) * batch) % LANES, axis=1)
            out = jnp.where(lane // batch == j, moved, out)
    return out


def _rwkv_prep_kernel(rkv_ref, lora_ref, vfirst_ref, decay_up_ref, aaa_up_ref, gate_up_ref,
                      vres_up_ref, vec_ref, seg_ref, upto_ref, whole_ref,
                      r_ref, k_ref, a_ref, b_ref, pt_ref, vl_ref, v_ref, g_ref, bonus_ref,
                      *, has_vres, batch):
    wdt = rkv_ref.shape[0] // 3
    heads = wdt // RWKV_HEAD_DIM
    steps = LANES // batch
    n_groups = rkv_ref.shape[1] // LANES
    decay_base, aaa_base, vres_base = vec_ref[0], vec_ref[1], vec_ref[2]
    k_k, k_a, r_k = vec_ref[3], vec_ref[4], vec_ref[5]
    log_p_before = 0.0
    for gi in range(n_groups):
        cols = slice(gi * LANES, (gi + 1) * LANES)
        r = rkv_ref[0:wdt, cols]
        k = rkv_ref[wdt:2 * wdt, cols]
        v = rkv_ref[2 * wdt:3 * wdt, cols]
        wa = lora_ref[0:LANES, cols]
        gd = lora_ref[LANES:2 * LANES, cols]

        z = -(decay_base + jnp.dot(decay_up_ref[...], jnp.tanh(wa).astype(BF16),
                                   preferred_element_type=F32))
        softplus = jnp.maximum(z, 0.0) + jnp.log(1.0 + jnp.exp(-jnp.abs(z)))
        log_w = -jnp.exp(-softplus - 0.5)
        log_p = log_p_before + _times_01_matrix(log_w, upto_ref)
        if gi + 1 < n_groups:
            log_p_before = log_p_before + _times_01_matrix(log_w, whole_ref)
        p_t = jnp.exp(log_p)
        inv_p_t = jnp.exp(-log_p)
        p_prev = jnp.exp(log_p - log_w)
        a = jax.nn.sigmoid(aaa_base + jnp.dot(aaa_up_ref[...], wa.astype(BF16),
                                              preferred_element_type=F32))
        g_ref[:, cols] = jnp.dot(gate_up_ref[...], jax.nn.sigmoid(gd).astype(BF16),
                                 preferred_element_type=F32)
        if has_vres:
            vd = lora_ref[2 * LANES:3 * LANES, cols]
            mix = jax.nn.sigmoid(vres_base + jnp.dot(vres_up_ref[...], vd.astype(BF16),
                                                     preferred_element_type=F32))
            v = v + (vfirst_ref[:, cols] - v) * mix
        kk = k * k_k
        norm = jnp.sqrt(_segment_sum_t(kk * kk, seg_ref))
        kk = kk / jnp.maximum(norm, 1e-12)
        k2 = k * (1.0 + (a - 1.0) * k_a)
        v_ref[:, cols] = v
        bonus_ref[:, cols] = _segment_sum_t(r * k2 * r_k, seg_ref) * v

        for ref, val in ((r_ref, r * p_t), (k_ref, k2 * inv_p_t), (a_ref, -kk * p_prev),
                         (b_ref, kk * a * inv_p_t)):
            for t, part in enumerate(_keys_to_lanes(val, heads, batch)):
                ref[gi * steps + t] = part
        for t, part in enumerate(_values_to_lanes(v, heads, batch)):
            vl_ref[gi * steps + t] = part
        if gi + 1 == n_groups:
            pt_ref[0] = _last_step_to_lanes(p_t, heads, batch)


def _rwkv_prep(rkv_t, lora_t, vfirst_t, decay_up_t, aaa_up_t, gate_up_t, vres_up_t, vecs, seg,
               has_vres, batch):
    m = rkv_t.shape[1]
    wdt = rkv_t.shape[0] // 3
    heads = wdt // RWKV_HEAD_DIM
    assert KEY_HALVES * heads * batch == LANES, "recurrence layout needs 2*heads*batch == 128 lanes"
    steps = RECURRENCE_CHUNK
    tm = steps * batch
    seq = m // batch
    assert tm % LANES == 0 and seq % steps == 0
    t_idx, b_idx = jnp.arange(LANES) // batch, jnp.arange(LANES) % batch
    same_batch = b_idx[:, None] == b_idx[None, :]
    upto = (same_batch & (t_idx[:, None] <= t_idx[None, :])).astype(BF16)
    whole = same_batch.astype(BF16)
    half = RWKV_HEAD_DIM // KEY_HALVES
    cols = lambda n: pl.BlockSpec((n, tm), lambda i: (0, i))
    full = lambda arr: pl.BlockSpec(arr.shape, lambda i: (0,) * arr.ndim)
    key_spec = pl.BlockSpec((steps, half, LANES), lambda i: (i, 0, 0))
    val_spec = pl.BlockSpec((steps, RWKV_HEAD_DIM, LANES), lambda i: (i, 0, 0))
    key_out = jax.ShapeDtypeStruct((seq, half, LANES), F32)
    val_out = jax.ShapeDtypeStruct((seq, RWKV_HEAD_DIM, LANES), F32)
    nat_out = jax.ShapeDtypeStruct((wdt, m), F32)
    return pl.pallas_call(
        functools.partial(_rwkv_prep_kernel, has_vres=has_vres, batch=batch),
        grid=(m // tm,),
        in_specs=[cols(3 * wdt), cols(LORA_PAD), cols(wdt), full(decay_up_t), full(aaa_up_t),
                  full(gate_up_t), full(vres_up_t), full(vecs), full(seg), full(upto), full(whole)],
        out_specs=[key_spec] * 4 + [pl.BlockSpec((1, half, LANES), lambda i: (i, 0, 0))]
                  + [val_spec] + [cols(wdt)] * 3,
        out_shape=[key_out] * 4 + [jax.ShapeDtypeStruct((seq // steps, half, LANES), F32)]
                  + [val_out] + [nat_out] * 3,
        compiler_params=_params("arbitrary"),
        name="rwkv_prep",
    )(rkv_t, lora_t, vfirst_t, decay_up_t, aaa_up_t, gate_up_t, vres_up_t, vecs, seg, upto, whole)


def _rwkv_scan_kernel(a_ref, a_next_ref, b_ref, k_ref, r_ref, pt_ref, v_ref, y_ref, state_ref, sa_ref):
    @pl.when(pl.program_id(0) == 0)
    def _():
        state_ref[...] = jnp.zeros_like(state_ref)
        sa_ref[...] = jnp.zeros_like(sa_ref)

    steps, nk, _ = a_ref.shape
    nv = v_ref.shape[1]

    def step(t, sa_half, next_a_row, chunk_end):
        sa = sa_half + pltpu.roll(sa_half, LANES // KEY_HALVES, axis=1)
        row = lambda ref, kx: jnp.broadcast_to(ref[t, pl.ds(kx, 1), :], (nv, LANES))
        v_t = v_ref[t]
        acc_y = None
        acc_sa = None
        for kx in range(nk):
            new = state_ref[kx] + v_t * row(k_ref, kx) + sa * row(b_ref, kx)
            term_y = new * row(r_ref, kx)
            if chunk_end:
                new = new * jnp.broadcast_to(pt_ref[0, pl.ds(kx, 1), :], (nv, LANES))
            state_ref[kx] = new
            term_sa = new * jnp.broadcast_to(next_a_row(kx), (nv, LANES))
            acc_y = term_y if acc_y is None else acc_y + term_y
            acc_sa = term_sa if acc_sa is None else acc_sa + term_sa
        y_ref[t] = acc_y
        return acc_sa

    def body(t, sa_half):
        return step(t, sa_half, lambda kx: a_ref[t + 1, pl.ds(kx, 1), :], False)

    sa_half = lax.fori_loop(0, steps - 1, body, sa_ref[...], unroll=2)
    sa_ref[...] = step(steps - 1, sa_half, lambda kx: a_next_ref[0, pl.ds(kx, 1), :], True)


def _rwkv_scan(a_t, b_t, k_t, r_t, pt, v_t):
    s, nk, _ = a_t.shape
    nv = v_t.shape[1]
    steps = RECURRENCE_CHUNK
    n_blocks = s // steps
    kin = pl.BlockSpec((steps, nk, LANES), lambda i: (i, 0, 0))
    knext = pl.BlockSpec((1, nk, LANES), lambda i: (jnp.minimum((i + 1) * steps, s - 1), 0, 0))
    pin = pl.BlockSpec((1, nk, LANES), lambda i: (i, 0, 0))
    vin = pl.BlockSpec((steps, nv, LANES), lambda i: (i, 0, 0))
    return pl.pallas_call(
        _rwkv_scan_kernel,
        grid=(n_blocks,),
        in_specs=[kin, knext] + [kin] * 3 + [pin, vin],
        out_specs=vin,
        out_shape=jax.ShapeDtypeStruct((s, nv, LANES), F32),
        scratch_shapes=[pltpu.VMEM((nk, nv, LANES), F32), pltpu.VMEM((nv, LANES), F32)],
        compiler_params=_params("arbitrary"),
        name="rwkv_scan",
    )(a_t, a_t, b_t, k_t, r_t, pt, v_t)


def _layer_norm(z, g, b):
    mu = jnp.mean(z, axis=-1, keepdims=True)
    zc = z - mu
    var = jnp.mean(zc * zc, axis=-1, keepdims=True)
    return zc * lax.rsqrt(var + LN_EPS) * g + b


def _outproj_kernel(att_ref, y_ref, bonus_ref, g_ref, x_ref, wa_ref, wr_ref, vec_ref, ln_ref,
                    seg_ref, out_ref, outb_ref, att_rows, *, alpha, batch):
    n_tiles = att_rows.shape[0]
    for bi in range(batch):
        for j in range(n_tiles):
            att_rows[j, pl.ds(bi, att_ref.shape[1], stride=batch), :] = att_ref[bi, :, j * LANES:(j + 1) * LANES]
    att = jnp.concatenate([att_rows[j] for j in range(n_tiles)], axis=1)

    heads = bonus_ref.shape[0] // RWKV_HEAD_DIM
    steps = LANES // batch
    inv_n = 1.0 / RWKV_HEAD_DIM
    n_groups = y_ref.shape[0] // steps
    wide = lambda col: jnp.concatenate([col] * n_groups, axis=1)
    y = jnp.concatenate(
        [_values_from_lanes([y_ref[gi * steps + t] for t in range(steps)], heads, batch)
         for gi in range(n_groups)], axis=1)
    mu = _segment_sum_t(y, seg_ref) * inv_n
    yc = y - mu
    var = _segment_sum_t(yc * yc, seg_ref) * inv_n
    yn = yc * lax.rsqrt(var + GN_EPS) * wide(vec_ref[0]) + wide(vec_ref[1])
    rw = ((yn + bonus_ref[...]) * g_ref[...]).T

    mix = (jnp.dot(att.astype(BF16), wa_ref[...], preferred_element_type=F32)
           + jnp.dot(rw.astype(BF16), wr_ref[...], preferred_element_type=F32))
    out = _layer_norm(alpha * x_ref[...] + mix, ln_ref[0:1, :], ln_ref[1:2, :])
    out_ref[...] = out
    outb_ref[...] = out.astype(BF16)


def _outproj(att_b, y_lanes, bonus, g, x2, wa, wr, gn_vec, ln_vec, seg, alpha, tm=256):
    m, d = x2.shape
    batch, _, a = att_b.shape
    tm = min(tm, m)
    assert tm % LANES == 0
    rows = lambda n: pl.BlockSpec((tm, n), lambda i: (i, 0))
    cols = lambda arr: pl.BlockSpec((arr.shape[0], tm), lambda i: (0, i))
    full = lambda arr: pl.BlockSpec(arr.shape, lambda i: (0,) * arr.ndim)
    att_spec = pl.BlockSpec((batch, tm // batch, a), lambda i: (0, i, 0))
    y_spec = pl.BlockSpec((tm // batch,) + y_lanes.shape[1:], lambda i: (i, 0, 0))
    return pl.pallas_call(
        functools.partial(_outproj_kernel, alpha=alpha, batch=batch),
        grid=(m // tm,),
        in_specs=[att_spec, y_spec, cols(bonus), cols(g)]
                 + [rows(d), full(wa), full(wr), full(gn_vec), full(ln_vec), full(seg)],
        out_specs=[rows(d), rows(d)],
        out_shape=[jax.ShapeDtypeStruct((m, d), F32), jax.ShapeDtypeStruct((m, d), BF16)],
        scratch_shapes=[pltpu.VMEM((a // LANES, tm, LANES), F32)],
        compiler_params=_params("arbitrary"),
        name="outproj_ln",
    )(att_b, y_lanes, bonus, g, x2, wa, wr, gn_vec, ln_vec, seg)


def _swiglu_kernel(expert_ref, x_ref, wg_ref, wu_ref, wd_ref, *rest, alpha):
    if alpha is None:
        out_ref, acc_ref = rest
    else:
        res_ref, ln_ref, out_ref, acc_ref = rest
    j = pl.program_id(1)

    @pl.when(j == 0)
    def _():
        acc_ref[...] = jnp.zeros_like(acc_ref)

    x = x_ref[...]
    gate = jnp.dot(x, wg_ref[0], preferred_element_type=F32)
    up = jnp.dot(x, wu_ref[0], preferred_element_type=F32)
    h = (gate * jax.nn.sigmoid(gate) * up).astype(BF16)
    acc_ref[...] += jnp.dot(h, wd_ref[0], preferred_element_type=F32)

    @pl.when(j == pl.num_programs(1) - 1)
    def _():
        if alpha is None:
            out_ref[...] = acc_ref[...]
        else:
            out_ref[...] = _layer_norm(alpha * res_ref[...] + acc_ref[...], ln_ref[0:1, :], ln_ref[1:2, :])


def _swiglu(tile_expert, xb, wg, wu, wd, tm, fc, residual=None, ln_vec=None, alpha=None):
    m, d = xb.shape
    f = wg.shape[2]
    rows = pl.BlockSpec((tm, d), lambda i, j, e: (i, 0))
    extra_specs, extra_args = [], []
    if residual is not None:
        extra_specs = [rows, pl.BlockSpec(ln_vec.shape, lambda i, j, e: (0, 0))]
        extra_args = [residual, ln_vec]
    return pl.pallas_call(
        functools.partial(_swiglu_kernel, alpha=alpha),
        grid_spec=pltpu.PrefetchScalarGridSpec(
            num_scalar_prefetch=1,
            grid=(m // tm, f // fc),
            in_specs=[rows,
                      pl.BlockSpec((1, d, fc), lambda i, j, e: (e[i], 0, j)),
                      pl.BlockSpec((1, d, fc), lambda i, j, e: (e[i], 0, j)),
                      pl.BlockSpec((1, fc, d), lambda i, j, e: (e[i], j, 0))] + extra_specs,
            out_specs=rows,
            scratch_shapes=[pltpu.VMEM((tm, d), F32)]),
        out_shape=jax.ShapeDtypeStruct((m, d), F32),
        compiler_params=_params("arbitrary", "arbitrary"),
        name="swiglu",
    )(tile_expert, xb, wg, wu, wd, *extra_args)


MAX_FFN_CHUNK = 2048


def _ffn_chunk(f):
    if f % LANES:
        raise ValueError(f"feed-forward width {f} is not a multiple of {LANES}")
    tiles = f // LANES
    return LANES * max(n for n in range(1, tiles + 1) if tiles % n == 0 and n * LANES <= MAX_FFN_CHUNK)


def _router_kernel(x_ref, wr_ref, out_ref):
    logits = jnp.dot(x_ref[...], wr_ref[...], preferred_element_type=F32,
                     precision=lax.Precision.HIGHEST)
    lane = lax.broadcasted_iota(jnp.int32, logits.shape, 1)
    logits = jnp.where(lane < N_EXPERTS, logits, NEG_BIG)
    v1 = jnp.max(logits, axis=-1, keepdims=True)
    i1 = jnp.min(jnp.where(logits == v1, lane, LANES), axis=-1, keepdims=True)
    rest = jnp.where(lane == i1, NEG_BIG, logits)
    v2 = jnp.max(rest, axis=-1, keepdims=True)
    i2 = jnp.min(jnp.where(rest == v2, lane, LANES), axis=-1, keepdims=True)
    e2 = jnp.exp(v2 - v1)
    g1 = 1.0 / (1.0 + e2)
    g2 = e2 / (1.0 + e2)
    out_ref[...] = jnp.where(lane == 0, i1.astype(F32),
                             jnp.where(lane == 1, i2.astype(F32),
                                       jnp.where(lane == 2, g1, jnp.where(lane == 3, g2, 0.0))))


def _router(x2, wr_pad, tm=512):
    m, d = x2.shape
    tm = min(tm, m)
    return pl.pallas_call(
        _router_kernel,
        grid=(m // tm,),
        in_specs=[pl.BlockSpec((tm, d), lambda i: (i, 0)), pl.BlockSpec(wr_pad.shape, lambda i: (0, 0))],
        out_specs=pl.BlockSpec((tm, LANES), lambda i: (i, 0)),
        out_shape=jax.ShapeDtypeStruct((m, LANES), F32),
        compiler_params=_params("arbitrary"),
        name="router_top2",
    )(x2, wr_pad)


def _moe_add_ln_kernel(x_ref, y1_ref, y2_ref, route_ref, ln_ref, out_ref, *, alpha):
    g1 = route_ref[:, 2:3]
    g2 = route_ref[:, 3:4]
    f = g1 * y1_ref[...] + g2 * y2_ref[...]
    out_ref[...] = _layer_norm(alpha * x_ref[...] + f, ln_ref[0:1, :], ln_ref[1:2, :])


def _moe_add_ln(x2, y1, y2, route, ln_vec, alpha, tm=512):
    m, d = x2.shape
    tm = min(tm, m)
    rows = pl.BlockSpec((tm, d), lambda i: (i, 0))
    return pl.pallas_call(
        functools.partial(_moe_add_ln_kernel, alpha=alpha),
        grid=(m // tm,),
        in_specs=[rows, rows, rows, pl.BlockSpec((tm, LANES), lambda i: (i, 0)),
                  pl.BlockSpec(ln_vec.shape, lambda i: (0, 0))],
        out_specs=rows,
        out_shape=jax.ShapeDtypeStruct((m, d), F32),
        compiler_params=_params("arbitrary"),
        name="moe_add_ln",
    )(x2, y1, y2, route, ln_vec)


def _moe(x2, xb, router_w, wg, wu, wd, first_expert, n_exp, ln_vec, alpha, tm):
    m, d = x2.shape
    wr_pad = jnp.pad(router_w, ((0, 0), (0, LANES - n_exp)))
    route = _router(x2, wr_pad)
    idx = route[:, 0:2].astype(jnp.int32).reshape(-1)
    experts = jnp.arange(n_exp, dtype=jnp.int32)
    counts = jnp.sum((idx[:, None] == experts[None, :]).astype(jnp.int32), axis=0)
    padded = ((counts + tm - 1) // tm) * tm
    pad_end = jnp.cumsum(padded)
    pad_start = pad_end - padded
    start = jnp.cumsum(counts) - counts
    order = jnp.argsort(idx, stable=True)
    rank = jnp.argsort(order)
    slot_pos = pad_start[idx] + (rank - start[idx])
    rows_total = 2 * m + n_exp * tm
    tile_start = jnp.arange(rows_total // tm, dtype=jnp.int32) * tm
    tile_expert = jnp.minimum(jnp.sum((tile_start[:, None] >= pad_end[None, :]).astype(jnp.int32), axis=1),
                              n_exp - 1)
    row = jnp.arange(rows_total, dtype=jnp.int32)
    row_expert = jnp.repeat(tile_expert, tm)
    sorted_pos = jnp.clip(row - pad_start[row_expert] + start[row_expert], 0, 2 * m - 1)
    src_token = order[sorted_pos] // 2
    xg = jnp.take(xb, src_token, axis=0, mode="clip")
    yg = _swiglu(tile_expert + first_expert, xg, wg, wu, wd, tm, _ffn_chunk(wg.shape[2]))
    pos = slot_pos.reshape(m, 2)
    y1 = jnp.take(yg, pos[:, 0], axis=0, mode="clip")
    y2 = jnp.take(yg, pos[:, 1], axis=0, mode="clip")
    return _moe_add_ln(x2, y1, y2, route, ln_vec, alpha)


def kernel(x, w_in, w_in_vres, shift_mu, shift_mu_vres, decay_up, decay_base, aaa_up, aaa_base, vres_up, vres_base, gate_up, k_k, k_a, r_k, lnx_g, lnx_b, w_out, ln1_g, ln1_b, ln2_g, ln2_b, ffn_w_gate, ffn_w_up, ffn_w_down, router, moe_w_gate, moe_w_up, moe_w_down):
    b, s, d = x.shape
    depth = w_in.shape[0]
    wdt = decay_up.shape[2]
    att = w_out.shape[1] - wdt
    alpha = float((2 * depth) ** 0.25)
    m = b * s
    ffn_tm = min(512, m)

    seg_head = jnp.arange(SEG_ROWS) // RWKV_HEAD_DIM
    seg = (seg_head[:, None] == seg_head[None, :]).astype(BF16)
    q_scale = jnp.concatenate([jnp.full((att,), ATT_HEAD_DIM ** -0.5, F32), jnp.ones((2 * att,), F32)])
    zero_tiles = jnp.zeros((m // ffn_tm,), jnp.int32)
    ffn_w = [w.astype(BF16) for w in (ffn_w_gate, ffn_w_up, ffn_w_down)]
    n_exp = moe_w_gate.shape[1]
    moe_w = [w.astype(BF16).reshape((-1,) + w.shape[2:]) for w in (moe_w_gate, moe_w_up, moe_w_down)]

    x2 = x.transpose(1, 0, 2).reshape(m, d)
    v_first = None
    for l in range(depth):
        has_vres = l > 0
        w_l = w_in[l]
        c0 = 3 * att
        wqkv = (w_l[:, :c0] * q_scale).astype(BF16)
        wrkv_t = w_l[:, c0:c0 + 3 * wdt].T.astype(BF16)
        lora_w = w_l[:, c0 + 3 * wdt:]
        lora_mu = shift_mu[l, 3 * wdt:]
        if has_vres:
            lora_w = jnp.concatenate([lora_w, w_in_vres[l - 1]], axis=1)
            lora_mu = jnp.concatenate([lora_mu, shift_mu_vres[l - 1]])
        n_lora = lora_w.shape[1]
        wlora_t = jnp.pad(lora_w, ((0, 0), (0, LORA_PAD - n_lora))).T.astype(BF16)
        columns = lambda vec: jnp.broadcast_to(vec[..., None], vec.shape + (LANES,))
        mu_lora = columns(jnp.pad(lora_mu, (0, LORA_PAD - n_lora)))
        mu_rkv = columns(shift_mu[l, :3 * wdt])

        qkv, rkv_t, lora_t = _inproj(x2, wqkv, wrkv_t, wlora_t, mu_rkv, mu_lora, b)

        att_b = _attention(qkv)

        zpad_t = lambda w, rows_before: jnp.pad(
            w, ((rows_before, LANES - rows_before - w.shape[0]), (0, 0))).T.astype(BF16)
        decay_up_p = zpad_t(decay_up[l], 0)
        aaa_up_p = zpad_t(aaa_up[l], DECAY_LORA)
        gate_up_p = gate_up[l].T.astype(BF16)
        if has_vres:
            vres_up_p = zpad_t(vres_up[l - 1], 0)
            vres_b = vres_base[l - 1]
            vf_in = v_first
        else:
            vres_up_p = jnp.zeros((wdt, LANES), BF16)
            vres_b = jnp.zeros((wdt,), F32)
            vf_in = rkv_t
        vecs = columns(jnp.stack([decay_base[l], aaa_base[l], vres_b, k_k[l], k_a[l], r_k[l]]))
        r_, k_, a_, b_, pt_, vl_, v_, g_, bonus = _rwkv_prep(
            rkv_t, lora_t, vf_in, decay_up_p, aaa_up_p, gate_up_p, vres_up_p, vecs, seg, has_vres, b)
        if not has_vres:
            v_first = v_
        y_lanes = _rwkv_scan(a_, b_, k_, r_, pt_, vl_)

        gn_vec = columns(jnp.stack([lnx_g[l], lnx_b[l]]))
        ln1 = jnp.stack([ln1_g[l], ln1_b[l]] + [jnp.zeros((d,), F32)] * 6)
        ln2 = jnp.stack([ln2_g[l], ln2_b[l]] + [jnp.zeros((d,), F32)] * 6)
        wo = w_out[l].astype(BF16)
        x2, xb = _outproj(att_b, y_lanes, bonus, g_, x2, wo[:att], wo[att:], gn_vec, ln1, seg, alpha)

        i = l // 2
        if l % 2 == 0:
            x2 = _swiglu(zero_tiles + i, xb, *ffn_w, ffn_tm, _ffn_chunk(ffn_w[0].shape[2]),
                         residual=x2, ln_vec=ln2, alpha=alpha)
        else:
            x2 = _moe(x2, xb, router[i], *moe_w, i * n_exp, n_exp, ln2, alpha, ffn_tm)
    return x2.reshape(s, b, d).transpose(1, 0, 2)
```

```python
import functools

import jax
import jax.numpy as jnp
from jax import lax
from jax.experimental import pallas as pl
from jax.experimental.pallas import tpu as pltpu

ATT_HEAD_DIM = 64
RWKV_HEAD_DIM = 64
DILATIONS = (1, 4, 16)
ATT_BLOCK = 128
DECAY_LORA = 64
N_EXPERTS = 8
LN_EPS = 1e-5
GN_EPS = 64e-5

LANES = 128
LORA_PAD = 384
VMEM_LIMIT = 56 * 1024 * 1024
NEG_BIG = -1e30

F32 = jnp.float32
BF16 = jnp.bfloat16


def _params(*sem):
    return pltpu.CompilerParams(dimension_semantics=sem, vmem_limit_bytes=VMEM_LIMIT)


def _inproj_kernel(x_ref, wqkv_ref, wrkv_ref, wlora_ref, mu_rkv_ref, mu_lora_ref,
                   qkv_ref, rkv_ref, lora_ref, carry_rkv, carry_lora, qkv_rows, *, batch):
    xb = x_ref[...].astype(BF16)
    qkv = jnp.dot(xb, wqkv_ref[...], preferred_element_type=F32)
    n_tiles, rows, _ = qkv_rows.shape
    for j in range(n_tiles):
        qkv_rows[j] = qkv[:, j * LANES:(j + 1) * LANES]
    for bi in range(batch):
        for j in range(n_tiles):
            qkv_ref[bi, :, j * LANES:(j + 1) * LANES] = qkv_rows[j, pl.ds(bi, rows // batch, stride=batch), :]

    @pl.when(pl.program_id(0) == 0)
    def _():
        carry_rkv[...] = jnp.zeros_like(carry_rkv)
        carry_lora[...] = jnp.zeros_like(carry_lora)

    nt = (((1,), (1,)), ((), ()))
    lane = lax.broadcasted_iota(jnp.int32, (1, LANES), 1)

    def shifted(wt_ref, mu_ref, carry_ref, out_ref):
        p = lax.dot_general(wt_ref[...], xb, nt, preferred_element_type=F32)
        tm = p.shape[1]
        mu = mu_ref[...]
        prev = pltpu.roll(p, batch, axis=1)
        from_carry = pltpu.roll(carry_ref[...], batch, axis=1)
        for j in range(tm // LANES):
            cols = slice(j * LANES, (j + 1) * LANES)
            prev_j = prev[:, cols]
            if j == 0:
                prev_j = jnp.where(lane < batch, from_carry, prev_j)
            out_ref[:, cols] = p[:, cols] + (prev_j - p[:, cols]) * mu
        carry_ref[...] = p[:, tm - LANES:tm]

    shifted(wrkv_ref, mu_rkv_ref, carry_rkv, rkv_ref)
    shifted(wlora_ref, mu_lora_ref, carry_lora, lora_ref)


def _inproj(x2, wqkv, wrkv_t, wlora_t, mu_rkv, mu_lora, batch, tm=512):
    m, d = x2.shape
    nq, nr, nl = wqkv.shape[1], wrkv_t.shape[0], wlora_t.shape[0]
    tm = min(tm, m)
    full = lambda shape: pl.BlockSpec(shape, lambda i: (0, 0))
    return pl.pallas_call(
        functools.partial(_inproj_kernel, batch=batch),
        grid=(m // tm,),
        in_specs=[pl.BlockSpec((tm, d), lambda i: (i, 0)),
                  full((d, nq)), full((nr, d)), full((nl, d)), full((nr, LANES)), full((nl, LANES))],
        out_specs=[pl.BlockSpec((batch, tm // batch, nq), lambda i: (0, i, 0)),
                   pl.BlockSpec((nr, tm), lambda i: (0, i)),
                   pl.BlockSpec((nl, tm), lambda i: (0, i))],
        out_shape=[jax.ShapeDtypeStruct((batch, m // batch, nq), F32),
                   jax.ShapeDtypeStruct((nr, m), F32),
                   jax.ShapeDtypeStruct((nl, m), F32)],
        scratch_shapes=[pltpu.VMEM((nr, LANES), F32), pltpu.VMEM((nl, LANES), F32),
                        pltpu.VMEM((nq // LANES, tm, LANES), F32)],
        compiler_params=_params("arbitrary"),
        name="inproj",
    )(x2, wqkv, wrkv_t, wlora_t, mu_rkv, mu_lora)


CLASS_COPY_ROWS = 256
ATT_WINDOW = 2 * ATT_BLOCK
ATT_HEADS_PER_TILE = LANES // ATT_HEAD_DIM


def _attn_kernel(q_ref, k_ref, v_ref, out_ref, acc_ref, m_ref, l_ref, qc_ref, kc_ref, vc_ref,
                 q4_ref, k4_ref, v4_ref, tmp_ref,
                 s0_scr, s1_scr, p0_scr, p1_scr, ml0_scr, ml1_scr, bias_scr):
    seq = q_ref.shape[0]
    n_blocks = seq // ATT_BLOCK
    base = DILATIONS[1]
    sub = DILATIONS[2] // base
    qlen = seq // base
    lane = lax.broadcasted_iota(jnp.int32, (ATT_BLOCK, LANES), 1)
    in_head = [(lane >= h * ATT_HEAD_DIM) & (lane < (h + 1) * ATT_HEAD_DIM)
               for h in range(ATT_HEADS_PER_TILE)]
    rel = (lax.broadcasted_iota(jnp.int32, (ATT_BLOCK, ATT_WINDOW), 0)
           - lax.broadcasted_iota(jnp.int32, (ATT_BLOCK, ATT_WINDOW), 1))
    nt = (((1,), (1,)), ((), ()))
    sources = ((q_ref, q4_ref, qc_ref), (k_ref, k4_ref, kc_ref), (v_ref, v4_ref, vc_ref))

    s_scr, p_scr, ml_scr = (s0_scr, s1_scr), (p0_scr, p1_scr), (ml0_scr, ml1_scr)
    for ref in s_scr + p_scr + ml_scr:
        ref[...] = jnp.zeros_like(ref)
    for first_window in range(2):
        dist = rel + first_window * ATT_BLOCK
        bias_scr[first_window] = jnp.where((dist >= 0) & (dist <= ATT_BLOCK), 0.0, NEG_BIG)

    def split_by_base(c, carry):
        for j in range(qlen // CLASS_COPY_ROWS):
            src = pl.ds(c + j * CLASS_COPY_ROWS * base, CLASS_COPY_ROWS, stride=base)
            dst = pl.ds(pl.multiple_of(c * qlen, CLASS_COPY_ROWS) + j * CLASS_COPY_ROWS, CLASS_COPY_ROWS)
            for full_ref, split_ref, _ in sources:
                split_ref[dst, :] = full_ref[src, :]
        return carry

    lax.fori_loop(0, base, split_by_base, 0)

    def run_pattern(per_class, commit):
        def locate(g):
            g = jnp.clip(g, 0, n_blocks - 1)
            first = (g % per_class) == 0
            q_start = pl.multiple_of(g * ATT_BLOCK, ATT_BLOCK)
            k_start = pl.multiple_of(jnp.where(first, g, g - 1) * ATT_BLOCK, ATT_BLOCK)
            return g, q_start, k_start

        def scores(g, slot):
            _, q_start, k_start = locate(g)
            q = qc_ref[pl.ds(q_start, ATT_BLOCK), :]
            kw = kc_ref[pl.ds(k_start, ATT_WINDOW), :]
            bias = bias_scr[(q_start - k_start) // ATT_BLOCK]
            for h in range(ATT_HEADS_PER_TILE):
                qh = jnp.where(in_head[h], q, jnp.zeros_like(q))
                s_scr[slot][h] = lax.dot_general(qh, kw, nt, preferred_element_type=F32) + bias

        def softmax(slot):
            m_acc = l_acc = jnp.zeros((ATT_BLOCK, LANES), F32)
            for h in range(ATT_HEADS_PER_TILE):
                sc = s_scr[slot][h]
                m = jnp.max(sc, axis=-1, keepdims=True)
                p = jnp.exp(sc - m)
                p_scr[slot][h] = p.astype(BF16)
                m_acc = jnp.where(in_head[h], m, m_acc)
                l_acc = jnp.where(in_head[h], jnp.sum(p, axis=-1, keepdims=True), l_acc)
            ml_scr[slot][0] = m_acc
            ml_scr[slot][1] = l_acc

        def accumulate(g, slot):
            valid = g >= 0
            g, _, k_start = locate(g)
            vw = vc_ref[pl.ds(k_start, ATT_WINDOW), :]
            o = jnp.zeros((ATT_BLOCK, LANES), F32)
            for h in range(ATT_HEADS_PER_TILE):
                o = jnp.where(in_head[h], jnp.dot(p_scr[slot][h], vw, preferred_element_type=F32), o)
            commit(g, valid, o, ml_scr[slot][0], ml_scr[slot][1])

        def pipeline_pair(it, carry):
            for parity in range(2):
                i = 2 * it + parity
                accumulate(i - 4, parity)
                softmax(parity)
                scores(i, parity)
            return carry

        lax.fori_loop(0, (n_blocks + 4) // 2, pipeline_pair, 0)

    def merge(rows, valid, o, m_blk, l_blk):
        m_blk = jnp.where(valid, m_blk, NEG_BIG)
        m_old = m_ref[rows, :]
        m_new = jnp.maximum(m_old, m_blk)
        w_old = jnp.exp(m_old - m_new)
        w_new = jnp.where(valid, jnp.exp(m_blk - m_new), 0.0)
        acc_ref[rows, :] = acc_ref[rows, :] * w_old + o * w_new
        l_ref[rows, :] = l_ref[rows, :] * w_old + l_blk * w_new
        m_ref[rows, :] = m_new

    for j in range(seq // CLASS_COPY_ROWS):
        rows = pl.ds(j * CLASS_COPY_ROWS, CLASS_COPY_ROWS)
        for full_ref, _, class_ref in sources:
            class_ref[rows, :] = full_ref[rows, :].astype(BF16)

    def commit_dense(g, valid, o, m_blk, l_blk):
        del valid
        per_base = ATT_BLOCK // base
        for idx, val in enumerate((o, m_blk, l_blk)):
            tmp_ref[idx] = val
        for c in range(base):
            dst = pl.ds(pl.multiple_of(c * qlen + g * per_base, per_base), per_base)
            for idx, state_ref in enumerate((acc_ref, m_ref, l_ref)):
                state_ref[dst, :] = tmp_ref[idx, pl.ds(c, per_base, stride=base), :]

    run_pattern(n_blocks, commit_dense)

    for j in range(seq // CLASS_COPY_ROWS):
        rows = pl.ds(j * CLASS_COPY_ROWS, CLASS_COPY_ROWS)
        for _, split_ref, class_ref in sources:
            class_ref[rows, :] = split_ref[rows, :].astype(BF16)

    def commit_base(g, valid, o, m_blk, l_blk):
        merge(pl.ds(pl.multiple_of(g * ATT_BLOCK, ATT_BLOCK), ATT_BLOCK), valid, o, m_blk, l_blk)

    run_pattern(qlen // ATT_BLOCK, commit_base)

    sub_len = qlen // sub
    per_sub = sub_len // ATT_BLOCK

    def split_again(cls, carry):
        src = pl.ds((cls // sub) * qlen + cls % sub, sub_len, stride=sub)
        dst = pl.ds(pl.multiple_of(cls * sub_len, CLASS_COPY_ROWS), sub_len)
        for _, split_ref, class_ref in sources:
            class_ref[dst, :] = split_ref[src, :].astype(BF16)
        return carry

    lax.fori_loop(0, base * sub, split_again, 0)

    def commit_sub(g, valid, o, m_blk, l_blk):
        cls = g // per_sub
        start = (cls // sub) * qlen + cls % sub + (g % per_sub) * (ATT_BLOCK * sub)
        merge(pl.ds(start, ATT_BLOCK, stride=sub), valid, o, m_blk, l_blk)

    run_pattern(per_sub, commit_sub)

    for c in range(base):
        for j in range(qlen // CLASS_COPY_ROWS):
            rows = pl.ds(c * qlen + j * CLASS_COPY_ROWS, CLASS_COPY_ROWS)
            out_ref[pl.ds(c + j * CLASS_COPY_ROWS * base, CLASS_COPY_ROWS, stride=base), :] = (
                acc_ref[rows, :] / l_ref[rows, :])


def _attention(qkv_b):
    b, s, c3 = qkv_b.shape
    a = c3 // 3
    npair = a // LANES
    assert s % (DILATIONS[-1] * ATT_WINDOW) == 0 and (s // DILATIONS[-1]) % CLASS_COPY_ROWS == 0
    blk = lambda off: pl.BlockSpec((None, s, LANES), lambda bi, hp: (bi, 0, off + hp))
    pipe = (ATT_HEADS_PER_TILE, ATT_BLOCK, ATT_WINDOW)
    return pl.pallas_call(
        _attn_kernel,
        grid=(b, npair),
        in_specs=[blk(0), blk(npair), blk(2 * npair)],
        out_specs=blk(0),
        out_shape=jax.ShapeDtypeStruct((b, s, a), F32),
        scratch_shapes=[pltpu.VMEM((s, LANES), F32)] * 3 + [pltpu.VMEM((s, LANES), BF16)] * 3
                       + [pltpu.VMEM((s, LANES), F32)] * 3 + [pltpu.VMEM((3, ATT_BLOCK, LANES), F32)]
                       + [pltpu.VMEM(pipe, F32)] * 2 + [pltpu.VMEM(pipe, BF16)] * 2
                       + [pltpu.VMEM((2, ATT_BLOCK, LANES), F32)] * 2
                       + [pltpu.VMEM((2, ATT_BLOCK, ATT_WINDOW), F32)],
        compiler_params=_params("arbitrary", "arbitrary"),
        name="dilated_attn",
    )(qkv_b, qkv_b, qkv_b)


KEY_HALVES = 2
SEG_ROWS = 256
RECURRENCE_CHUNK = 32


def _chunk_transpose(parts, chunk):
    n = len(parts)
    lane = lax.broadcasted_iota(jnp.int32, parts[0].shape, 1)
    s = n // 2
    while s >= 1:
        swap = ((lane // chunk) & s) != 0
        new = list(parts)
        for j in range(n):
            if j & s == 0:
                lo, hi = parts[j], parts[j + s]
                new[j] = jnp.where(swap, pltpu.roll(hi, s * chunk, axis=1), lo)
                new[j + s] = jnp.where(swap, hi, pltpu.roll(lo, LANES - s * chunk, axis=1))
        parts = new
        s //= 2
    return parts


def _keys_to_lanes(q, heads, batch):
    half = RWKV_HEAD_DIM // KEY_HALVES
    parts = [q[(h * KEY_HALVES + kh) * half:(h * KEY_HALVES + kh + 1) * half, :]
             for kh in range(KEY_HALVES) for h in range(heads)]
    return _chunk_transpose(parts, batch)


def _values_to_lanes(q, heads, batch):
    parts = [q[h * RWKV_HEAD_DIM:(h + 1) * RWKV_HEAD_DIM, :]
             for _ in range(KEY_HALVES) for h in range(heads)]
    return _chunk_transpose(parts, batch)


def _values_from_lanes(parts, heads, batch):
    per_half = _chunk_transpose(parts, batch)
    return jnp.concatenate([per_half[h] + per_half[heads + h] for h in range(heads)], axis=0)


def _segment_sum_t(x, seg_ref):
    seg = seg_ref[...]
    n = seg.shape[0]
    pieces = _bf16_pieces(x)
    slabs = []
    for s0 in range(0, x.shape[0], n):
        rows = slice(s0, s0 + n)
        slabs.append(sum(jnp.dot(seg, piece[rows], preferred_element_type=F32) for piece in pieces))
    return jnp.concatenate(slabs, axis=0)


def _bf16_pieces(x):
    hi = x.astype(BF16)
    rest = x - hi.astype(F32)
    mid = rest.astype(BF16)
    lo = (rest - mid.astype(F32)).astype(BF16)
    return hi, mid, lo


def _times_01_matrix(x, mat_ref):
    mat = mat_ref[...]
    return sum(jnp.dot(piece, mat, preferred_element_type=F32) for piece in _bf16_pieces(x))


def _last_step_to_lanes(q, heads, batch):
    half = RWKV_HEAD_DIM // KEY_HALVES
    n = LANES // batch
    lane = lax.broadcasted_iota(jnp.int32, (half, LANES), 1)
    out = jnp.zeros((half, LANES), F32)
    for kh in range(KEY_HALVES):
        for h in range(heads):
            j = kh * heads + h
            slab = q[(h * KEY_HALVES + kh) * half:(h * KEY_HALVES + kh + 1) * half, :]
            moved = pltpu.roll(slab, ((j - (n - 1)) * batch) % LANES, axis=1)
            out = jnp.where(lane // batch == j, moved, out)
    return out


def _rwkv_prep_kernel(rkv_ref, lora_ref, vfirst_ref, decay_up_ref, aaa_up_ref, gate_up_ref,
                      vres_up_ref, vec_ref, seg_ref, upto_ref, whole_ref,
                      r_ref, k_ref, a_ref, b_ref, pt_ref, vl_ref, v_ref, g_ref, bonus_ref,
                      *, has_vres, batch):
    wdt = rkv_ref.shape[0] // 3
    heads = wdt // RWKV_HEAD_DIM
    steps = LANES // batch
    n_groups = rkv_ref.shape[1] // LANES
    decay_base, aaa_base, vres_base = vec_ref[0], vec_ref[1], vec_ref[2]
    k_k, k_a, r_k = vec_ref[3], vec_ref[4], vec_ref[5]
    log_p_before = 0.0
    for gi in range(n_groups):
        cols = slice(gi * LANES, (gi + 1) * LANES)
        r = rkv_ref[0:wdt, cols]
        k = rkv_ref[wdt:2 * wdt, cols]
        v = rkv_ref[2 * wdt:3 * wdt, cols]
        wa = lora_ref[0:LANES, cols]
        gd = lora_ref[LANES:2 * LANES, cols]

        z = -(decay_base + jnp.dot(decay_up_ref[...], jnp.tanh(wa).astype(BF16),
                                   preferred_element_type=F32))
        softplus = jnp.maximum(z, 0.0) + jnp.log(1.0 + jnp.exp(-jnp.abs(z)))
        log_w = -jnp.exp(-softplus - 0.5)
        log_p = log_p_before + _times_01_matrix(log_w, upto_ref)
        if gi + 1 < n_groups:
            log_p_before = log_p_before + _times_01_matrix(log_w, whole_ref)
        p_t = jnp.exp(log_p)
        inv_p_t = jnp.exp(-log_p)
        p_prev = jnp.exp(log_p - log_w)
        a = jax.nn.sigmoid(aaa_base + jnp.dot(aaa_up_ref[...], wa.astype(BF16),
                                              preferred_element_type=F32))
        g_ref[:, cols] = jnp.dot(gate_up_ref[...], jax.nn.sigmoid(gd).astype(BF16),
                                 preferred_element_type=F32)
        if has_vres:
            vd = lora_ref[2 * LANES:3 * LANES, cols]
            mix = jax.nn.sigmoid(vres_base + jnp.dot(vres_up_ref[...], vd.astype(BF16),
                                                     preferred_element_type=F32))
            v = v + (vfirst_ref[:, cols] - v) * mix
        kk = k * k_k
        norm = jnp.sqrt(_segment_sum_t(kk * kk, seg_ref))
        kk = kk / jnp.maximum(norm, 1e-12)
        k2 = k * (1.0 + (a - 1.0) * k_a)
        v_ref[:, cols] = v
        bonus_ref[:, cols] = _segment_sum_t(r * k2 * r_k, seg_ref) * v

        for ref, val in ((r_ref, r * p_t), (k_ref, k2 * inv_p_t), (a_ref, -kk * p_prev),
                         (b_ref, kk * a * inv_p_t)):
            for t, part in enumerate(_keys_to_lanes(val, heads, batch)):
                ref[gi * steps + t] = part
        for t, part in enumerate(_values_to_lanes(v, heads, batch)):
            vl_ref[gi * steps + t] = part
        if gi + 1 == n_groups:
            pt_ref[0] = _last_step_to_lanes(p_t, heads, batch)


def _rwkv_prep(rkv_t, lora_t, vfirst_t, decay_up_t, aaa_up_t, gate_up_t, vres_up_t, vecs, seg,
               has_vres, batch):
    m = rkv_t.shape[1]
    wdt = rkv_t.shape[0] // 3
    heads = wdt // RWKV_HEAD_DIM
    assert KEY_HALVES * heads * batch == LANES, "recurrence layout needs 2*heads*batch == 128 lanes"
    steps = RECURRENCE_CHUNK
    tm = steps * batch
    seq = m // batch
    assert tm % LANES == 0 and seq % steps == 0
    t_idx, b_idx = jnp.arange(LANES) // batch, jnp.arange(LANES) % batch
    same_batch = b_idx[:, None] == b_idx[None, :]
    upto = (same_batch & (t_idx[:, None] <= t_idx[None, :])).astype(BF16)
    whole = same_batch.astype(BF16)
    half = RWKV_HEAD_DIM // KEY_HALVES
    cols = lambda n: pl.BlockSpec((n, tm), lambda i: (0, i))
    full = lambda arr: pl.BlockSpec(arr.shape, lambda i: (0,) * arr.ndim)
    key_spec = pl.BlockSpec((steps, half, LANES), lambda i: (i, 0, 0))
    val_spec = pl.BlockSpec((steps, RWKV_HEAD_DIM, LANES), lambda i: (i, 0, 0))
    key_out = jax.ShapeDtypeStruct((seq, half, LANES), F32)
    val_out = jax.ShapeDtypeStruct((seq, RWKV_HEAD_DIM, LANES), F32)
    nat_out = jax.ShapeDtypeStruct((wdt, m), F32)
    return pl.pallas_call(
        functools.partial(_rwkv_prep_kernel, has_vres=has_vres, batch=batch),
        grid=(m // tm,),
        in_specs=[cols(3 * wdt), cols(LORA_PAD), cols(wdt), full(decay_up_t), full(aaa_up_t),
                  full(gate_up_t), full(vres_up_t), full(vecs), full(seg), full(upto), full(whole)],
        out_specs=[key_spec] * 4 + [pl.BlockSpec((1, half, LANES), lambda i: (i, 0, 0))]
                  + [val_spec] + [cols(wdt)] * 3,
        out_shape=[key_out] * 4 + [jax.ShapeDtypeStruct((seq // steps, half, LANES), F32)]
                  + [val_out] + [nat_out] * 3,
        compiler_params=_params("arbitrary"),
        name="rwkv_prep",
    )(rkv_t, lora_t, vfirst_t, decay_up_t, aaa_up_t, gate_up_t, vres_up_t, vecs, seg, upto, whole)


def _rwkv_scan_kernel(a_ref, a_next_ref, b_ref, k_ref, r_ref, pt_ref, v_ref, y_ref, state_ref, sa_ref):
    @pl.when(pl.program_id(0) == 0)
    def _():
        state_ref[...] = jnp.zeros_like(state_ref)
        sa_ref[...] = jnp.zeros_like(sa_ref)

    steps, nk, _ = a_ref.shape
    nv = v_ref.shape[1]

    def step(t, sa_half, next_a_row, chunk_end):
        sa = sa_half + pltpu.roll(sa_half, LANES // KEY_HALVES, axis=1)
        row = lambda ref, kx: jnp.broadcast_to(ref[t, pl.ds(kx, 1), :], (nv, LANES))
        v_t = v_ref[t]
        acc_y = None
        acc_sa = None
        for kx in range(nk):
            new = state_ref[kx] + v_t * row(k_ref, kx) + sa * row(b_ref, kx)
            term_y = new * row(r_ref, kx)
            if chunk_end:
                new = new * jnp.broadcast_to(pt_ref[0, pl.ds(kx, 1), :], (nv, LANES))
            state_ref[kx] = new
            term_sa = new * jnp.broadcast_to(next_a_row(kx), (nv, LANES))
            acc_y = term_y if acc_y is None else acc_y + term_y
            acc_sa = term_sa if acc_sa is None else acc_sa + term_sa
        y_ref[t] = acc_y
        return acc_sa

    def body(t, sa_half):
        return step(t, sa_half, lambda kx: a_ref[t + 1, pl.ds(kx, 1), :], False)

    sa_half = lax.fori_loop(0, steps - 1, body, sa_ref[...], unroll=2)
    sa_ref[...] = step(steps - 1, sa_half, lambda kx: a_next_ref[0, pl.ds(kx, 1), :], True)


def _rwkv_scan(a_t, b_t, k_t, r_t, pt, v_t):
    s, nk, _ = a_t.shape
    nv = v_t.shape[1]
    steps = RECURRENCE_CHUNK
    n_blocks = s // steps
    kin = pl.BlockSpec((steps, nk, LANES), lambda i: (i, 0, 0))
    knext = pl.BlockSpec((1, nk, LANES), lambda i: (jnp.minimum((i + 1) * steps, s - 1), 0, 0))
    pin = pl.BlockSpec((1, nk, LANES), lambda i: (i, 0, 0))
    vin = pl.BlockSpec((steps, nv, LANES), lambda i: (i, 0, 0))
    return pl.pallas_call(
        _rwkv_scan_kernel,
        grid=(n_blocks,),
        in_specs=[kin, knext] + [kin] * 3 + [pin, vin],
        out_specs=vin,
        out_shape=jax.ShapeDtypeStruct((s, nv, LANES), F32),
        scratch_shapes=[pltpu.VMEM((nk, nv, LANES), F32), pltpu.VMEM((nv, LANES), F32)],
        compiler_params=_params("arbitrary"),
        name="rwkv_scan",
    )(a_t, a_t, b_t, k_t, r_t, pt, v_t)


def _layer_norm(z, g, b):
    mu = jnp.mean(z, axis=-1, keepdims=True)
    zc = z - mu
    var = jnp.mean(zc * zc, axis=-1, keepdims=True)
    return zc * lax.rsqrt(var + LN_EPS) * g + b


def _outproj_kernel(att_ref, y_ref, bonus_ref, g_ref, x_ref, wa_ref, wr_ref, vec_ref, ln_ref,
                    seg_ref, out_ref, outb_ref, att_rows, *, alpha, batch):
    n_tiles = att_rows.shape[0]
    for bi in range(batch):
        for j in range(n_tiles):
            att_rows[j, pl.ds(bi, att_ref.shape[1], stride=batch), :] = att_ref[bi, :, j * LANES:(j + 1) * LANES]
    att = jnp.concatenate([att_rows[j] for j in range(n_tiles)], axis=1)

    heads = bonus_ref.shape[0] // RWKV_HEAD_DIM
    steps = LANES // batch
    inv_n = 1.0 / RWKV_HEAD_DIM
    n_groups = y_ref.shape[0] // steps
    wide = lambda col: jnp.concatenate([col] * n_groups, axis=1)
    y = jnp.concatenate(
        [_values_from_lanes([y_ref[gi * steps + t] for t in range(steps)], heads, batch)
         for gi in range(n_groups)], axis=1)
    mu = _segment_sum_t(y, seg_ref) * inv_n
    yc = y - mu
    var = _segment_sum_t(yc * yc, seg_ref) * inv_n
    yn = yc * lax.rsqrt(var + GN_EPS) * wide(vec_ref[0]) + wide(vec_ref[1])
    rw = ((yn + bonus_ref[...]) * g_ref[...]).T

    mix = (jnp.dot(att.astype(BF16), wa_ref[...], preferred_element_type=F32)
           + jnp.dot(rw.astype(BF16), wr_ref[...], preferred_element_type=F32))
    out = _layer_norm(alpha * x_ref[...] + mix, ln_ref[0:1, :], ln_ref[1:2, :])
    out_ref[...] = out
    outb_ref[...] = out.astype(BF16)


def _outproj(att_b, y_lanes, bonus, g, x2, wa, wr, gn_vec, ln_vec, seg, alpha, tm=512):
    m, d = x2.shape
    batch, _, a = att_b.shape
    tm = min(tm, m)
    assert tm % LANES == 0
    rows = lambda n: pl.BlockSpec((tm, n), lambda i: (i, 0))
    cols = lambda arr: pl.BlockSpec((arr.shape[0], tm), lambda i: (0, i))
    full = lambda arr: pl.BlockSpec(arr.shape, lambda i: (0,) * arr.ndim)
    att_spec = pl.BlockSpec((batch, tm // batch, a), lambda i: (0, i, 0))
    y_spec = pl.BlockSpec((tm // batch,) + y_lanes.shape[1:], lambda i: (i, 0, 0))
    return pl.pallas_call(
        functools.partial(_outproj_kernel, alpha=alpha, batch=batch),
        grid=(m // tm,),
        in_specs=[att_spec, y_spec, cols(bonus), cols(g)]
                 + [rows(d), full(wa), full(wr), full(gn_vec), full(ln_vec), full(seg)],
        out_specs=[rows(d), rows(d)],
        out_shape=[jax.ShapeDtypeStruct((m, d), F32), jax.ShapeDtypeStruct((m, d), BF16)],
        scratch_shapes=[pltpu.VMEM((a // LANES, tm, LANES), F32)],
        compiler_params=_params("arbitrary"),
        name="outproj_ln",
    )(att_b, y_lanes, bonus, g, x2, wa, wr, gn_vec, ln_vec, seg)


def _swiglu_kernel(expert_ref, x_ref, wg_ref, wu_ref, wd_ref, *rest, alpha):
    if alpha is None:
        out_ref, acc_ref = rest
    else:
        res_ref, ln_ref, out_ref, acc_ref = rest
    j = pl.program_id(1)

    @pl.when(j == 0)
    def _():
        acc_ref[...] = jnp.zeros_like(acc_ref)

    x = x_ref[...]
    gate = jnp.dot(x, wg_ref[0], preferred_element_type=F32)
    up = jnp.dot(x, wu_ref[0], preferred_element_type=F32)
    h = (gate * jax.nn.sigmoid(gate) * up).astype(BF16)
    acc_ref[...] += jnp.dot(h, wd_ref[0], preferred_element_type=F32)

    @pl.when(j == pl.num_programs(1) - 1)
    def _():
        if alpha is None:
            out_ref[...] = acc_ref[...]
        else:
            out_ref[...] = _layer_norm(alpha * res_ref[...] + acc_ref[...], ln_ref[0:1, :], ln_ref[1:2, :])


def _swiglu(tile_expert, xb, wg, wu, wd, tm, fc, residual=None, ln_vec=None, alpha=None):
    m, d = xb.shape
    f = wg.shape[2]
    rows = pl.BlockSpec((tm, d), lambda i, j, e: (i, 0))
    extra_specs, extra_args = [], []
    if residual is not None:
        extra_specs = [rows, pl.BlockSpec(ln_vec.shape, lambda i, j, e: (0, 0))]
        extra_args = [residual, ln_vec]
    return pl.pallas_call(
        functools.partial(_swiglu_kernel, alpha=alpha),
        grid_spec=pltpu.PrefetchScalarGridSpec(
            num_scalar_prefetch=1,
            grid=(m // tm, f // fc),
            in_specs=[rows,
                      pl.BlockSpec((1, d, fc), lambda i, j, e: (e[i], 0, j)),
                      pl.BlockSpec((1, d, fc), lambda i, j, e: (e[i], 0, j)),
                      pl.BlockSpec((1, fc, d), lambda i, j, e: (e[i], j, 0))] + extra_specs,
            out_specs=rows,
            scratch_shapes=[pltpu.VMEM((tm, d), F32)]),
        out_shape=jax.ShapeDtypeStruct((m, d), F32),
        compiler_params=_params("arbitrary", "arbitrary"),
        name="swiglu",
    )(tile_expert, xb, wg, wu, wd, *extra_args)


MAX_FFN_CHUNK = 2048


def _ffn_chunk(f):
    if f % LANES:
        raise ValueError(f"feed-forward width {f} is not a multiple of {LANES}")
    tiles = f // LANES
    return LANES * max(n for n in range(1, tiles + 1) if tiles % n == 0 and n * LANES <= MAX_FFN_CHUNK)


def _router_kernel(x_ref, wr_ref, out_ref):
    logits = jnp.dot(x_ref[...], wr_ref[...], preferred_element_type=F32,
                     precision=lax.Precision.HIGHEST)
    lane = lax.broadcasted_iota(jnp.int32, logits.shape, 1)
    logits = jnp.where(lane < N_EXPERTS, logits, NEG_BIG)
    v1 = jnp.max(logits, axis=-1, keepdims=True)
    i1 = jnp.min(jnp.where(logits == v1, lane, LANES), axis=-1, keepdims=True)
    rest = jnp.where(lane == i1, NEG_BIG, logits)
    v2 = jnp.max(rest, axis=-1, keepdims=True)
    i2 = jnp.min(jnp.where(rest == v2, lane, LANES), axis=-1, keepdims=True)
    e2 = jnp.exp(v2 - v1)
    g1 = 1.0 / (1.0 + e2)
    g2 = e2 / (1.0 + e2)
    out_ref[...] = jnp.where(lane == 0, i1.astype(F32),
                             jnp.where(lane == 1, i2.astype(F32),
                                       jnp.where(lane == 2, g1, jnp.where(lane == 3, g2, 0.0))))


def _router(x2, wr_pad, tm=512):
    m, d = x2.shape
    tm = min(tm, m)
    return pl.pallas_call(
        _router_kernel,
        grid=(m // tm,),
        in_specs=[pl.BlockSpec((tm, d), lambda i: (i, 0)), pl.BlockSpec(wr_pad.shape, lambda i: (0, 0))],
        out_specs=pl.BlockSpec((tm, LANES), lambda i: (i, 0)),
        out_shape=jax.ShapeDtypeStruct((m, LANES), F32),
        compiler_params=_params("arbitrary"),
        name="router_top2",
    )(x2, wr_pad)


def _moe_add_ln_kernel(x_ref, y1_ref, y2_ref, route_ref, ln_ref, out_ref, *, alpha):
    g1 = route_ref[:, 2:3]
    g2 = route_ref[:, 3:4]
    f = g1 * y1_ref[...] + g2 * y2_ref[...]
    out_ref[...] = _layer_norm(alpha * x_ref[...] + f, ln_ref[0:1, :], ln_ref[1:2, :])


def _moe_add_ln(x2, y1, y2, route, ln_vec, alpha, tm=512):
    m, d = x2.shape
    tm = min(tm, m)
    rows = pl.BlockSpec((tm, d), lambda i: (i, 0))
    return pl.pallas_call(
        functools.partial(_moe_add_ln_kernel, alpha=alpha),
        grid=(m // tm,),
        in_specs=[rows, rows, rows, pl.BlockSpec((tm, LANES), lambda i: (i, 0)),
                  pl.BlockSpec(ln_vec.shape, lambda i: (0, 0))],
        out_specs=rows,
        out_shape=jax.ShapeDtypeStruct((m, d), F32),
        compiler_params=_params("arbitrary"),
        name="moe_add_ln",
    )(x2, y1, y2, route, ln_vec)


def _moe(x2, xb, router_w, wg, wu, wd, first_expert, n_exp, ln_vec, alpha, tm):
    m, d = x2.shape
    wr_pad = jnp.pad(router_w, ((0, 0), (0, LANES - n_exp)))
    route = _router(x2, wr_pad)
    idx = route[:, 0:2].astype(jnp.int32).reshape(-1)
    experts = jnp.arange(n_exp, dtype=jnp.int32)
    counts = jnp.sum((idx[:, None] == experts[None, :]).astype(jnp.int32), axis=0)
    padded = ((counts + tm - 1) // tm) * tm
    pad_end = jnp.cumsum(padded)
    pad_start = pad_end - padded
    start = jnp.cumsum(counts) - counts
    order = jnp.argsort(idx, stable=True)
    rank = jnp.argsort(order)
    slot_pos = pad_start[idx] + (rank - start[idx])
    rows_total = 2 * m + n_exp * tm
    tile_start = jnp.arange(rows_total // tm, dtype=jnp.int32) * tm
    tile_expert = jnp.minimum(jnp.sum((tile_start[:, None] >= pad_end[None, :]).astype(jnp.int32), axis=1),
                              n_exp - 1)
    row = jnp.arange(rows_total, dtype=jnp.int32)
    row_expert = jnp.repeat(tile_expert, tm)
    sorted_pos = jnp.clip(row - pad_start[row_expert] + start[row_expert], 0, 2 * m - 1)
    src_token = order[sorted_pos] // 2
    xg = jnp.take(xb, src_token, axis=0, mode="clip")
    yg = _swiglu(tile_expert + first_expert, xg, wg, wu, wd, tm, _ffn_chunk(wg.shape[2]))
    pos = slot_pos.reshape(m, 2)
    y1 = jnp.take(yg, pos[:, 0], axis=0, mode="clip")
    y2 = jnp.take(yg, pos[:, 1], axis=0, mode="clip")
    return _moe_add_ln(x2, y1, y2, route, ln_vec, alpha)


def kernel(x, w_in, w_in_vres, shift_mu, shift_mu_vres, decay_up, decay_base, aaa_up, aaa_base, vres_up, vres_base, gate_up, k_k, k_a, r_k, lnx_g, lnx_b, w_out, ln1_g, ln1_b, ln2_g, ln2_b, ffn_w_gate, ffn_w_up, ffn_w_down, router, moe_w_gate, moe_w_up, moe_w_down):
    b, s, d = x.shape
    depth = w_in.shape[0]
    wdt = decay_up.shape[2]
    att = w_out.shape[1] - wdt
    alpha = float((2 * depth) ** 0.25)
    m = b * s
    ffn_tm = min(512, m)

    seg_head = jnp.arange(SEG_ROWS) // RWKV_HEAD_DIM
    seg = (seg_head[:, None] == seg_head[None, :]).astype(BF16)
    q_scale = jnp.concatenate([jnp.full((att,), ATT_HEAD_DIM ** -0.5, F32), jnp.ones((2 * att,), F32)])
    zero_tiles = jnp.zeros((m // ffn_tm,), jnp.int32)
    ffn_w = [w.astype(BF16) for w in (ffn_w_gate, ffn_w_up, ffn_w_down)]
    n_exp = moe_w_gate.shape[1]
    moe_w = [w.astype(BF16).reshape((-1,) + w.shape[2:]) for w in (moe_w_gate, moe_w_up, moe_w_down)]

    x2 = x.transpose(1, 0, 2).reshape(m, d)
    v_first = None
    for l in range(depth):
        has_vres = l > 0
        w_l = w_in[l]
        c0 = 3 * att
        wqkv = (w_l[:, :c0] * q_scale).astype(BF16)
        wrkv_t = w_l[:, c0:c0 + 3 * wdt].T.astype(BF16)
        lora_w = w_l[:, c0 + 3 * wdt:]
        lora_mu = shift_mu[l, 3 * wdt:]
        if has_vres:
            lora_w = jnp.concatenate([lora_w, w_in_vres[l - 1]], axis=1)
            lora_mu = jnp.concatenate([lora_mu, shift_mu_vres[l - 1]])
        n_lora = lora_w.shape[1]
        wlora_t = jnp.pad(lora_w, ((0, 0), (0, LORA_PAD - n_lora))).T.astype(BF16)
        columns = lambda vec: jnp.broadcast_to(vec[..., None], vec.shape + (LANES,))
        mu_lora = columns(jnp.pad(lora_mu, (0, LORA_PAD - n_lora)))
        mu_rkv = columns(shift_mu[l, :3 * wdt])

        qkv, rkv_t, lora_t = _inproj(x2, wqkv, wrkv_t, wlora_t, mu_rkv, mu_lora, b)

        att_b = _attention(qkv)

        zpad_t = lambda w, rows_before: jnp.pad(
            w, ((rows_before, LANES - rows_before - w.shape[0]), (0, 0))).T.astype(BF16)
        decay_up_p = zpad_t(decay_up[l], 0)
        aaa_up_p = zpad_t(aaa_up[l], DECAY_LORA)
        gate_up_p = gate_up[l].T.astype(BF16)
        if has_vres:
            vres_up_p = zpad_t(vres_up[l - 1], 0)
            vres_b = vres_base[l - 1]
            vf_in = v_first
        else:
            vres_up_p = jnp.zeros((wdt, LANES), BF16)
            vres_b = jnp.zeros((wdt,), F32)
            vf_in = rkv_t
        vecs = columns(jnp.stack([decay_base[l], aaa_base[l], vres_b, k_k[l], k_a[l], r_k[l]]))
        r_, k_, a_, b_, pt_, vl_, v_, g_, bonus = _rwkv_prep(
            rkv_t, lora_t, vf_in, decay_up_p, aaa_up_p, gate_up_p, vres_up_p, vecs, seg, has_vres, b)
        if not has_vres:
            v_first = v_
        y_lanes = _rwkv_scan(a_, b_, k_, r_, pt_, vl_)

        gn_vec = columns(jnp.stack([lnx_g[l], lnx_b[l]]))
        ln1 = jnp.stack([ln1_g[l], ln1_b[l]] + [jnp.zeros((d,), F32)] * 6)
        ln2 = jnp.stack([ln2_g[l], ln2_b[l]] + [jnp.zeros((d,), F32)] * 6)
        wo = w_out[l].astype(BF16)
        x2, xb = _outproj(att_b, y_lanes, bonus, g_, x2, wo[:att], wo[att:], gn_vec, ln1, seg, alpha)

        i = l // 2
        if l % 2 == 0:
            x2 = _swiglu(zero_tiles + i, xb, *ffn_w, ffn_tm, _ffn_chunk(ffn_w[0].shape[2]),
                         residual=x2, ln_vec=ln2, alpha=alpha)
        else:
            x2 = _moe(x2, xb, router[i], *moe_w, i * n_exp, n_exp, ln2, alpha, ffn_tm)
    return x2.reshape(s, b, d).transpose(1, 0, 2)
```

```python
import functools

import jax
import jax.numpy as jnp
from jax import lax
from jax.experimental import pallas as pl
from jax.experimental.pallas import tpu as pltpu

ATT_HEAD_DIM = 64
RWKV_HEAD_DIM = 64
DILATIONS = (1, 4, 16)
ATT_BLOCK = 128
DECAY_LORA = 64
N_EXPERTS = 8
LN_EPS = 1e-5
GN_EPS = 64e-5

LANES = 128
LORA_PAD = 384
VMEM_LIMIT = 56 * 1024 * 1024
NEG_BIG = -1e30

F32 = jnp.float32
BF16 = jnp.bfloat16


def _params(*sem):
    return pltpu.CompilerParams(dimension_semantics=sem, vmem_limit_bytes=VMEM_LIMIT)


def _inproj_kernel(x_ref, wqkv_ref, wrkv_ref, wlora_ref, mu_rkv_ref, mu_lora_ref,
                   qkv_ref, rkv_ref, lora_ref, carry_rkv, carry_lora, qkv_rows, *, batch):
    xb = x_ref[...].astype(BF16)
    qkv = jnp.dot(xb, wqkv_ref[...], preferred_element_type=F32)
    n_tiles, rows, _ = qkv_rows.shape
    for j in range(n_tiles):
        qkv_rows[j] = qkv[:, j * LANES:(j + 1) * LANES]
    for bi in range(batch):
        for j in range(n_tiles):
            qkv_ref[bi, :, j * LANES:(j + 1) * LANES] = qkv_rows[j, pl.ds(bi, rows // batch, stride=batch), :]

    @pl.when(pl.program_id(0) == 0)
    def _():
        carry_rkv[...] = jnp.zeros_like(carry_rkv)
        carry_lora[...] = jnp.zeros_like(carry_lora)

    nt = (((1,), (1,)), ((), ()))
    lane = lax.broadcasted_iota(jnp.int32, (1, LANES), 1)

    def shifted(wt_ref, mu_ref, carry_ref, out_ref):
        p = lax.dot_general(wt_ref[...], xb, nt, preferred_element_type=F32)
        tm = p.shape[1]
        mu = mu_ref[...]
        prev = pltpu.roll(p, batch, axis=1)
        from_carry = pltpu.roll(carry_ref[...], batch, axis=1)
        for j in range(tm // LANES):
            cols = slice(j * LANES, (j + 1) * LANES)
            prev_j = prev[:, cols]
            if j == 0:
                prev_j = jnp.where(lane < batch, from_carry, prev_j)
            out_ref[:, cols] = p[:, cols] + (prev_j - p[:, cols]) * mu
        carry_ref[...] = p[:, tm - LANES:tm]

    shifted(wrkv_ref, mu_rkv_ref, carry_rkv, rkv_ref)
    shifted(wlora_ref, mu_lora_ref, carry_lora, lora_ref)


def _inproj(x2, wqkv, wrkv_t, wlora_t, mu_rkv, mu_lora, batch, tm=512):
    m, d = x2.shape
    nq, nr, nl = wqkv.shape[1], wrkv_t.shape[0], wlora_t.shape[0]
    tm = min(tm, m)
    full = lambda shape: pl.BlockSpec(shape, lambda i: (0, 0))
    return pl.pallas_call(
        functools.partial(_inproj_kernel, batch=batch),
        grid=(m // tm,),
        in_specs=[pl.BlockSpec((tm, d), lambda i: (i, 0)),
                  full((d, nq)), full((nr, d)), full((nl, d)), full((nr, LANES)), full((nl, LANES))],
        out_specs=[pl.BlockSpec((batch, tm // batch, nq), lambda i: (0, i, 0)),
                   pl.BlockSpec((nr, tm), lambda i: (0, i)),
                   pl.BlockSpec((nl, tm), lambda i: (0, i))],
        out_shape=[jax.ShapeDtypeStruct((batch, m // batch, nq), F32),
                   jax.ShapeDtypeStruct((nr, m), F32),
                   jax.ShapeDtypeStruct((nl, m), F32)],
        scratch_shapes=[pltpu.VMEM((nr, LANES), F32), pltpu.VMEM((nl, LANES), F32),
                        pltpu.VMEM((nq // LANES, tm, LANES), F32)],
        compiler_params=_params("arbitrary"),
        name="inproj",
    )(x2, wqkv, wrkv_t, wlora_t, mu_rkv, mu_lora)


CLASS_COPY_ROWS = 256
ATT_WINDOW = 2 * ATT_BLOCK
ATT_HEADS_PER_TILE = LANES // ATT_HEAD_DIM


def _attn_kernel(q_ref, k_ref, v_ref, out_ref, acc_ref, m_ref, l_ref, qc_ref, kc_ref, vc_ref,
                 q4_ref, k4_ref, v4_ref, tmp_ref,
                 s0_scr, s1_scr, p0_scr, p1_scr, ml0_scr, ml1_scr, bias_scr):
    seq = q_ref.shape[0]
    n_blocks = seq // ATT_BLOCK
    base = DILATIONS[1]
    sub = DILATIONS[2] // base
    qlen = seq // base
    lane = lax.broadcasted_iota(jnp.int32, (ATT_BLOCK, LANES), 1)
    in_head = [(lane >= h * ATT_HEAD_DIM) & (lane < (h + 1) * ATT_HEAD_DIM)
               for h in range(ATT_HEADS_PER_TILE)]
    rel = (lax.broadcasted_iota(jnp.int32, (ATT_BLOCK, ATT_WINDOW), 0)
           - lax.broadcasted_iota(jnp.int32, (ATT_BLOCK, ATT_WINDOW), 1))
    nt = (((1,), (1,)), ((), ()))
    sources = ((q_ref, q4_ref, qc_ref), (k_ref, k4_ref, kc_ref), (v_ref, v4_ref, vc_ref))

    s_scr, p_scr, ml_scr = (s0_scr, s1_scr), (p0_scr, p1_scr), (ml0_scr, ml1_scr)
    for ref in s_scr + p_scr + ml_scr:
        ref[...] = jnp.zeros_like(ref)
    for first_window in range(2):
        dist = rel + first_window * ATT_BLOCK
        bias_scr[first_window] = jnp.where((dist >= 0) & (dist <= ATT_BLOCK), 0.0, NEG_BIG)

    def split_by_base(c, carry):
        for j in range(qlen // CLASS_COPY_ROWS):
            src = pl.ds(c + j * CLASS_COPY_ROWS * base, CLASS_COPY_ROWS, stride=base)
            dst = pl.ds(pl.multiple_of(c * qlen, CLASS_COPY_ROWS) + j * CLASS_COPY_ROWS, CLASS_COPY_ROWS)
            for full_ref, split_ref, _ in sources:
                split_ref[dst, :] = full_ref[src, :]
        return carry

    lax.fori_loop(0, base, split_by_base, 0)

    def run_pattern(per_class, commit):
        def locate(g):
            g = jnp.clip(g, 0, n_blocks - 1)
            first = (g % per_class) == 0
            q_start = pl.multiple_of(g * ATT_BLOCK, ATT_BLOCK)
            k_start = pl.multiple_of(jnp.where(first, g, g - 1) * ATT_BLOCK, ATT_BLOCK)
            return g, q_start, k_start

        def scores(g, slot):
            _, q_start, k_start = locate(g)
            q = qc_ref[pl.ds(q_start, ATT_BLOCK), :]
            kw = kc_ref[pl.ds(k_start, ATT_WINDOW), :]
            bias = bias_scr[(q_start - k_start) // ATT_BLOCK]
            for h in range(ATT_HEADS_PER_TILE):
                qh = jnp.where(in_head[h], q, jnp.zeros_like(q))
                s_scr[slot][h] = lax.dot_general(qh, kw, nt, preferred_element_type=F32) + bias

        def softmax(slot):
            m_acc = l_acc = jnp.zeros((ATT_BLOCK, LANES), F32)
            for h in range(ATT_HEADS_PER_TILE):
                sc = s_scr[slot][h]
                m = jnp.max(sc, axis=-1, keepdims=True)
                p = jnp.exp(sc - m)
                p_scr[slot][h] = p.astype(BF16)
                m_acc = jnp.where(in_head[h], m, m_acc)
                l_acc = jnp.where(in_head[h], jnp.sum(p, axis=-1, keepdims=True), l_acc)
            ml_scr[slot][0] = m_acc
            ml_scr[slot][1] = l_acc

        def accumulate(g, slot):
            valid = g >= 0
            g, _, k_start = locate(g)
            vw = vc_ref[pl.ds(k_start, ATT_WINDOW), :]
            o = jnp.zeros((ATT_BLOCK, LANES), F32)
            for h in range(ATT_HEADS_PER_TILE):
                o = jnp.where(in_head[h], jnp.dot(p_scr[slot][h], vw, preferred_element_type=F32), o)
            commit(g, valid, o, ml_scr[slot][0], ml_scr[slot][1])

        def pipeline_pair(it, carry):
            for parity in range(2):
                i = 2 * it + parity
                accumulate(i - 4, parity)
                softmax(parity)
                scores(i, parity)
            return carry

        lax.fori_loop(0, (n_blocks + 4) // 2, pipeline_pair, 0, unroll=3)

    def merge(rows, valid, o, m_blk, l_blk):
        m_blk = jnp.where(valid, m_blk, NEG_BIG)
        m_old = m_ref[rows, :]
        m_new = jnp.maximum(m_old, m_blk)
        w_old = jnp.exp(m_old - m_new)
        w_new = jnp.where(valid, jnp.exp(m_blk - m_new), 0.0)
        acc_ref[rows, :] = acc_ref[rows, :] * w_old + o * w_new
        l_ref[rows, :] = l_ref[rows, :] * w_old + l_blk * w_new
        m_ref[rows, :] = m_new

    for j in range(seq // CLASS_COPY_ROWS):
        rows = pl.ds(j * CLASS_COPY_ROWS, CLASS_COPY_ROWS)
        for full_ref, _, class_ref in sources:
            class_ref[rows, :] = full_ref[rows, :].astype(BF16)

    def commit_dense(g, valid, o, m_blk, l_blk):
        del valid
        per_base = ATT_BLOCK // base
        for idx, val in enumerate((o, m_blk, l_blk)):
            tmp_ref[idx] = val
        for c in range(base):
            dst = pl.ds(pl.multiple_of(c * qlen + g * per_base, per_base), per_base)
            for idx, state_ref in enumerate((acc_ref, m_ref, l_ref)):
                state_ref[dst, :] = tmp_ref[idx, pl.ds(c, per_base, stride=base), :]

    run_pattern(n_blocks, commit_dense)

    for j in range(seq // CLASS_COPY_ROWS):
        rows = pl.ds(j * CLASS_COPY_ROWS, CLASS_COPY_ROWS)
        for _, split_ref, class_ref in sources:
            class_ref[rows, :] = split_ref[rows, :].astype(BF16)

    def commit_base(g, valid, o, m_blk, l_blk):
        merge(pl.ds(pl.multiple_of(g * ATT_BLOCK, ATT_BLOCK), ATT_BLOCK), valid, o, m_blk, l_blk)

    run_pattern(qlen // ATT_BLOCK, commit_base)

    sub_len = qlen // sub
    per_sub = sub_len // ATT_BLOCK

    def split_again(cls, carry):
        src = pl.ds((cls // sub) * qlen + cls % sub, sub_len, stride=sub)
        dst = pl.ds(pl.multiple_of(cls * sub_len, CLASS_COPY_ROWS), sub_len)
        for _, split_ref, class_ref in sources:
            class_ref[dst, :] = split_ref[src, :].astype(BF16)
        return carry

    lax.fori_loop(0, base * sub, split_again, 0)

    def commit_sub(g, valid, o, m_blk, l_blk):
        cls = g // per_sub
        start = (cls // sub) * qlen + cls % sub + (g % per_sub) * (ATT_BLOCK * sub)
        merge(pl.ds(start, ATT_BLOCK, stride=sub), valid, o, m_blk, l_blk)

    run_pattern(per_sub, commit_sub)

    for c in range(base):
        for j in range(qlen // CLASS_COPY_ROWS):
            rows = pl.ds(c * qlen + j * CLASS_COPY_ROWS, CLASS_COPY_ROWS)
            out_ref[pl.ds(c + j * CLASS_COPY_ROWS * base, CLASS_COPY_ROWS, stride=base), :] = (
                acc_ref[rows, :] / l_ref[rows, :])


def _attention(qkv_b):
    b, s, c3 = qkv_b.shape
    a = c3 // 3
    npair = a // LANES
    assert s % (DILATIONS[-1] * ATT_WINDOW) == 0 and (s // DILATIONS[-1]) % CLASS_COPY_ROWS == 0
    blk = lambda off: pl.BlockSpec((None, s, LANES), lambda bi, hp: (bi, 0, off + hp))
    pipe = (ATT_HEADS_PER_TILE, ATT_BLOCK, ATT_WINDOW)
    return pl.pallas_call(
        _attn_kernel,
        grid=(b, npair),
        in_specs=[blk(0), blk(npair), blk(2 * npair)],
        out_specs=blk(0),
        out_shape=jax.ShapeDtypeStruct((b, s, a), F32),
        scratch_shapes=[pltpu.VMEM((s, LANES), F32)] * 3 + [pltpu.VMEM((s, LANES), BF16)] * 3
                       + [pltpu.VMEM((s, LANES), F32)] * 3 + [pltpu.VMEM((3, ATT_BLOCK, LANES), F32)]
                       + [pltpu.VMEM(pipe, F32)] * 2 + [pltpu.VMEM(pipe, BF16)] * 2
                       + [pltpu.VMEM((2, ATT_BLOCK, LANES), F32)] * 2
                       + [pltpu.VMEM((2, ATT_BLOCK, ATT_WINDOW), F32)],
        compiler_params=_params("arbitrary", "arbitrary"),
        name="dilated_attn",
    )(qkv_b, qkv_b, qkv_b)


KEY_HALVES = 2
SEG_ROWS = 256
RECURRENCE_CHUNK = 32


def _chunk_transpose(parts, chunk):
    n = len(parts)
    lane = lax.broadcasted_iota(jnp.int32, parts[0].shape, 1)
    s = n // 2
    while s >= 1:
        swap = ((lane // chunk) & s) != 0
        new = list(parts)
        for j in range(n):
            if j & s == 0:
                lo, hi = parts[j], parts[j + s]
                new[j] = jnp.where(swap, pltpu.roll(hi, s * chunk, axis=1), lo)
                new[j + s] = jnp.where(swap, hi, pltpu.roll(lo, LANES - s * chunk, axis=1))
        parts = new
        s //= 2
    return parts


def _keys_to_lanes(q, heads, batch):
    half = RWKV_HEAD_DIM // KEY_HALVES
    parts = [q[(h * KEY_HALVES + kh) * half:(h * KEY_HALVES + kh + 1) * half, :]
             for kh in range(KEY_HALVES) for h in range(heads)]
    return _chunk_transpose(parts, batch)


def _values_to_lanes(q, heads, batch):
    parts = [q[h * RWKV_HEAD_DIM:(h + 1) * RWKV_HEAD_DIM, :]
             for _ in range(KEY_HALVES) for h in range(heads)]
    return _chunk_transpose(parts, batch)


def _values_from_lanes(parts, heads, batch):
    per_half = _chunk_transpose(parts, batch)
    return jnp.concatenate([per_half[h] + per_half[heads + h] for h in range(heads)], axis=0)


def _segment_sum_t(x, seg_ref):
    seg = seg_ref[...]
    n = seg.shape[0]
    pieces = _bf16_pieces(x)
    slabs = []
    for s0 in range(0, x.shape[0], n):
        rows = slice(s0, s0 + n)
        slabs.append(sum(jnp.dot(seg, piece[rows], preferred_element_type=F32) for piece in pieces))
    return jnp.concatenate(slabs, axis=0)


def _bf16_pieces(x):
    hi = x.astype(BF16)
    rest = x - hi.astype(F32)
    mid = rest.astype(BF16)
    lo = (rest - mid.astype(F32)).astype(BF16)
    return hi, mid, lo


def _times_01_matrix(x, mat_ref):
    mat = mat_ref[...]
    return sum(jnp.dot(piece, mat, preferred_element_type=F32) for piece in _bf16_pieces(x))


def _last_step_to_lanes(q, heads, batch):
    half = RWKV_HEAD_DIM // KEY_HALVES
    n = LANES // batch
    lane = lax.broadcasted_iota(jnp.int32, (half, LANES), 1)
    out = jnp.zeros((half, LANES), F32)
    for kh in range(KEY_HALVES):
        for h in range(heads):
            j = kh * heads + h
            slab = q[(h * KEY_HALVES + kh) * half:(h * KEY_HALVES + kh + 1) * half, :]
            moved = pltpu.roll(slab, ((j - (n - 1)) * batch) % LANES, axis=1)
            out = jnp.where(lane // batch == j, moved, out)
    return out


def _rwkv_prep_kernel(rkv_ref, lora_ref, vfirst_ref, decay_up_ref, aaa_up_ref, gate_up_ref,
                      vres_up_ref, vec_ref, seg_ref, upto_ref, whole_ref,
                      r_ref, k_ref, a_ref, b_ref, pt_ref, vl_ref, v_ref, g_ref, bonus_ref,
                      *, has_vres, batch):
    wdt = rkv_ref.shape[0] // 3
    heads = wdt // RWKV_HEAD_DIM
    steps = LANES // batch
    n_groups = rkv_ref.shape[1] // LANES
    decay_base, aaa_base, vres_base = vec_ref[0], vec_ref[1], vec_ref[2]
    k_k, k_a, r_k = vec_ref[3], vec_ref[4], vec_ref[5]
    log_p_before = 0.0
    for gi in range(n_groups):
        cols = slice(gi * LANES, (gi + 1) * LANES)
        r = rkv_ref[0:wdt, cols]
        k = rkv_ref[wdt:2 * wdt, cols]
        v = rkv_ref[2 * wdt:3 * wdt, cols]
        wa = lora_ref[0:LANES, cols]
        gd = lora_ref[LANES:2 * LANES, cols]

        z = -(decay_base + jnp.dot(decay_up_ref[...], jnp.tanh(wa).astype(BF16),
                                   preferred_element_type=F32))
        softplus = jnp.maximum(z, 0.0) + jnp.log(1.0 + jnp.exp(-jnp.abs(z)))
        log_w = -jnp.exp(-softplus - 0.5)
        log_p = log_p_before + _times_01_matrix(log_w, upto_ref)
        if gi + 1 < n_groups:
            log_p_before = log_p_before + _times_01_matrix(log_w, whole_ref)
        p_t = jnp.exp(log_p)
        inv_p_t = jnp.exp(-log_p)
        p_prev = jnp.exp(log_p - log_w)
        a = jax.nn.sigmoid(aaa_base + jnp.dot(aaa_up_ref[...], wa.astype(BF16),
                                              preferred_element_type=F32))
        g_ref[:, cols] = jnp.dot(gate_up_ref[...], jax.nn.sigmoid(gd).astype(BF16),
                                 preferred_element_type=F32)
        if has_vres:
            vd = lora_ref[2 * LANES:3 * LANES, cols]
            mix = jax.nn.sigmoid(vres_base + jnp.dot(vres_up_ref[...], vd.astype(BF16),
                                                     preferred_element_type=F32))
            v = v + (vfirst_ref[:, cols] - v) * mix
        kk = k * k_k
        norm = jnp.sqrt(_segment_sum_t(kk * kk, seg_ref))
        kk = kk / jnp.maximum(norm, 1e-12)
        k2 = k * (1.0 + (a - 1.0) * k_a)
        v_ref[:, cols] = v
        bonus_ref[:, cols] = _segment_sum_t(r * k2 * r_k, seg_ref) * v

        for ref, val in ((r_ref, r * p_t), (k_ref, k2 * inv_p_t), (a_ref, -kk * p_prev),
                         (b_ref, kk * a * inv_p_t)):
            for t, part in enumerate(_keys_to_lanes(val, heads, batch)):
                ref[gi * steps + t] = part
        for t, part in enumerate(_values_to_lanes(v, heads, batch)):
            vl_ref[gi * steps + t] = part
        if gi + 1 == n_groups:
            pt_ref[0] = _last_step_to_lanes(p_t, heads, batch)


def _rwkv_prep(rkv_t, lora_t, vfirst_t, decay_up_t, aaa_up_t, gate_up_t, vres_up_t, vecs, seg,
               has_vres, batch):
    m = rkv_t.shape[1]
    wdt = rkv_t.shape[0] // 3
    heads = wdt // RWKV_HEAD_DIM
    assert KEY_HALVES * heads * batch == LANES, "recurrence layout needs 2*heads*batch == 128 lanes"
    steps = RECURRENCE_CHUNK
    tm = steps * batch
    seq = m // batch
    assert tm % LANES == 0 and seq % steps == 0
    t_idx, b_idx = jnp.arange(LANES) // batch, jnp.arange(LANES) % batch
    same_batch = b_idx[:, None] == b_idx[None, :]
    upto = (same_batch & (t_idx[:, None] <= t_idx[None, :])).astype(BF16)
    whole = same_batch.astype(BF16)
    half = RWKV_HEAD_DIM // KEY_HALVES
    cols = lambda n: pl.BlockSpec((n, tm), lambda i: (0, i))
    full = lambda arr: pl.BlockSpec(arr.shape, lambda i: (0,) * arr.ndim)
    key_spec = pl.BlockSpec((steps, half, LANES), lambda i: (i, 0, 0))
    val_spec = pl.BlockSpec((steps, RWKV_HEAD_DIM, LANES), lambda i: (i, 0, 0))
    key_out = jax.ShapeDtypeStruct((seq, half, LANES), F32)
    val_out = jax.ShapeDtypeStruct((seq, RWKV_HEAD_DIM, LANES), F32)
    nat_out = jax.ShapeDtypeStruct((wdt, m), F32)
    return pl.pallas_call(
        functools.partial(_rwkv_prep_kernel, has_vres=has_vres, batch=batch),
        grid=(m // tm,),
        in_specs=[cols(3 * wdt), cols(LORA_PAD), cols(wdt), full(decay_up_t), full(aaa_up_t),
                  full(gate_up_t), full(vres_up_t), full(vecs), full(seg), full(upto), full(whole)],
        out_specs=[key_spec] * 4 + [pl.BlockSpec((1, half, LANES), lambda i: (i, 0, 0))]
                  + [val_spec] + [cols(wdt)] * 3,
        out_shape=[key_out] * 4 + [jax.ShapeDtypeStruct((seq // steps, half, LANES), F32)]
                  + [val_out] + [nat_out] * 3,
        compiler_params=_params("arbitrary"),
        name="rwkv_prep",
    )(rkv_t, lora_t, vfirst_t, decay_up_t, aaa_up_t, gate_up_t, vres_up_t, vecs, seg, upto, whole)


def _rwkv_scan_kernel(a_ref, a_next_ref, b_ref, k_ref, r_ref, pt_ref, v_ref, y_ref, state_ref, sa_ref):
    @pl.when(pl.program_id(0) == 0)
    def _():
        state_ref[...] = jnp.zeros_like(state_ref)
        sa_ref[...] = jnp.zeros_like(sa_ref)

    steps, nk, _ = a_ref.shape
    nv = v_ref.shape[1]

    def step(t, sa_half, next_a_row, chunk_end):
        sa = sa_half + pltpu.roll(sa_half, LANES // KEY_HALVES, axis=1)
        row = lambda ref, kx: jnp.broadcast_to(ref[t, pl.ds(kx, 1), :], (nv, LANES))
        v_t = v_ref[t]
        acc_y = None
        acc_sa = None
        for kx in range(nk):
            new = state_ref[kx] + v_t * row(k_ref, kx) + sa * row(b_ref, kx)
            term_y = new * row(r_ref, kx)
            if chunk_end:
                new = new * jnp.broadcast_to(pt_ref[0, pl.ds(kx, 1), :], (nv, LANES))
            state_ref[kx] = new
            term_sa = new * jnp.broadcast_to(next_a_row(kx), (nv, LANES))
            acc_y = term_y if acc_y is None else acc_y + term_y
            acc_sa = term_sa if acc_sa is None else acc_sa + term_sa
        y_ref[t] = acc_y
        return acc_sa

    def body(t, sa_half):
        return step(t, sa_half, lambda kx: a_ref[t + 1, pl.ds(kx, 1), :], False)

    sa_half = lax.fori_loop(0, steps - 1, body, sa_ref[...], unroll=2)
    sa_ref[...] = step(steps - 1, sa_half, lambda kx: a_next_ref[0, pl.ds(kx, 1), :], True)


def _rwkv_scan(a_t, b_t, k_t, r_t, pt, v_t):
    s, nk, _ = a_t.shape
    nv = v_t.shape[1]
    steps = RECURRENCE_CHUNK
    n_blocks = s // steps
    kin = pl.BlockSpec((steps, nk, LANES), lambda i: (i, 0, 0))
    knext = pl.BlockSpec((1, nk, LANES), lambda i: (jnp.minimum((i + 1) * steps, s - 1), 0, 0))
    pin = pl.BlockSpec((1, nk, LANES), lambda i: (i, 0, 0))
    vin = pl.BlockSpec((steps, nv, LANES), lambda i: (i, 0, 0))
    return pl.pallas_call(
        _rwkv_scan_kernel,
        grid=(n_blocks,),
        in_specs=[kin, knext] + [kin] * 3 + [pin, vin],
        out_specs=vin,
        out_shape=jax.ShapeDtypeStruct((s, nv, LANES), F32),
        scratch_shapes=[pltpu.VMEM((nk, nv, LANES), F32), pltpu.VMEM((nv, LANES), F32)],
        compiler_params=_params("arbitrary"),
        name="rwkv_scan",
    )(a_t, a_t, b_t, k_t, r_t, pt, v_t)


def _layer_norm(z, g, b):
    mu = jnp.mean(z, axis=-1, keepdims=True)
    zc = z - mu
    var = jnp.mean(zc * zc, axis=-1, keepdims=True)
    return zc * lax.rsqrt(var + LN_EPS) * g + b


def _outproj_kernel(att_ref, y_ref, bonus_ref, g_ref, x_ref, wa_ref, wr_ref, vec_ref, ln_ref,
                    seg_ref, out_ref, outb_ref, att_rows, *, alpha, batch):
    n_tiles = att_rows.shape[0]
    for bi in range(batch):
        for j in range(n_tiles):
            att_rows[j, pl.ds(bi, att_ref.shape[1], stride=batch), :] = att_ref[bi, :, j * LANES:(j + 1) * LANES]
    att = jnp.concatenate([att_rows[j] for j in range(n_tiles)], axis=1)

    heads = bonus_ref.shape[0] // RWKV_HEAD_DIM
    steps = LANES // batch
    inv_n = 1.0 / RWKV_HEAD_DIM
    n_groups = y_ref.shape[0] // steps
    wide = lambda col: jnp.concatenate([col] * n_groups, axis=1)
    y = jnp.concatenate(
        [_values_from_lanes([y_ref[gi * steps + t] for t in range(steps)], heads, batch)
         for gi in range(n_groups)], axis=1)
    mu = _segment_sum_t(y, seg_ref) * inv_n
    yc = y - mu
    var = _segment_sum_t(yc * yc, seg_ref) * inv_n
    yn = yc * lax.rsqrt(var + GN_EPS) * wide(vec_ref[0]) + wide(vec_ref[1])
    rw = ((yn + bonus_ref[...]) * g_ref[...]).T

    mix = (jnp.dot(att.astype(BF16), wa_ref[...], preferred_element_type=F32)
           + jnp.dot(rw.astype(BF16), wr_ref[...], preferred_element_type=F32))
    out = _layer_norm(alpha * x_ref[...] + mix, ln_ref[0:1, :], ln_ref[1:2, :])
    out_ref[...] = out
    outb_ref[...] = out.astype(BF16)


def _outproj(att_b, y_lanes, bonus, g, x2, wa, wr, gn_vec, ln_vec, seg, alpha, tm=512):
    m, d = x2.shape
    batch, _, a = att_b.shape
    tm = min(tm, m)
    assert tm % LANES == 0
    rows = lambda n: pl.BlockSpec((tm, n), lambda i: (i, 0))
    cols = lambda arr: pl.BlockSpec((arr.shape[0], tm), lambda i: (0, i))
    full = lambda arr: pl.BlockSpec(arr.shape, lambda i: (0,) * arr.ndim)
    att_spec = pl.BlockSpec((batch, tm // batch, a), lambda i: (0, i, 0))
    y_spec = pl.BlockSpec((tm // batch,) + y_lanes.shape[1:], lambda i: (i, 0, 0))
    return pl.pallas_call(
        functools.partial(_outproj_kernel, alpha=alpha, batch=batch),
        grid=(m // tm,),
        in_specs=[att_spec, y_spec, cols(bonus), cols(g)]
                 + [rows(d), full(wa), full(wr), full(gn_vec), full(ln_vec), full(seg)],
        out_specs=[rows(d), rows(d)],
        out_shape=[jax.ShapeDtypeStruct((m, d), F32), jax.ShapeDtypeStruct((m, d), BF16)],
        scratch_shapes=[pltpu.VMEM((a // LANES, tm, LANES), F32)],
        compiler_params=_params("arbitrary"),
        name="outproj_ln",
    )(att_b, y_lanes, bonus, g, x2, wa, wr, gn_vec, ln_vec, seg)


def _swiglu_kernel(expert_ref, x_ref, wg_ref, wu_ref, wd_ref, *rest, alpha):
    if alpha is None:
        out_ref, acc_ref = rest
    else:
        res_ref, ln_ref, out_ref, acc_ref = rest
    j = pl.program_id(1)

    @pl.when(j == 0)
    def _():
        acc_ref[...] = jnp.zeros_like(acc_ref)

    x = x_ref[...]
    gate = jnp.dot(x, wg_ref[0], preferred_element_type=F32)
    up = jnp.dot(x, wu_ref[0], preferred_element_type=F32)
    h = (gate * jax.nn.sigmoid(gate) * up).astype(BF16)
    acc_ref[...] += jnp.dot(h, wd_ref[0], preferred_element_type=F32)

    @pl.when(j == pl.num_programs(1) - 1)
    def _():
        if alpha is None:
            out_ref[...] = acc_ref[...]
        else:
            out_ref[...] = _layer_norm(alpha * res_ref[...] + acc_ref[...], ln_ref[0:1, :], ln_ref[1:2, :])


def _swiglu(tile_expert, xb, wg, wu, wd, tm, fc, residual=None, ln_vec=None, alpha=None):
    m, d = xb.shape
    f = wg.shape[2]
    rows = pl.BlockSpec((tm, d), lambda i, j, e: (i, 0))
    extra_specs, extra_args = [], []
    if residual is not None:
        extra_specs = [rows, pl.BlockSpec(ln_vec.shape, lambda i, j, e: (0, 0))]
        extra_args = [residual, ln_vec]
    return pl.pallas_call(
        functools.partial(_swiglu_kernel, alpha=alpha),
        grid_spec=pltpu.PrefetchScalarGridSpec(
            num_scalar_prefetch=1,
            grid=(m // tm, f // fc),
            in_specs=[rows,
                      pl.BlockSpec((1, d, fc), lambda i, j, e: (e[i], 0, j)),
                      pl.BlockSpec((1, d, fc), lambda i, j, e: (e[i], 0, j)),
                      pl.BlockSpec((1, fc, d), lambda i, j, e: (e[i], j, 0))] + extra_specs,
            out_specs=rows,
            scratch_shapes=[pltpu.VMEM((tm, d), F32)]),
        out_shape=jax.ShapeDtypeStruct((m, d), F32),
        compiler_params=_params("arbitrary", "arbitrary"),
        name="swiglu",
    )(tile_expert, xb, wg, wu, wd, *extra_args)


MAX_FFN_CHUNK = 2048


def _ffn_chunk(f):
    if f % LANES:
        raise ValueError(f"feed-forward width {f} is not a multiple of {LANES}")
    tiles = f // LANES
    return LANES * max(n for n in range(1, tiles + 1) if tiles % n == 0 and n * LANES <= MAX_FFN_CHUNK)


def _router_kernel(x_ref, wr_ref, out_ref):
    logits = jnp.dot(x_ref[...], wr_ref[...], preferred_element_type=F32,
                     precision=lax.Precision.HIGHEST)
    lane = lax.broadcasted_iota(jnp.int32, logits.shape, 1)
    logits = jnp.where(lane < N_EXPERTS, logits, NEG_BIG)
    v1 = jnp.max(logits, axis=-1, keepdims=True)
    i1 = jnp.min(jnp.where(logits == v1, lane, LANES), axis=-1, keepdims=True)
    rest = jnp.where(lane == i1, NEG_BIG, logits)
    v2 = jnp.max(rest, axis=-1, keepdims=True)
    i2 = jnp.min(jnp.where(rest == v2, lane, LANES), axis=-1, keepdims=True)
    e2 = jnp.exp(v2 - v1)
    g1 = 1.0 / (1.0 + e2)
    g2 = e2 / (1.0 + e2)
    out_ref[...] = jnp.where(lane == 0, i1.astype(F32),
                             jnp.where(lane == 1, i2.astype(F32),
                                       jnp.where(lane == 2, g1, jnp.where(lane == 3, g2, 0.0))))


def _router(x2, wr_pad, tm=512):
    m, d = x2.shape
    tm = min(tm, m)
    return pl.pallas_call(
        _router_kernel,
        grid=(m // tm,),
        in_specs=[pl.BlockSpec((tm, d), lambda i: (i, 0)), pl.BlockSpec(wr_pad.shape, lambda i: (0, 0))],
        out_specs=pl.BlockSpec((tm, LANES), lambda i: (i, 0)),
        out_shape=jax.ShapeDtypeStruct((m, LANES), F32),
        compiler_params=_params("arbitrary"),
        name="router_top2",
    )(x2, wr_pad)


def _moe_add_ln_kernel(x_ref, y1_ref, y2_ref, route_ref, ln_ref, out_ref, *, alpha):
    g1 = route_ref[:, 2:3]
    g2 = route_ref[:, 3:4]
    f = g1 * y1_ref[...] + g2 * y2_ref[...]
    out_ref[...] = _layer_norm(alpha * x_ref[...] + f, ln_ref[0:1, :], ln_ref[1:2, :])


def _moe_add_ln(x2, y1, y2, route, ln_vec, alpha, tm=512):
    m, d = x2.shape
    tm = min(tm, m)
    rows = pl.BlockSpec((tm, d), lambda i: (i, 0))
    return pl.pallas_call(
        functools.partial(_moe_add_ln_kernel, alpha=alpha),
        grid=(m // tm,),
        in_specs=[rows, rows, rows, pl.BlockSpec((tm, LANES), lambda i: (i, 0)),
                  pl.BlockSpec(ln_vec.shape, lambda i: (0, 0))],
        out_specs=rows,
        out_shape=jax.ShapeDtypeStruct((m, d), F32),
        compiler_params=_params("arbitrary"),
        name="moe_add_ln",
    )(x2, y1, y2, route, ln_vec)


def _moe(x2, xb, router_w, wg, wu, wd, first_expert, n_exp, ln_vec, alpha, tm):
    m, d = x2.shape
    wr_pad = jnp.pad(router_w, ((0, 0), (0, LANES - n_exp)))
    route = _router(x2, wr_pad)
    idx = route[:, 0:2].astype(jnp.int32).reshape(-1)
    experts = jnp.arange(n_exp, dtype=jnp.int32)
    counts = jnp.sum((idx[:, None] == experts[None, :]).astype(jnp.int32), axis=0)
    padded = ((counts + tm - 1) // tm) * tm
    pad_end = jnp.cumsum(padded)
    pad_start = pad_end - padded
    start = jnp.cumsum(counts) - counts
    order = jnp.argsort(idx, stable=True)
    rank = jnp.argsort(order)
    slot_pos = pad_start[idx] + (rank - start[idx])
    rows_total = 2 * m + n_exp * tm
    tile_start = jnp.arange(rows_total // tm, dtype=jnp.int32) * tm
    tile_expert = jnp.minimum(jnp.sum((tile_start[:, None] >= pad_end[None, :]).astype(jnp.int32), axis=1),
                              n_exp - 1)
    row = jnp.arange(rows_total, dtype=jnp.int32)
    row_expert = jnp.repeat(tile_expert, tm)
    sorted_pos = jnp.clip(row - pad_start[row_expert] + start[row_expert], 0, 2 * m - 1)
    src_token = order[sorted_pos] // 2
    xg = jnp.take(xb, src_token, axis=0, mode="clip")
    yg = _swiglu(tile_expert + first_expert, xg, wg, wu, wd, tm, _ffn_chunk(wg.shape[2]))
    pos = slot_pos.reshape(m, 2)
    y1 = jnp.take(yg, pos[:, 0], axis=0, mode="clip")
    y2 = jnp.take(yg, pos[:, 1], axis=0, mode="clip")
    return _moe_add_ln(x2, y1, y2, route, ln_vec, alpha)


def kernel(x, w_in, w_in_vres, shift_mu, shift_mu_vres, decay_up, decay_base, aaa_up, aaa_base, vres_up, vres_base, gate_up, k_k, k_a, r_k, lnx_g, lnx_b, w_out, ln1_g, ln1_b, ln2_g, ln2_b, ffn_w_gate, ffn_w_up, ffn_w_down, router, moe_w_gate, moe_w_up, moe_w_down):
    b, s, d = x.shape
    depth = w_in.shape[0]
    wdt = decay_up.shape[2]
    att = w_out.shape[1] - wdt
    alpha = float((2 * depth) ** 0.25)
    m = b * s
    ffn_tm = min(512, m)

    seg_head = jnp.arange(SEG_ROWS) // RWKV_HEAD_DIM
    seg = (seg_head[:, None] == seg_head[None, :]).astype(BF16)
    q_scale = jnp.concatenate([jnp.full((att,), ATT_HEAD_DIM ** -0.5, F32), jnp.ones((2 * att,), F32)])
    zero_tiles = jnp.zeros((m // ffn_tm,), jnp.int32)
    ffn_w = [w.astype(BF16) for w in (ffn_w_gate, ffn_w_up, ffn_w_down)]
    n_exp = moe_w_gate.shape[1]
    moe_w = [w.astype(BF16).reshape((-1,) + w.shape[2:]) for w in (moe_w_gate, moe_w_up, moe_w_down)]

    x2 = x.transpose(1, 0, 2).reshape(m, d)
    v_first = None
    for l in range(depth):
        has_vres = l > 0
        w_l = w_in[l]
        c0 = 3 * att
        wqkv = (w_l[:, :c0] * q_scale).astype(BF16)
        wrkv_t = w_l[:, c0:c0 + 3 * wdt].T.astype(BF16)
        lora_w = w_l[:, c0 + 3 * wdt:]
        lora_mu = shift_mu[l, 3 * wdt:]
        if has_vres:
            lora_w = jnp.concatenate([lora_w, w_in_vres[l - 1]], axis=1)
            lora_mu = jnp.concatenate([lora_mu, shift_mu_vres[l - 1]])
        n_lora = lora_w.shape[1]
        wlora_t = jnp.pad(lora_w, ((0, 0), (0, LORA_PAD - n_lora))).T.astype(BF16)
        columns = lambda vec: jnp.broadcast_to(vec[..., None], vec.shape + (LANES,))
        mu_lora = columns(jnp.pad(lora_mu, (0, LORA_PAD - n_lora)))
        mu_rkv = columns(shift_mu[l, :3 * wdt])

        qkv, rkv_t, lora_t = _inproj(x2, wqkv, wrkv_t, wlora_t, mu_rkv, mu_lora, b)

        att_b = _attention(qkv)

        zpad_t = lambda w, rows_before: jnp.pad(
            w, ((rows_before, LANES - rows_before - w.shape[0]), (0, 0))).T.astype(BF16)
        decay_up_p = zpad_t(decay_up[l], 0)
        aaa_up_p = zpad_t(aaa_up[l], DECAY_LORA)
        gate_up_p = gate_up[l].T.astype(BF16)
        if has_vres:
            vres_up_p = zpad_t(vres_up[l - 1], 0)
            vres_b = vres_base[l - 1]
            vf_in = v_first
        else:
            vres_up_p = jnp.zeros((wdt, LANES), BF16)
            vres_b = jnp.zeros((wdt,), F32)
            vf_in = rkv_t
        vecs = columns(jnp.stack([decay_base[l], aaa_base[l], vres_b, k_k[l], k_a[l], r_k[l]]))
        r_, k_, a_, b_, pt_, vl_, v_, g_, bonus = _rwkv_prep(
            rkv_t, lora_t, vf_in, decay_up_p, aaa_up_p, gate_up_p, vres_up_p, vecs, seg, has_vres, b)
        if not has_vres:
            v_first = v_
        y_lanes = _rwkv_scan(a_, b_, k_, r_, pt_, vl_)

        gn_vec = columns(jnp.stack([lnx_g[l], lnx_b[l]]))
        ln1 = jnp.stack([ln1_g[l], ln1_b[l]] + [jnp.zeros((d,), F32)] * 6)
        ln2 = jnp.stack([ln2_g[l], ln2_b[l]] + [jnp.zeros((d,), F32)] * 6)
        wo = w_out[l].astype(BF16)
        x2, xb = _outproj(att_b, y_lanes, bonus, g_, x2, wo[:att], wo[att:], gn_vec, ln1, seg, alpha)

        i = l // 2
        if l % 2 == 0:
            x2 = _swiglu(zero_tiles + i, xb, *ffn_w, ffn_tm, _ffn_chunk(ffn_w[0].shape[2]),
                         residual=x2, ln_vec=ln2, alpha=alpha)
        else:
            x2 = _moe(x2, xb, router[i], *moe_w, i * n_exp, n_exp, ln2, alpha, ffn_tm)
    return x2.reshape(s, b, d).transpose(1, 0, 2)
```

```python
import functools

import jax
import jax.numpy as jnp
from jax import lax
from jax.experimental import pallas as pl
from jax.experimental.pallas import tpu as pltpu

ATT_HEAD_DIM = 64
RWKV_HEAD_DIM = 64
DILATIONS = (1, 4, 16)
ATT_BLOCK = 128
DECAY_LORA = 64
N_EXPERTS = 8
LN_EPS = 1e-5
GN_EPS = 64e-5

LANES = 128
LORA_PAD = 384
VMEM_LIMIT = 56 * 1024 * 1024
NEG_BIG = -1e30

F32 = jnp.float32
BF16 = jnp.bfloat16


def _params(*sem):
    return pltpu.CompilerParams(dimension_semantics=sem, vmem_limit_bytes=VMEM_LIMIT)


def _inproj_kernel(x_ref, wqkv_ref, wrkv_ref, wlora_ref, mu_rkv_ref, mu_lora_ref,
                   qkv_ref, rkv_ref, lora_ref, *rest, batch, from_batch_major):
    if from_batch_major:
        x_rows_ref, carry_rkv, carry_lora, qkv_rows, x_rows = rest
        for bi in range(batch):
            for j in range(x_rows.shape[0]):
                x_rows[j, pl.ds(bi, x_ref.shape[1], stride=batch), :] = x_ref[bi, :, j * LANES:(j + 1) * LANES]
        x = jnp.concatenate([x_rows[j] for j in range(x_rows.shape[0])], axis=1)
        x_rows_ref[...] = x
    else:
        carry_rkv, carry_lora, qkv_rows = rest
        x = x_ref[...]
    xb = x.astype(BF16)
    qkv = jnp.dot(xb, wqkv_ref[...], preferred_element_type=F32)
    n_tiles, rows, _ = qkv_rows.shape
    for j in range(n_tiles):
        qkv_rows[j] = qkv[:, j * LANES:(j + 1) * LANES]
    for bi in range(batch):
        for j in range(n_tiles):
            qkv_ref[bi, :, j * LANES:(j + 1) * LANES] = qkv_rows[j, pl.ds(bi, rows // batch, stride=batch), :]

    @pl.when(pl.program_id(0) == 0)
    def _():
        carry_rkv[...] = jnp.zeros_like(carry_rkv)
        carry_lora[...] = jnp.zeros_like(carry_lora)

    nt = (((1,), (1,)), ((), ()))
    lane = lax.broadcasted_iota(jnp.int32, (1, LANES), 1)

    def shifted(wt_ref, mu_ref, carry_ref, out_ref):
        p = lax.dot_general(wt_ref[...], xb, nt, preferred_element_type=F32)
        tm = p.shape[1]
        mu = mu_ref[...]
        prev = pltpu.roll(p, batch, axis=1)
        from_carry = pltpu.roll(carry_ref[...], batch, axis=1)
        for j in range(tm // LANES):
            cols = slice(j * LANES, (j + 1) * LANES)
            prev_j = prev[:, cols]
            if j == 0:
                prev_j = jnp.where(lane < batch, from_carry, prev_j)
            out_ref[:, cols] = p[:, cols] + (prev_j - p[:, cols]) * mu
        carry_ref[...] = p[:, tm - LANES:tm]

    shifted(wrkv_ref, mu_rkv_ref, carry_rkv, rkv_ref)
    shifted(wlora_ref, mu_lora_ref, carry_lora, lora_ref)


def _inproj(x, wqkv, wrkv_t, wlora_t, mu_rkv, mu_lora, batch, tm=512):
    from_batch_major = x.ndim == 3
    d = x.shape[-1]
    m = x.size // d
    nq, nr, nl = wqkv.shape[1], wrkv_t.shape[0], wlora_t.shape[0]
    tm = min(tm, m)
    full = lambda shape: pl.BlockSpec(shape, lambda i: (0, 0))
    rows = pl.BlockSpec((tm, d), lambda i: (i, 0))
    x_spec = pl.BlockSpec((batch, tm // batch, d), lambda i: (0, i, 0)) if from_batch_major else rows
    return pl.pallas_call(
        functools.partial(_inproj_kernel, batch=batch, from_batch_major=from_batch_major),
        grid=(m // tm,),
        in_specs=[x_spec,
                  full((d, nq)), full((nr, d)), full((nl, d)), full((nr, LANES)), full((nl, LANES))],
        out_specs=[pl.BlockSpec((batch, tm // batch, nq), lambda i: (0, i, 0)),
                   pl.BlockSpec((nr, tm), lambda i: (0, i)),
                   pl.BlockSpec((nl, tm), lambda i: (0, i))] + [rows] * from_batch_major,
        out_shape=[jax.ShapeDtypeStruct((batch, m // batch, nq), F32),
                   jax.ShapeDtypeStruct((nr, m), F32),
                   jax.ShapeDtypeStruct((nl, m), F32)]
                  + [jax.ShapeDtypeStruct((m, d), F32)] * from_batch_major,
        scratch_shapes=[pltpu.VMEM((nr, LANES), F32), pltpu.VMEM((nl, LANES), F32),
                        pltpu.VMEM((nq // LANES, tm, LANES), F32)]
                       + [pltpu.VMEM((d // LANES, tm, LANES), F32)] * from_batch_major,
        compiler_params=_params("arbitrary"),
        name="inproj",
    )(x, wqkv, wrkv_t, wlora_t, mu_rkv, mu_lora)


CLASS_COPY_ROWS = 256
ATT_WINDOW = 2 * ATT_BLOCK
ATT_HEADS_PER_TILE = LANES // ATT_HEAD_DIM


def _attn_kernel(q_ref, k_ref, v_ref, out_ref, acc_ref, m_ref, l_ref, qc_ref, kc_ref, vc_ref,
                 q4_ref, k4_ref, v4_ref, tmp_ref,
                 s0_scr, s1_scr, p0_scr, p1_scr, ml0_scr, ml1_scr, bias_scr):
    seq = q_ref.shape[0]
    n_blocks = seq // ATT_BLOCK
    base = DILATIONS[1]
    sub = DILATIONS[2] // base
    qlen = seq // base
    lane = lax.broadcasted_iota(jnp.int32, (ATT_BLOCK, LANES), 1)
    in_head = [(lane >= h * ATT_HEAD_DIM) & (lane < (h + 1) * ATT_HEAD_DIM)
               for h in range(ATT_HEADS_PER_TILE)]
    rel = (lax.broadcasted_iota(jnp.int32, (ATT_BLOCK, ATT_WINDOW), 0)
           - lax.broadcasted_iota(jnp.int32, (ATT_BLOCK, ATT_WINDOW), 1))
    nt = (((1,), (1,)), ((), ()))
    sources = ((q_ref, q4_ref, qc_ref), (k_ref, k4_ref, kc_ref), (v_ref, v4_ref, vc_ref))

    s_scr, p_scr, ml_scr = (s0_scr, s1_scr), (p0_scr, p1_scr), (ml0_scr, ml1_scr)
    for ref in s_scr + p_scr + ml_scr:
        ref[...] = jnp.zeros_like(ref)
    for first_window in range(2):
        dist = rel + first_window * ATT_BLOCK
        bias_scr[first_window] = jnp.where((dist >= 0) & (dist <= ATT_BLOCK), 0.0, NEG_BIG)

    def split_by_base(c, carry):
        for j in range(qlen // CLASS_COPY_ROWS):
            src = pl.ds(c + j * CLASS_COPY_ROWS * base, CLASS_COPY_ROWS, stride=base)
            dst = pl.ds(pl.multiple_of(c * qlen, CLASS_COPY_ROWS) + j * CLASS_COPY_ROWS, CLASS_COPY_ROWS)
            for full_ref, split_ref, _ in sources:
                split_ref[dst, :] = full_ref[src, :]
        return carry

    lax.fori_loop(0, base, split_by_base, 0)

    def run_pattern(per_class, commit):
        def locate(g):
            g = jnp.clip(g, 0, n_blocks - 1)
            first = (g % per_class) == 0
            q_start = pl.multiple_of(g * ATT_BLOCK, ATT_BLOCK)
            k_start = pl.multiple_of(jnp.where(first, g, g - 1) * ATT_BLOCK, ATT_BLOCK)
            return g, q_start, k_start

        def scores(g, slot):
            _, q_start, k_start = locate(g)
            q = qc_ref[pl.ds(q_start, ATT_BLOCK), :]
            kw = kc_ref[pl.ds(k_start, ATT_WINDOW), :]
            bias = bias_scr[(q_start - k_start) // ATT_BLOCK]
            for h in range(ATT_HEADS_PER_TILE):
                qh = jnp.where(in_head[h], q, jnp.zeros_like(q))
                s_scr[slot][h] = lax.dot_general(qh, kw, nt, preferred_element_type=F32) + bias

        def softmax(slot):
            m_acc = l_acc = jnp.zeros((ATT_BLOCK, LANES), F32)
            for h in range(ATT_HEADS_PER_TILE):
                sc = s_scr[slot][h]
                m = jnp.max(sc, axis=-1, keepdims=True)
                p = jnp.exp(sc - m)
                p_scr[slot][h] = p.astype(BF16)
                m_acc = jnp.where(in_head[h], m, m_acc)
                l_acc = jnp.where(in_head[h], jnp.sum(p, axis=-1, keepdims=True), l_acc)
            ml_scr[slot][0] = m_acc
            ml_scr[slot][1] = l_acc

        def accumulate(g, slot):
            valid = g >= 0
            g, _, k_start = locate(g)
            vw = vc_ref[pl.ds(k_start, ATT_WINDOW), :]
            o = jnp.zeros((ATT_BLOCK, LANES), F32)
            for h in range(ATT_HEADS_PER_TILE):
                o = jnp.where(in_head[h], jnp.dot(p_scr[slot][h], vw, preferred_element_type=F32), o)
            commit(g, valid, o, ml_scr[slot][0], ml_scr[slot][1])

        def pipeline_pair(it, carry):
            for parity in range(2):
                i = 2 * it + parity
                accumulate(i - 4, parity)
                softmax(parity)
                scores(i, parity)
            return carry

        lax.fori_loop(0, (n_blocks + 4) // 2, pipeline_pair, 0, unroll=3)

    def merge(rows, valid, o, m_blk, l_blk):
        m_blk = jnp.where(valid, m_blk, NEG_BIG)
        m_old = m_ref[rows, :]
        m_new = jnp.maximum(m_old, m_blk)
        w_old = jnp.exp(m_old - m_new)
        w_new = jnp.where(valid, jnp.exp(m_blk - m_new), 0.0)
        acc_ref[rows, :] = acc_ref[rows, :] * w_old + o * w_new
        l_ref[rows, :] = l_ref[rows, :] * w_old + l_blk * w_new
        m_ref[rows, :] = m_new

    for j in range(seq // CLASS_COPY_ROWS):
        rows = pl.ds(j * CLASS_COPY_ROWS, CLASS_COPY_ROWS)
        for full_ref, _, class_ref in sources:
            class_ref[rows, :] = full_ref[rows, :].astype(BF16)

    def commit_dense(g, valid, o, m_blk, l_blk):
        del valid
        per_base = ATT_BLOCK // base
        for idx, val in enumerate((o, m_blk, l_blk)):
            tmp_ref[idx] = val
        for c in range(base):
            dst = pl.ds(pl.multiple_of(c * qlen + g * per_base, per_base), per_base)
            for idx, state_ref in enumerate((acc_ref, m_ref, l_ref)):
                state_ref[dst, :] = tmp_ref[idx, pl.ds(c, per_base, stride=base), :]

    run_pattern(n_blocks, commit_dense)

    for j in range(seq // CLASS_COPY_ROWS):
        rows = pl.ds(j * CLASS_COPY_ROWS, CLASS_COPY_ROWS)
        for _, split_ref, class_ref in sources:
            class_ref[rows, :] = split_ref[rows, :].astype(BF16)

    def commit_base(g, valid, o, m_blk, l_blk):
        merge(pl.ds(pl.multiple_of(g * ATT_BLOCK, ATT_BLOCK), ATT_BLOCK), valid, o, m_blk, l_blk)

    run_pattern(qlen // ATT_BLOCK, commit_base)

    sub_len = qlen // sub
    per_sub = sub_len // ATT_BLOCK

    def split_again(cls, carry):
        src = pl.ds((cls // sub) * qlen + cls % sub, sub_len, stride=sub)
        dst = pl.ds(pl.multiple_of(cls * sub_len, CLASS_COPY_ROWS), sub_len)
        for _, split_ref, class_ref in sources:
            class_ref[dst, :] = split_ref[src, :].astype(BF16)
        return carry

    lax.fori_loop(0, base * sub, split_again, 0)

    def commit_sub(g, valid, o, m_blk, l_blk):
        cls = g // per_sub
        start = (cls // sub) * qlen + cls % sub + (g % per_sub) * (ATT_BLOCK * sub)
        merge(pl.ds(start, ATT_BLOCK, stride=sub), valid, o, m_blk, l_blk)

    run_pattern(per_sub, commit_sub)

    for c in range(base):
        for j in range(qlen // CLASS_COPY_ROWS):
            rows = pl.ds(c * qlen + j * CLASS_COPY_ROWS, CLASS_COPY_ROWS)
            out_ref[pl.ds(c + j * CLASS_COPY_ROWS * base, CLASS_COPY_ROWS, stride=base), :] = (
                acc_ref[rows, :] / l_ref[rows, :])


def _attention(qkv_b):
    b, s, c3 = qkv_b.shape
    a = c3 // 3
    npair = a // LANES
    assert s % (DILATIONS[-1] * ATT_WINDOW) == 0 and (s // DILATIONS[-1]) % CLASS_COPY_ROWS == 0
    blk = lambda off: pl.BlockSpec((None, s, LANES), lambda bi, hp: (bi, 0, off + hp))
    pipe = (ATT_HEADS_PER_TILE, ATT_BLOCK, ATT_WINDOW)
    return pl.pallas_call(
        _attn_kernel,
        grid=(b, npair),
        in_specs=[blk(0), blk(npair), blk(2 * npair)],
        out_specs=blk(0),
        out_shape=jax.ShapeDtypeStruct((b, s, a), F32),
        scratch_shapes=[pltpu.VMEM((s, LANES), F32)] * 3 + [pltpu.VMEM((s, LANES), BF16)] * 3
                       + [pltpu.VMEM((s, LANES), F32)] * 3 + [pltpu.VMEM((3, ATT_BLOCK, LANES), F32)]
                       + [pltpu.VMEM(pipe, F32)] * 2 + [pltpu.VMEM(pipe, BF16)] * 2
                       + [pltpu.VMEM((2, ATT_BLOCK, LANES), F32)] * 2
                       + [pltpu.VMEM((2, ATT_BLOCK, ATT_WINDOW), F32)],
        compiler_params=_params("arbitrary", "arbitrary"),
        name="dilated_attn",
    )(qkv_b, qkv_b, qkv_b)


KEY_HALVES = 2
SEG_ROWS = 256
RECURRENCE_CHUNK = 32


def _chunk_transpose(parts, chunk):
    n = len(parts)
    lane = lax.broadcasted_iota(jnp.int32, parts[0].shape, 1)
    s = n // 2
    while s >= 1:
        swap = ((lane // chunk) & s) != 0
        new = list(parts)
        for j in range(n):
            if j & s == 0:
                lo, hi = parts[j], parts[j + s]
                new[j] = jnp.where(swap, pltpu.roll(hi, s * chunk, axis=1), lo)
                new[j + s] = jnp.where(swap, hi, pltpu.roll(lo, LANES - s * chunk, axis=1))
        parts = new
        s //= 2
    return parts


def _keys_to_lanes(q, heads, batch):
    half = RWKV_HEAD_DIM // KEY_HALVES
    parts = [q[(h * KEY_HALVES + kh) * half:(h * KEY_HALVES + kh + 1) * half, :]
             for kh in range(KEY_HALVES) for h in range(heads)]
    return _chunk_transpose(parts, batch)


def _values_to_lanes(q, heads, batch):
    parts = [q[h * RWKV_HEAD_DIM:(h + 1) * RWKV_HEAD_DIM, :]
             for _ in range(KEY_HALVES) for h in range(heads)]
    return _chunk_transpose(parts, batch)


def _values_from_lanes(parts, heads, batch):
    per_half = _chunk_transpose(parts, batch)
    return jnp.concatenate([per_half[h] + per_half[heads + h] for h in range(heads)], axis=0)


def _segment_sum_t(x, seg_ref):
    seg = seg_ref[...]
    n = seg.shape[0]
    pieces = _bf16_pieces(x)
    slabs = []
    for s0 in range(0, x.shape[0], n):
        rows = slice(s0, s0 + n)
        slabs.append(sum(jnp.dot(seg, piece[rows], preferred_element_type=F32) for piece in pieces))
    return jnp.concatenate(slabs, axis=0)


def _bf16_pieces(x):
    hi = x.astype(BF16)
    rest = x - hi.astype(F32)
    mid = rest.astype(BF16)
    lo = (rest - mid.astype(F32)).astype(BF16)
    return hi, mid, lo


def _times_01_matrix(x, mat_ref):
    mat = mat_ref[...]
    return sum(jnp.dot(piece, mat, preferred_element_type=F32) for piece in _bf16_pieces(x))


def _last_step_to_lanes(q, heads, batch):
    half = RWKV_HEAD_DIM // KEY_HALVES
    n = LANES // batch
    lane = lax.broadcasted_iota(jnp.int32, (half, LANES), 1)
    out = jnp.zeros((half, LANES), F32)
    for kh in range(KEY_HALVES):
        for h in range(heads):
            j = kh * heads + h
            slab = q[(h * KEY_HALVES + kh) * half:(h * KEY_HALVES + kh + 1) * half, :]
            moved = pltpu.roll(slab, ((j - (n - 1)) * batch) % LANES, axis=1)
            out = jnp.where(lane // batch == j, moved, out)
    return out


def _rwkv_prep_kernel(rkv_ref, lora_ref, vfirst_ref, decay_up_ref, aaa_up_ref, gate_up_ref,
                      vres_up_ref, vec_ref, seg_ref, upto_ref, whole_ref,
                      r_ref, k_ref, a_ref, b_ref, pt_ref, vl_ref, v_ref, g_ref, bonus_ref,
                      *, has_vres, batch):
    wdt = rkv_ref.shape[0] // 3
    heads = wdt // RWKV_HEAD_DIM
    steps = LANES // batch
    n_groups = rkv_ref.shape[1] // LANES
    decay_base, aaa_base, vres_base = vec_ref[0], vec_ref[1], vec_ref[2]
    k_k, k_a, r_k = vec_ref[3], vec_ref[4], vec_ref[5]
    log_p_before = 0.0
    for gi in range(n_groups):
        cols = slice(gi * LANES, (gi + 1) * LANES)
        r = rkv_ref[0:wdt, cols]
        k = rkv_ref[wdt:2 * wdt, cols]
        v = rkv_ref[2 * wdt:3 * wdt, cols]
        wa = lora_ref[0:LANES, cols]
        gd = lora_ref[LANES:2 * LANES, cols]

        z = -(decay_base + jnp.dot(decay_up_ref[...], jnp.tanh(wa).astype(BF16),
                                   preferred_element_type=F32))
        softplus = jnp.maximum(z, 0.0) + jnp.log(1.0 + jnp.exp(-jnp.abs(z)))
        log_w = -jnp.exp(-softplus - 0.5)
        log_p = log_p_before + _times_01_matrix(log_w, upto_ref)
        if gi + 1 < n_groups:
            log_p_before = log_p_before + _times_01_matrix(log_w, whole_ref)
        p_t = jnp.exp(log_p)
        inv_p_t = jnp.exp(-log_p)
        p_prev = jnp.exp(log_p - log_w)
        a = jax.nn.sigmoid(aaa_base + jnp.dot(aaa_up_ref[...], wa.astype(BF16),
                                              preferred_element_type=F32))
        g_ref[:, cols] = jnp.dot(gate_up_ref[...], jax.nn.sigmoid(gd).astype(BF16),
                                 preferred_element_type=F32)
        if has_vres:
            vd = lora_ref[2 * LANES:3 * LANES, cols]
            mix = jax.nn.sigmoid(vres_base + jnp.dot(vres_up_ref[...], vd.astype(BF16),
                                                     preferred_element_type=F32))
            v = v + (vfirst_ref[:, cols] - v) * mix
        kk = k * k_k
        norm = jnp.sqrt(_segment_sum_t(kk * kk, seg_ref))
        kk = kk / jnp.maximum(norm, 1e-12)
        k2 = k * (1.0 + (a - 1.0) * k_a)
        v_ref[:, cols] = v
        bonus_ref[:, cols] = _segment_sum_t(r * k2 * r_k, seg_ref) * v

        for ref, val in ((r_ref, r * p_t), (k_ref, k2 * inv_p_t), (a_ref, -kk * p_prev),
                         (b_ref, kk * a * inv_p_t)):
            for t, part in enumerate(_keys_to_lanes(val, heads, batch)):
                ref[gi * steps + t] = part
        for t, part in enumerate(_values_to_lanes(v, heads, batch)):
            vl_ref[gi * steps + t] = part
        if gi + 1 == n_groups:
            pt_ref[0] = _last_step_to_lanes(p_t, heads, batch)


def _rwkv_prep(rkv_t, lora_t, vfirst_t, decay_up_t, aaa_up_t, gate_up_t, vres_up_t, vecs, seg,
               has_vres, batch):
    m = rkv_t.shape[1]
    wdt = rkv_t.shape[0] // 3
    heads = wdt // RWKV_HEAD_DIM
    assert KEY_HALVES * heads * batch == LANES, "recurrence layout needs 2*heads*batch == 128 lanes"
    steps = RECURRENCE_CHUNK
    tm = steps * batch
    seq = m // batch
    assert tm % LANES == 0 and seq % steps == 0
    t_idx, b_idx = jnp.arange(LANES) // batch, jnp.arange(LANES) % batch
    same_batch = b_idx[:, None] == b_idx[None, :]
    upto = (same_batch & (t_idx[:, None] <= t_idx[None, :])).astype(BF16)
    whole = same_batch.astype(BF16)
    half = RWKV_HEAD_DIM // KEY_HALVES
    cols = lambda n: pl.BlockSpec((n, tm), lambda i: (0, i))
    full = lambda arr: pl.BlockSpec(arr.shape, lambda i: (0,) * arr.ndim)
    key_spec = pl.BlockSpec((steps, half, LANES), lambda i: (i, 0, 0))
    val_spec = pl.BlockSpec((steps, RWKV_HEAD_DIM, LANES), lambda i: (i, 0, 0))
    key_out = jax.ShapeDtypeStruct((seq, half, LANES), F32)
    val_out = jax.ShapeDtypeStruct((seq, RWKV_HEAD_DIM, LANES), F32)
    nat_out = jax.ShapeDtypeStruct((wdt, m), F32)
    return pl.pallas_call(
        functools.partial(_rwkv_prep_kernel, has_vres=has_vres, batch=batch),
        grid=(m // tm,),
        in_specs=[cols(3 * wdt), cols(LORA_PAD), cols(wdt), full(decay_up_t), full(aaa_up_t),
                  full(gate_up_t), full(vres_up_t), full(vecs), full(seg), full(upto), full(whole)],
        out_specs=[key_spec] * 4 + [pl.BlockSpec((1, half, LANES), lambda i: (i, 0, 0))]
                  + [val_spec] + [cols(wdt)] * 3,
        out_shape=[key_out] * 4 + [jax.ShapeDtypeStruct((seq // steps, half, LANES), F32)]
                  + [val_out] + [nat_out] * 3,
        compiler_params=_params("arbitrary"),
        name="rwkv_prep",
    )(rkv_t, lora_t, vfirst_t, decay_up_t, aaa_up_t, gate_up_t, vres_up_t, vecs, seg, upto, whole)


def _rwkv_scan_kernel(a_ref, a_next_ref, b_ref, k_ref, r_ref, pt_ref, v_ref, y_ref, state_ref, sa_ref):
    @pl.when(pl.program_id(0) == 0)
    def _():
        state_ref[...] = jnp.zeros_like(state_ref)
        sa_ref[...] = jnp.zeros_like(sa_ref)

    steps, nk, _ = a_ref.shape
    nv = v_ref.shape[1]

    def step(t, sa_half, next_a_row, chunk_end):
        sa = sa_half + pltpu.roll(sa_half, LANES // KEY_HALVES, axis=1)
        row = lambda ref, kx: jnp.broadcast_to(ref[t, pl.ds(kx, 1), :], (nv, LANES))
        v_t = v_ref[t]
        acc_y = None
        acc_sa = None
        for kx in range(nk):
            new = state_ref[kx] + v_t * row(k_ref, kx) + sa * row(b_ref, kx)
            term_y = new * row(r_ref, kx)
            if chunk_end:
                new = new * jnp.broadcast_to(pt_ref[0, pl.ds(kx, 1), :], (nv, LANES))
            state_ref[kx] = new
            term_sa = new * jnp.broadcast_to(next_a_row(kx), (nv, LANES))
            acc_y = term_y if acc_y is None else acc_y + term_y
            acc_sa = term_sa if acc_sa is None else acc_sa + term_sa
        y_ref[t] = acc_y
        return acc_sa

    def body(t, sa_half):
        return step(t, sa_half, lambda kx: a_ref[t + 1, pl.ds(kx, 1), :], False)

    sa_half = lax.fori_loop(0, steps - 1, body, sa_ref[...], unroll=2)
    sa_ref[...] = step(steps - 1, sa_half, lambda kx: a_next_ref[0, pl.ds(kx, 1), :], True)


def _rwkv_scan(a_t, b_t, k_t, r_t, pt, v_t):
    s, nk, _ = a_t.shape
    nv = v_t.shape[1]
    steps = RECURRENCE_CHUNK
    n_blocks = s // steps
    kin = pl.BlockSpec((steps, nk, LANES), lambda i: (i, 0, 0))
    knext = pl.BlockSpec((1, nk, LANES), lambda i: (jnp.minimum((i + 1) * steps, s - 1), 0, 0))
    pin = pl.BlockSpec((1, nk, LANES), lambda i: (i, 0, 0))
    vin = pl.BlockSpec((steps, nv, LANES), lambda i: (i, 0, 0))
    return pl.pallas_call(
        _rwkv_scan_kernel,
        grid=(n_blocks,),
        in_specs=[kin, knext] + [kin] * 3 + [pin, vin],
        out_specs=vin,
        out_shape=jax.ShapeDtypeStruct((s, nv, LANES), F32),
        scratch_shapes=[pltpu.VMEM((nk, nv, LANES), F32), pltpu.VMEM((nv, LANES), F32)],
        compiler_params=_params("arbitrary"),
        name="rwkv_scan",
    )(a_t, a_t, b_t, k_t, r_t, pt, v_t)


def _layer_norm(z, g, b):
    mu = jnp.mean(z, axis=-1, keepdims=True)
    zc = z - mu
    var = jnp.mean(zc * zc, axis=-1, keepdims=True)
    return zc * lax.rsqrt(var + LN_EPS) * g + b


def _outproj_kernel(att_ref, y_ref, bonus_ref, g_ref, x_ref, wa_ref, wr_ref, vec_ref, ln_ref,
                    seg_ref, out_ref, outb_ref, att_rows, *, alpha, batch):
    n_tiles = att_rows.shape[0]
    for bi in range(batch):
        for j in range(n_tiles):
            att_rows[j, pl.ds(bi, att_ref.shape[1], stride=batch), :] = att_ref[bi, :, j * LANES:(j + 1) * LANES]
    att = jnp.concatenate([att_rows[j] for j in range(n_tiles)], axis=1)

    heads = bonus_ref.shape[0] // RWKV_HEAD_DIM
    steps = LANES // batch
    inv_n = 1.0 / RWKV_HEAD_DIM
    n_groups = y_ref.shape[0] // steps
    wide = lambda col: jnp.concatenate([col] * n_groups, axis=1)
    y = jnp.concatenate(
        [_values_from_lanes([y_ref[gi * steps + t] for t in range(steps)], heads, batch)
         for gi in range(n_groups)], axis=1)
    mu = _segment_sum_t(y, seg_ref) * inv_n
    yc = y - mu
    var = _segment_sum_t(yc * yc, seg_ref) * inv_n
    yn = yc * lax.rsqrt(var + GN_EPS) * wide(vec_ref[0]) + wide(vec_ref[1])
    rw = ((yn + bonus_ref[...]) * g_ref[...]).T

    mix = (jnp.dot(att.astype(BF16), wa_ref[...], preferred_element_type=F32)
           + jnp.dot(rw.astype(BF16), wr_ref[...], preferred_element_type=F32))
    out = _layer_norm(alpha * x_ref[...] + mix, ln_ref[0:1, :], ln_ref[1:2, :])
    out_ref[...] = out
    outb_ref[...] = out.astype(BF16)


def _outproj(att_b, y_lanes, bonus, g, x2, wa, wr, gn_vec, ln_vec, seg, alpha, tm=512):
    m, d = x2.shape
    batch, _, a = att_b.shape
    tm = min(tm, m)
    assert tm % LANES == 0
    rows = lambda n: pl.BlockSpec((tm, n), lambda i: (i, 0))
    cols = lambda arr: pl.BlockSpec((arr.shape[0], tm), lambda i: (0, i))
    full = lambda arr: pl.BlockSpec(arr.shape, lambda i: (0,) * arr.ndim)
    att_spec = pl.BlockSpec((batch, tm // batch, a), lambda i: (0, i, 0))
    y_spec = pl.BlockSpec((tm // batch,) + y_lanes.shape[1:], lambda i: (i, 0, 0))
    return pl.pallas_call(
        functools.partial(_outproj_kernel, alpha=alpha, batch=batch),
        grid=(m // tm,),
        in_specs=[att_spec, y_spec, cols(bonus), cols(g)]
                 + [rows(d), full(wa), full(wr), full(gn_vec), full(ln_vec), full(seg)],
        out_specs=[rows(d), rows(d)],
        out_shape=[jax.ShapeDtypeStruct((m, d), F32), jax.ShapeDtypeStruct((m, d), BF16)],
        scratch_shapes=[pltpu.VMEM((a // LANES, tm, LANES), F32)],
        compiler_params=_params("arbitrary"),
        name="outproj_ln",
    )(att_b, y_lanes, bonus, g, x2, wa, wr, gn_vec, ln_vec, seg)


def _swiglu_kernel(expert_ref, x_ref, wg_ref, wu_ref, wd_ref, *rest, alpha):
    if alpha is None:
        out_ref, acc_ref = rest
    else:
        res_ref, ln_ref, out_ref, acc_ref = rest
    j = pl.program_id(1)

    @pl.when(j == 0)
    def _():
        acc_ref[...] = jnp.zeros_like(acc_ref)

    x = x_ref[...]
    gate = jnp.dot(x, wg_ref[0], preferred_element_type=F32)
    up = jnp.dot(x, wu_ref[0], preferred_element_type=F32)
    h = (gate * jax.nn.sigmoid(gate) * up).astype(BF16)
    acc_ref[...] += jnp.dot(h, wd_ref[0], preferred_element_type=F32)

    @pl.when(j == pl.num_programs(1) - 1)
    def _():
        if alpha is None:
            out_ref[...] = acc_ref[...]
        else:
            out_ref[...] = _layer_norm(alpha * res_ref[...] + acc_ref[...], ln_ref[0:1, :], ln_ref[1:2, :])


def _swiglu(tile_expert, xb, wg, wu, wd, tm, fc, residual=None, ln_vec=None, alpha=None):
    m, d = xb.shape
    f = wg.shape[2]
    rows = pl.BlockSpec((tm, d), lambda i, j, e: (i, 0))
    extra_specs, extra_args = [], []
    if residual is not None:
        extra_specs = [rows, pl.BlockSpec(ln_vec.shape, lambda i, j, e: (0, 0))]
        extra_args = [residual, ln_vec]
    return pl.pallas_call(
        functools.partial(_swiglu_kernel, alpha=alpha),
        grid_spec=pltpu.PrefetchScalarGridSpec(
            num_scalar_prefetch=1,
            grid=(m // tm, f // fc),
            in_specs=[rows,
                      pl.BlockSpec((1, d, fc), lambda i, j, e: (e[i], 0, j)),
                      pl.BlockSpec((1, d, fc), lambda i, j, e: (e[i], 0, j)),
                      pl.BlockSpec((1, fc, d), lambda i, j, e: (e[i], j, 0))] + extra_specs,
            out_specs=rows,
            scratch_shapes=[pltpu.VMEM((tm, d), F32)]),
        out_shape=jax.ShapeDtypeStruct((m, d), F32),
        compiler_params=_params("arbitrary", "arbitrary"),
        name="swiglu",
    )(tile_expert, xb, wg, wu, wd, *extra_args)


MAX_FFN_CHUNK = 2048


def _ffn_chunk(f):
    if f % LANES:
        raise ValueError(f"feed-forward width {f} is not a multiple of {LANES}")
    tiles = f // LANES
    return LANES * max(n for n in range(1, tiles + 1) if tiles % n == 0 and n * LANES <= MAX_FFN_CHUNK)


def _router_kernel(x_ref, wr_ref, out_ref):
    logits = jnp.dot(x_ref[...], wr_ref[...], preferred_element_type=F32,
                     precision=lax.Precision.HIGHEST)
    lane = lax.broadcasted_iota(jnp.int32, logits.shape, 1)
    logits = jnp.where(lane < N_EXPERTS, logits, NEG_BIG)
    v1 = jnp.max(logits, axis=-1, keepdims=True)
    i1 = jnp.min(jnp.where(logits == v1, lane, LANES), axis=-1, keepdims=True)
    rest = jnp.where(lane == i1, NEG_BIG, logits)
    v2 = jnp.max(rest, axis=-1, keepdims=True)
    i2 = jnp.min(jnp.where(rest == v2, lane, LANES), axis=-1, keepdims=True)
    e2 = jnp.exp(v2 - v1)
    g1 = 1.0 / (1.0 + e2)
    g2 = e2 / (1.0 + e2)
    out_ref[...] = jnp.where(lane == 0, i1.astype(F32),
                             jnp.where(lane == 1, i2.astype(F32),
                                       jnp.where(lane == 2, g1, jnp.where(lane == 3, g2, 0.0))))


def _router(x2, wr_pad, tm=512):
    m, d = x2.shape
    tm = min(tm, m)
    return pl.pallas_call(
        _router_kernel,
        grid=(m // tm,),
        in_specs=[pl.BlockSpec((tm, d), lambda i: (i, 0)), pl.BlockSpec(wr_pad.shape, lambda i: (0, 0))],
        out_specs=pl.BlockSpec((tm, LANES), lambda i: (i, 0)),
        out_shape=jax.ShapeDtypeStruct((m, LANES), F32),
        compiler_params=_params("arbitrary"),
        name="router_top2",
    )(x2, wr_pad)


def _moe_add_ln_kernel(x_ref, y1_ref, y2_ref, route_ref, ln_ref, out_ref, *scratch, alpha, out_batch):
    g1 = route_ref[:, 2:3]
    g2 = route_ref[:, 3:4]
    f = g1 * y1_ref[...] + g2 * y2_ref[...]
    res = _layer_norm(alpha * x_ref[...] + f, ln_ref[0:1, :], ln_ref[1:2, :])
    if out_batch is None:
        out_ref[...] = res
    else:
        rows_scr, = scratch
        n_tiles, rows, _ = rows_scr.shape
        for j in range(n_tiles):
            rows_scr[j] = res[:, j * LANES:(j + 1) * LANES]
        for bi in range(out_batch):
            for j in range(n_tiles):
                out_ref[bi, :, j * LANES:(j + 1) * LANES] = rows_scr[
                    j, pl.ds(bi, rows // out_batch, stride=out_batch), :]


def _moe_add_ln(x2, y1, y2, route, ln_vec, alpha, out_batch=None, tm=512):
    m, d = x2.shape
    tm = min(tm, m)
    rows = pl.BlockSpec((tm, d), lambda i: (i, 0))
    if out_batch is None:
        out_spec, out_shape, scratch = rows, jax.ShapeDtypeStruct((m, d), F32), []
    else:
        out_spec = pl.BlockSpec((out_batch, tm // out_batch, d), lambda i: (0, i, 0))
        out_shape = jax.ShapeDtypeStruct((out_batch, m // out_batch, d), F32)
        scratch = [pltpu.VMEM((d // LANES, tm, LANES), F32)]
    return pl.pallas_call(
        functools.partial(_moe_add_ln_kernel, alpha=alpha, out_batch=out_batch),
        grid=(m // tm,),
        in_specs=[rows, rows, rows, pl.BlockSpec((tm, LANES), lambda i: (i, 0)),
                  pl.BlockSpec(ln_vec.shape, lambda i: (0, 0))],
        out_specs=out_spec,
        out_shape=out_shape,
        scratch_shapes=scratch,
        compiler_params=_params("arbitrary"),
        name="moe_add_ln",
    )(x2, y1, y2, route, ln_vec)


def _moe(x2, xb, router_w, wg, wu, wd, first_expert, n_exp, ln_vec, alpha, tm, out_batch=None):
    m, d = x2.shape
    wr_pad = jnp.pad(router_w, ((0, 0), (0, LANES - n_exp)))
    route = _router(x2, wr_pad)
    idx = route[:, 0:2].astype(jnp.int32).reshape(-1)
    experts = jnp.arange(n_exp, dtype=jnp.int32)
    counts = jnp.sum((idx[:, None] == experts[None, :]).astype(jnp.int32), axis=0)
    padded = ((counts + tm - 1) // tm) * tm
    pad_end = jnp.cumsum(padded)
    pad_start = pad_end - padded
    start = jnp.cumsum(counts) - counts
    order = jnp.argsort(idx, stable=True)
    rank = jnp.argsort(order)
    slot_pos = pad_start[idx] + (rank - start[idx])
    rows_total = 2 * m + n_exp * tm
    tile_start = jnp.arange(rows_total // tm, dtype=jnp.int32) * tm
    tile_expert = jnp.minimum(jnp.sum((tile_start[:, None] >= pad_end[None, :]).astype(jnp.int32), axis=1),
                              n_exp - 1)
    row = jnp.arange(rows_total, dtype=jnp.int32)
    row_expert = jnp.repeat(tile_expert, tm)
    sorted_pos = jnp.clip(row - pad_start[row_expert] + start[row_expert], 0, 2 * m - 1)
    src_token = order[sorted_pos] // 2
    xg = jnp.take(xb, src_token, axis=0, mode="clip")
    yg = _swiglu(tile_expert + first_expert, xg, wg, wu, wd, tm, _ffn_chunk(wg.shape[2]))
    pos = slot_pos.reshape(m, 2)
    y1 = jnp.take(yg, pos[:, 0], axis=0, mode="clip")
    y2 = jnp.take(yg, pos[:, 1], axis=0, mode="clip")
    return _moe_add_ln(x2, y1, y2, route, ln_vec, alpha, out_batch=out_batch)


def kernel(x, w_in, w_in_vres, shift_mu, shift_mu_vres, decay_up, decay_base, aaa_up, aaa_base, vres_up, vres_base, gate_up, k_k, k_a, r_k, lnx_g, lnx_b, w_out, ln1_g, ln1_b, ln2_g, ln2_b, ffn_w_gate, ffn_w_up, ffn_w_down, router, moe_w_gate, moe_w_up, moe_w_down):
    b, s, d = x.shape
    depth = w_in.shape[0]
    wdt = decay_up.shape[2]
    att = w_out.shape[1] - wdt
    alpha = float((2 * depth) ** 0.25)
    m = b * s
    ffn_tm = min(512, m)

    seg_head = jnp.arange(SEG_ROWS) // RWKV_HEAD_DIM
    seg = (seg_head[:, None] == seg_head[None, :]).astype(BF16)
    q_scale = jnp.concatenate([jnp.full((att,), ATT_HEAD_DIM ** -0.5, F32), jnp.ones((2 * att,), F32)])
    zero_tiles = jnp.zeros((m // ffn_tm,), jnp.int32)
    ffn_w = [w.astype(BF16) for w in (ffn_w_gate, ffn_w_up, ffn_w_down)]
    n_exp = moe_w_gate.shape[1]
    moe_w = [w.astype(BF16).reshape((-1,) + w.shape[2:]) for w in (moe_w_gate, moe_w_up, moe_w_down)]

    x2 = None
    v_first = None
    for l in range(depth):
        has_vres = l > 0
        w_l = w_in[l]
        c0 = 3 * att
        wqkv = (w_l[:, :c0] * q_scale).astype(BF16)
        wrkv_t = w_l[:, c0:c0 + 3 * wdt].T.astype(BF16)
        lora_w = w_l[:, c0 + 3 * wdt:]
        lora_mu = shift_mu[l, 3 * wdt:]
        if has_vres:
            lora_w = jnp.concatenate([lora_w, w_in_vres[l - 1]], axis=1)
            lora_mu = jnp.concatenate([lora_mu, shift_mu_vres[l - 1]])
        n_lora = lora_w.shape[1]
        wlora_t = jnp.pad(lora_w, ((0, 0), (0, LORA_PAD - n_lora))).T.astype(BF16)
        columns = lambda vec: jnp.broadcast_to(vec[..., None], vec.shape + (LANES,))
        mu_lora = columns(jnp.pad(lora_mu, (0, LORA_PAD - n_lora)))
        mu_rkv = columns(shift_mu[l, :3 * wdt])

        if l == 0:
            qkv, rkv_t, lora_t, x2 = _inproj(x, wqkv, wrkv_t, wlora_t, mu_rkv, mu_lora, b)
        else:
            qkv, rkv_t, lora_t = _inproj(x2, wqkv, wrkv_t, wlora_t, mu_rkv, mu_lora, b)

        att_b = _attention(qkv)

        zpad_t = lambda w, rows_before: jnp.pad(
            w, ((rows_before, LANES - rows_before - w.shape[0]), (0, 0))).T.astype(BF16)
        decay_up_p = zpad_t(decay_up[l], 0)
        aaa_up_p = zpad_t(aaa_up[l], DECAY_LORA)
        gate_up_p = gate_up[l].T.astype(BF16)
        if has_vres:
            vres_up_p = zpad_t(vres_up[l - 1], 0)
            vres_b = vres_base[l - 1]
            vf_in = v_first
        else:
            vres_up_p = jnp.zeros((wdt, LANES), BF16)
            vres_b = jnp.zeros((wdt,), F32)
            vf_in = rkv_t
        vecs = columns(jnp.stack([decay_base[l], aaa_base[l], vres_b, k_k[l], k_a[l], r_k[l]]))
        r_, k_, a_, b_, pt_, vl_, v_, g_, bonus = _rwkv_prep(
            rkv_t, lora_t, vf_in, decay_up_p, aaa_up_p, gate_up_p, vres_up_p, vecs, seg, has_vres, b)
        if not has_vres:
            v_first = v_
        y_lanes = _rwkv_scan(a_, b_, k_, r_, pt_, vl_)

        gn_vec = columns(jnp.stack([lnx_g[l], lnx_b[l]]))
        ln1 = jnp.stack([ln1_g[l], ln1_b[l]] + [jnp.zeros((d,), F32)] * 6)
        ln2 = jnp.stack([ln2_g[l], ln2_b[l]] + [jnp.zeros((d,), F32)] * 6)
        wo = w_out[l].astype(BF16)
        x2, xb = _outproj(att_b, y_lanes, bonus, g_, x2, wo[:att], wo[att:], gn_vec, ln1, seg, alpha)

        i = l // 2
        if l % 2 == 0:
            x2 = _swiglu(zero_tiles + i, xb, *ffn_w, ffn_tm, _ffn_chunk(ffn_w[0].shape[2]),
                         residual=x2, ln_vec=ln2, alpha=alpha)
        elif l + 1 < depth:
            x2 = _moe(x2, xb, router[i], *moe_w, i * n_exp, n_exp, ln2, alpha, ffn_tm)
        else:
            return _moe(x2, xb, router[i], *moe_w, i * n_exp, n_exp, ln2, alpha, ffn_tm, out_batch=b)
    return x2.reshape(s, b, d).transpose(1, 0, 2)
```

```python
import functools

import jax
import jax.numpy as jnp
from jax import lax
from jax.experimental import pallas as pl
from jax.experimental.pallas import tpu as pltpu

ATT_HEAD_DIM = 64
RWKV_HEAD_DIM = 64
DILATIONS = (1, 4, 16)
ATT_BLOCK = 128
DECAY_LORA = 64
N_EXPERTS = 8
LN_EPS = 1e-5
GN_EPS = 64e-5

LANES = 128
LORA_PAD = 384
VMEM_LIMIT = 56 * 1024 * 1024
NEG_BIG = -1e30

F32 = jnp.float32
BF16 = jnp.bfloat16


def _params(*sem):
    return pltpu.CompilerParams(dimension_semantics=sem, vmem_limit_bytes=VMEM_LIMIT)


def _inproj_kernel(x_ref, wqkv_ref, wrkv_ref, wlora_ref, mu_rkv_ref, mu_lora_ref,
                   qkv_ref, rkv_ref, lora_ref, *rest, batch, from_batch_major):
    if from_batch_major:
        x_rows_ref, carry_rkv, carry_lora, qkv_rows, x_rows = rest
        for bi in range(batch):
            for j in range(x_rows.shape[0]):
                x_rows[j, pl.ds(bi, x_ref.shape[1], stride=batch), :] = x_ref[bi, :, j * LANES:(j + 1) * LANES]
        x = jnp.concatenate([x_rows[j] for j in range(x_rows.shape[0])], axis=1)
        x_rows_ref[...] = x
    else:
        carry_rkv, carry_lora, qkv_rows = rest
        x = x_ref[...]
    xb = x.astype(BF16)
    qkv = jnp.dot(xb, wqkv_ref[...], preferred_element_type=F32)
    n_tiles, rows, _ = qkv_rows.shape
    for j in range(n_tiles):
        qkv_rows[j] = qkv[:, j * LANES:(j + 1) * LANES]
    for bi in range(batch):
        for j in range(n_tiles):
            qkv_ref[bi, :, j * LANES:(j + 1) * LANES] = qkv_rows[j, pl.ds(bi, rows // batch, stride=batch), :]

    @pl.when(pl.program_id(0) == 0)
    def _():
        carry_rkv[...] = jnp.zeros_like(carry_rkv)
        carry_lora[...] = jnp.zeros_like(carry_lora)

    nt = (((1,), (1,)), ((), ()))
    lane = lax.broadcasted_iota(jnp.int32, (1, LANES), 1)

    def shifted(wt_ref, mu_ref, carry_ref, out_ref):
        p = lax.dot_general(wt_ref[...], xb, nt, preferred_element_type=F32)
        tm = p.shape[1]
        mu = mu_ref[...]
        prev = pltpu.roll(p, batch, axis=1)
        from_carry = pltpu.roll(carry_ref[...], batch, axis=1)
        for j in range(tm // LANES):
            cols = slice(j * LANES, (j + 1) * LANES)
            prev_j = prev[:, cols]
            if j == 0:
                prev_j = jnp.where(lane < batch, from_carry, prev_j)
            out_ref[:, cols] = p[:, cols] + (prev_j - p[:, cols]) * mu
        carry_ref[...] = p[:, tm - LANES:tm]

    shifted(wrkv_ref, mu_rkv_ref, carry_rkv, rkv_ref)
    shifted(wlora_ref, mu_lora_ref, carry_lora, lora_ref)


def _inproj(x, wqkv, wrkv_t, wlora_t, mu_rkv, mu_lora, batch, tm=512):
    from_batch_major = x.ndim == 3
    d = x.shape[-1]
    m = x.size // d
    nq, nr, nl = wqkv.shape[1], wrkv_t.shape[0], wlora_t.shape[0]
    tm = min(tm, m)
    full = lambda shape: pl.BlockSpec(shape, lambda i: (0, 0))
    rows = pl.BlockSpec((tm, d), lambda i: (i, 0))
    x_spec = pl.BlockSpec((batch, tm // batch, d), lambda i: (0, i, 0)) if from_batch_major else rows
    return pl.pallas_call(
        functools.partial(_inproj_kernel, batch=batch, from_batch_major=from_batch_major),
        grid=(m // tm,),
        in_specs=[x_spec,
                  full((d, nq)), full((nr, d)), full((nl, d)), full((nr, LANES)), full((nl, LANES))],
        out_specs=[pl.BlockSpec((batch, tm // batch, nq), lambda i: (0, i, 0)),
                   pl.BlockSpec((nr, tm), lambda i: (0, i)),
                   pl.BlockSpec((nl, tm), lambda i: (0, i))] + [rows] * from_batch_major,
        out_shape=[jax.ShapeDtypeStruct((batch, m // batch, nq), F32),
                   jax.ShapeDtypeStruct((nr, m), F32),
                   jax.ShapeDtypeStruct((nl, m), F32)]
                  + [jax.ShapeDtypeStruct((m, d), F32)] * from_batch_major,
        scratch_shapes=[pltpu.VMEM((nr, LANES), F32), pltpu.VMEM((nl, LANES), F32),
                        pltpu.VMEM((nq // LANES, tm, LANES), F32)]
                       + [pltpu.VMEM((d // LANES, tm, LANES), F32)] * from_batch_major,
        compiler_params=_params("arbitrary"),
        name="inproj",
    )(x, wqkv, wrkv_t, wlora_t, mu_rkv, mu_lora)


CLASS_COPY_ROWS = 256
ATT_WINDOW = 2 * ATT_BLOCK
ATT_HEADS_PER_TILE = LANES // ATT_HEAD_DIM


def _attn_kernel(q_ref, k_ref, v_ref, out_ref, acc_ref, m_ref, l_ref, qc_ref, kc_ref, vc_ref,
                 q4_ref, k4_ref, v4_ref, tmp_ref,
                 s0_scr, s1_scr, p0_scr, p1_scr, ml0_scr, ml1_scr, bias_scr):
    seq = q_ref.shape[0]
    n_blocks = seq // ATT_BLOCK
    base = DILATIONS[1]
    sub = DILATIONS[2] // base
    qlen = seq // base
    lane = lax.broadcasted_iota(jnp.int32, (ATT_BLOCK, LANES), 1)
    in_head = [(lane >= h * ATT_HEAD_DIM) & (lane < (h + 1) * ATT_HEAD_DIM)
               for h in range(ATT_HEADS_PER_TILE)]
    rel = (lax.broadcasted_iota(jnp.int32, (ATT_BLOCK, ATT_WINDOW), 0)
           - lax.broadcasted_iota(jnp.int32, (ATT_BLOCK, ATT_WINDOW), 1))
    nt = (((1,), (1,)), ((), ()))
    sources = ((q_ref, q4_ref, qc_ref), (k_ref, k4_ref, kc_ref), (v_ref, v4_ref, vc_ref))

    s_scr, p_scr, ml_scr = (s0_scr, s1_scr), (p0_scr, p1_scr), (ml0_scr, ml1_scr)
    for ref in s_scr + p_scr + ml_scr:
        ref[...] = jnp.zeros_like(ref)
    for first_window in range(2):
        dist = rel + first_window * ATT_BLOCK
        bias_scr[first_window] = jnp.where((dist >= 0) & (dist <= ATT_BLOCK), 0.0, NEG_BIG)

    def split_by_base(c, carry):
        for j in range(qlen // CLASS_COPY_ROWS):
            src = pl.ds(c + j * CLASS_COPY_ROWS * base, CLASS_COPY_ROWS, stride=base)
            dst = pl.ds(pl.multiple_of(c * qlen, CLASS_COPY_ROWS) + j * CLASS_COPY_ROWS, CLASS_COPY_ROWS)
            for full_ref, split_ref, _ in sources:
                split_ref[dst, :] = full_ref[src, :]
        return carry

    lax.fori_loop(0, base, split_by_base, 0)

    def run_pattern(per_class, commit):
        def locate(g):
            g = jnp.clip(g, 0, n_blocks - 1)
            first = (g % per_class) == 0
            q_start = pl.multiple_of(g * ATT_BLOCK, ATT_BLOCK)
            k_start = pl.multiple_of(jnp.where(first, g, g - 1) * ATT_BLOCK, ATT_BLOCK)
            return g, q_start, k_start

        def scores(g, slot):
            _, q_start, k_start = locate(g)
            q = qc_ref[pl.ds(q_start, ATT_BLOCK), :]
            kw = kc_ref[pl.ds(k_start, ATT_WINDOW), :]
            bias = bias_scr[(q_start - k_start) // ATT_BLOCK]
            for h in range(ATT_HEADS_PER_TILE):
                qh = jnp.where(in_head[h], q, jnp.zeros_like(q))
                s_scr[slot][h] = lax.dot_general(qh, kw, nt, preferred_element_type=F32) + bias

        def softmax(slot):
            m_acc = l_acc = jnp.zeros((ATT_BLOCK, LANES), F32)
            for h in range(ATT_HEADS_PER_TILE):
                sc = s_scr[slot][h]
                m = jnp.max(sc, axis=-1, keepdims=True)
                p = jnp.exp(sc - m)
                p_scr[slot][h] = p.astype(BF16)
                m_acc = jnp.where(in_head[h], m, m_acc)
                l_acc = jnp.where(in_head[h], jnp.sum(p, axis=-1, keepdims=True), l_acc)
            ml_scr[slot][0] = m_acc
            ml_scr[slot][1] = l_acc

        def accumulate(g, slot):
            valid = g >= 0
            g, _, k_start = locate(g)
            vw = vc_ref[pl.ds(k_start, ATT_WINDOW), :]
            o = jnp.zeros((ATT_BLOCK, LANES), F32)
            for h in range(ATT_HEADS_PER_TILE):
                o = jnp.where(in_head[h], jnp.dot(p_scr[slot][h], vw, preferred_element_type=F32), o)
            commit(g, valid, o, ml_scr[slot][0], ml_scr[slot][1])

        def pipeline_pair(it, carry):
            for parity in range(2):
                i = 2 * it + parity
                accumulate(i - 4, parity)
                softmax(parity)
                scores(i, parity)
            return carry

        lax.fori_loop(0, (n_blocks + 4) // 2, pipeline_pair, 0, unroll=3)

    def merge(rows, valid, o, m_blk, l_blk):
        m_blk = jnp.where(valid, m_blk, NEG_BIG)
        m_old = m_ref[rows, :]
        m_new = jnp.maximum(m_old, m_blk)
        w_old = jnp.exp(m_old - m_new)
        w_new = jnp.where(valid, jnp.exp(m_blk - m_new), 0.0)
        acc_ref[rows, :] = acc_ref[rows, :] * w_old + o * w_new
        l_ref[rows, :] = l_ref[rows, :] * w_old + l_blk * w_new
        m_ref[rows, :] = m_new

    for j in range(seq // CLASS_COPY_ROWS):
        rows = pl.ds(j * CLASS_COPY_ROWS, CLASS_COPY_ROWS)
        for full_ref, _, class_ref in sources:
            class_ref[rows, :] = full_ref[rows, :].astype(BF16)

    def commit_dense(g, valid, o, m_blk, l_blk):
        del valid
        per_base = ATT_BLOCK // base
        for idx, val in enumerate((o, m_blk, l_blk)):
            tmp_ref[idx] = val
        for c in range(base):
            dst = pl.ds(pl.multiple_of(c * qlen + g * per_base, per_base), per_base)
            for idx, state_ref in enumerate((acc_ref, m_ref, l_ref)):
                state_ref[dst, :] = tmp_ref[idx, pl.ds(c, per_base, stride=base), :]

    run_pattern(n_blocks, commit_dense)

    for j in range(seq // CLASS_COPY_ROWS):
        rows = pl.ds(j * CLASS_COPY_ROWS, CLASS_COPY_ROWS)
        for _, split_ref, class_ref in sources:
            class_ref[rows, :] = split_ref[rows, :].astype(BF16)

    def commit_base(g, valid, o, m_blk, l_blk):
        merge(pl.ds(pl.multiple_of(g * ATT_BLOCK, ATT_BLOCK), ATT_BLOCK), valid, o, m_blk, l_blk)

    run_pattern(qlen // ATT_BLOCK, commit_base)

    sub_len = qlen // sub
    per_sub = sub_len // ATT_BLOCK

    def split_again(cls, carry):
        src = pl.ds((cls // sub) * qlen + cls % sub, sub_len, stride=sub)
        dst = pl.ds(pl.multiple_of(cls * sub_len, CLASS_COPY_ROWS), sub_len)
        for _, split_ref, class_ref in sources:
            class_ref[dst, :] = split_ref[src, :].astype(BF16)
        return carry

    lax.fori_loop(0, base * sub, split_again, 0)

    def commit_sub(g, valid, o, m_blk, l_blk):
        cls = g // per_sub
        start = (cls // sub) * qlen + cls % sub + (g % per_sub) * (ATT_BLOCK * sub)
        merge(pl.ds(start, ATT_BLOCK, stride=sub), valid, o, m_blk, l_blk)

    run_pattern(per_sub, commit_sub)

    for c in range(base):
        for j in range(qlen // CLASS_COPY_ROWS):
            rows = pl.ds(c * qlen + j * CLASS_COPY_ROWS, CLASS_COPY_ROWS)
            out_ref[pl.ds(c + j * CLASS_COPY_ROWS * base, CLASS_COPY_ROWS, stride=base), :] = (
                acc_ref[rows, :] / l_ref[rows, :])


def _attention(qkv_b):
    b, s, c3 = qkv_b.shape
    a = c3 // 3
    npair = a // LANES
    assert s % (DILATIONS[-1] * ATT_WINDOW) == 0 and (s // DILATIONS[-1]) % CLASS_COPY_ROWS == 0
    blk = lambda off: pl.BlockSpec((None, s, LANES), lambda bi, hp: (bi, 0, off + hp))
    pipe = (ATT_HEADS_PER_TILE, ATT_BLOCK, ATT_WINDOW)
    return pl.pallas_call(
        _attn_kernel,
        grid=(b, npair),
        in_specs=[blk(0), blk(npair), blk(2 * npair)],
        out_specs=blk(0),
        out_shape=jax.ShapeDtypeStruct((b, s, a), F32),
        scratch_shapes=[pltpu.VMEM((s, LANES), F32)] * 3 + [pltpu.VMEM((s, LANES), BF16)] * 3
                       + [pltpu.VMEM((s, LANES), F32)] * 3 + [pltpu.VMEM((3, ATT_BLOCK, LANES), F32)]
                       + [pltpu.VMEM(pipe, F32)] * 2 + [pltpu.VMEM(pipe, BF16)] * 2
                       + [pltpu.VMEM((2, ATT_BLOCK, LANES), F32)] * 2
                       + [pltpu.VMEM((2, ATT_BLOCK, ATT_WINDOW), F32)],
        compiler_params=_params("arbitrary", "arbitrary"),
        name="dilated_attn",
    )(qkv_b, qkv_b, qkv_b)


KEY_HALVES = 2
SEG_ROWS = 256
RECURRENCE_CHUNK = 32


def _chunk_transpose(parts, chunk):
    n = len(parts)
    lane = lax.broadcasted_iota(jnp.int32, parts[0].shape, 1)
    s = n // 2
    while s >= 1:
        swap = ((lane // chunk) & s) != 0
        new = list(parts)
        for j in range(n):
            if j & s == 0:
                lo, hi = parts[j], parts[j + s]
                new[j] = jnp.where(swap, pltpu.roll(hi, s * chunk, axis=1), lo)
                new[j + s] = jnp.where(swap, hi, pltpu.roll(lo, LANES - s * chunk, axis=1))
        parts = new
        s //= 2
    return parts


def _keys_to_lanes(q, heads, batch):
    half = RWKV_HEAD_DIM // KEY_HALVES
    parts = [q[(h * KEY_HALVES + kh) * half:(h * KEY_HALVES + kh + 1) * half, :]
             for kh in range(KEY_HALVES) for h in range(heads)]
    return _chunk_transpose(parts, batch)


def _values_to_lanes(q, heads, batch):
    parts = [q[h * RWKV_HEAD_DIM:(h + 1) * RWKV_HEAD_DIM, :]
             for _ in range(KEY_HALVES) for h in range(heads)]
    return _chunk_transpose(parts, batch)


def _values_from_lanes(parts, heads, batch):
    per_half = _chunk_transpose(parts, batch)
    return jnp.concatenate([per_half[h] + per_half[heads + h] for h in range(heads)], axis=0)


def _segment_sum_t(x, seg_ref):
    seg = seg_ref[...]
    n = seg.shape[0]
    pieces = _bf16_pieces(x)
    slabs = []
    for s0 in range(0, x.shape[0], n):
        rows = slice(s0, s0 + n)
        slabs.append(sum(jnp.dot(seg, piece[rows], preferred_element_type=F32) for piece in pieces))
    return jnp.concatenate(slabs, axis=0)


def _bf16_pieces(x):
    hi = x.astype(BF16)
    rest = x - hi.astype(F32)
    mid = rest.astype(BF16)
    lo = (rest - mid.astype(F32)).astype(BF16)
    return hi, mid, lo


def _times_01_matrix(x, mat_ref):
    mat = mat_ref[...]
    return sum(jnp.dot(piece, mat, preferred_element_type=F32) for piece in _bf16_pieces(x))


def _last_step_to_lanes(q, heads, batch):
    half = RWKV_HEAD_DIM // KEY_HALVES
    n = LANES // batch
    lane = lax.broadcasted_iota(jnp.int32, (half, LANES), 1)
    out = jnp.zeros((half, LANES), F32)
    for kh in range(KEY_HALVES):
        for h in range(heads):
            j = kh * heads + h
            slab = q[(h * KEY_HALVES + kh) * half:(h * KEY_HALVES + kh + 1) * half, :]
            moved = pltpu.roll(slab, ((j - (n - 1)) * batch) % LANES, axis=1)
            out = jnp.where(lane // batch == j, moved, out)
    return out


def _rwkv_prep_kernel(rkv_ref, lora_ref, vfirst_ref, decay_up_ref, aaa_up_ref, gate_up_ref,
                      vres_up_ref, vec_ref, seg_ref, upto_ref, whole_ref,
                      r_ref, k_ref, a_ref, b_ref, pt_ref, vl_ref, v_ref, g_ref, bonus_ref,
                      *, has_vres, batch):
    wdt = rkv_ref.shape[0] // 3
    heads = wdt // RWKV_HEAD_DIM
    steps = LANES // batch
    n_groups = rkv_ref.shape[1] // LANES
    decay_base, aaa_base, vres_base = vec_ref[0], vec_ref[1], vec_ref[2]
    k_k, k_a, r_k = vec_ref[3], vec_ref[4], vec_ref[5]
    log_p_before = 0.0
    for gi in range(n_groups):
        cols = slice(gi * LANES, (gi + 1) * LANES)
        r = rkv_ref[0:wdt, cols]
        k = rkv_ref[wdt:2 * wdt, cols]
        v = rkv_ref[2 * wdt:3 * wdt, cols]
        wa = lora_ref[0:LANES, cols]
        gd = lora_ref[LANES:2 * LANES, cols]

        z = -(decay_base + jnp.dot(decay_up_ref[...], jnp.tanh(wa).astype(BF16),
                                   preferred_element_type=F32))
        softplus = jnp.maximum(z, 0.0) + jnp.log(1.0 + jnp.exp(-jnp.abs(z)))
        log_w = -jnp.exp(-softplus - 0.5)
        log_p = log_p_before + _times_01_matrix(log_w, upto_ref)
        if gi + 1 < n_groups:
            log_p_before = log_p_before + _times_01_matrix(log_w, whole_ref)
        p_t = jnp.exp(log_p)
        inv_p_t = jnp.exp(-log_p)
        p_prev = jnp.exp(log_p - log_w)
        a = jax.nn.sigmoid(aaa_base + jnp.dot(aaa_up_ref[...], wa.astype(BF16),
                                              preferred_element_type=F32))
        g_ref[:, cols] = jnp.dot(gate_up_ref[...], jax.nn.sigmoid(gd).astype(BF16),
                                 preferred_element_type=F32)
        if has_vres:
            vd = lora_ref[2 * LANES:3 * LANES, cols]
            mix = jax.nn.sigmoid(vres_base + jnp.dot(vres_up_ref[...], vd.astype(BF16),
                                                     preferred_element_type=F32))
            v = v + (vfirst_ref[:, cols] - v) * mix
        kk = k * k_k
        norm = jnp.sqrt(_segment_sum_t(kk * kk, seg_ref))
        kk = kk / jnp.maximum(norm, 1e-12)
        k2 = k * (1.0 + (a - 1.0) * k_a)
        v_ref[:, cols] = v
        bonus_ref[:, cols] = _segment_sum_t(r * k2 * r_k, seg_ref) * v

        for ref, val in ((r_ref, r * p_t), (k_ref, k2 * inv_p_t), (a_ref, -kk * p_prev),
                         (b_ref, kk * a * inv_p_t)):
            for t, part in enumerate(_keys_to_lanes(val, heads, batch)):
                ref[gi * steps + t] = part
        for t, part in enumerate(_values_to_lanes(v, heads, batch)):
            vl_ref[gi * steps + t] = part
        if gi + 1 == n_groups:
            pt_ref[0] = _last_step_to_lanes(p_t, heads, batch)


def _rwkv_prep(rkv_t, lora_t, vfirst_t, decay_up_t, aaa_up_t, gate_up_t, vres_up_t, vecs, seg,
               has_vres, batch):
    m = rkv_t.shape[1]
    wdt = rkv_t.shape[0] // 3
    heads = wdt // RWKV_HEAD_DIM
    assert KEY_HALVES * heads * batch == LANES, "recurrence layout needs 2*heads*batch == 128 lanes"
    steps = RECURRENCE_CHUNK
    tm = steps * batch
    seq = m // batch
    assert tm % LANES == 0 and seq % steps == 0
    t_idx, b_idx = jnp.arange(LANES) // batch, jnp.arange(LANES) % batch
    same_batch = b_idx[:, None] == b_idx[None, :]
    upto = (same_batch & (t_idx[:, None] <= t_idx[None, :])).astype(BF16)
    whole = same_batch.astype(BF16)
    half = RWKV_HEAD_DIM // KEY_HALVES
    cols = lambda n: pl.BlockSpec((n, tm), lambda i: (0, i))
    full = lambda arr: pl.BlockSpec(arr.shape, lambda i: (0,) * arr.ndim)
    key_spec = pl.BlockSpec((steps, half, LANES), lambda i: (i, 0, 0))
    val_spec = pl.BlockSpec((steps, RWKV_HEAD_DIM, LANES), lambda i: (i, 0, 0))
    key_out = jax.ShapeDtypeStruct((seq, half, LANES), F32)
    val_out = jax.ShapeDtypeStruct((seq, RWKV_HEAD_DIM, LANES), F32)
    nat_out = jax.ShapeDtypeStruct((wdt, m), F32)
    return pl.pallas_call(
        functools.partial(_rwkv_prep_kernel, has_vres=has_vres, batch=batch),
        grid=(m // tm,),
        in_specs=[cols(3 * wdt), cols(LORA_PAD), cols(wdt), full(decay_up_t), full(aaa_up_t),
                  full(gate_up_t), full(vres_up_t), full(vecs), full(seg), full(upto), full(whole)],
        out_specs=[key_spec] * 4 + [pl.BlockSpec((1, half, LANES), lambda i: (i, 0, 0))]
                  + [val_spec] + [cols(wdt)] * 3,
        out_shape=[key_out] * 4 + [jax.ShapeDtypeStruct((seq // steps, half, LANES), F32)]
                  + [val_out] + [nat_out] * 3,
        compiler_params=_params("arbitrary"),
        name="rwkv_prep",
    )(rkv_t, lora_t, vfirst_t, decay_up_t, aaa_up_t, gate_up_t, vres_up_t, vecs, seg, upto, whole)


def _rwkv_scan_kernel(a_ref, a_next_ref, b_ref, k_ref, r_ref, pt_ref, v_ref, y_ref, state_ref, sa_ref):
    @pl.when(pl.program_id(0) == 0)
    def _():
        state_ref[...] = jnp.zeros_like(state_ref)
        sa_ref[...] = jnp.zeros_like(sa_ref)

    steps, nk, _ = a_ref.shape
    nv = v_ref.shape[1]

    def step(t, sa_half, next_a_row, chunk_end):
        sa = sa_half + pltpu.roll(sa_half, LANES // KEY_HALVES, axis=1)
        row = lambda ref, kx: jnp.broadcast_to(ref[t, pl.ds(kx, 1), :], (nv, LANES))
        v_t = v_ref[t]
        acc_y = None
        acc_sa = None
        for kx in range(nk):
            new = state_ref[kx] + v_t * row(k_ref, kx) + sa * row(b_ref, kx)
            term_y = new * row(r_ref, kx)
            if chunk_end:
                new = new * jnp.broadcast_to(pt_ref[0, pl.ds(kx, 1), :], (nv, LANES))
            state_ref[kx] = new
            term_sa = new * jnp.broadcast_to(next_a_row(kx), (nv, LANES))
            acc_y = term_y if acc_y is None else acc_y + term_y
            acc_sa = term_sa if acc_sa is None else acc_sa + term_sa
        y_ref[t] = acc_y
        return acc_sa

    def body(t, sa_half):
        return step(t, sa_half, lambda kx: a_ref[t + 1, pl.ds(kx, 1), :], False)

    sa_half = lax.fori_loop(0, steps - 1, body, sa_ref[...], unroll=2)
    sa_ref[...] = step(steps - 1, sa_half, lambda kx: a_next_ref[0, pl.ds(kx, 1), :], True)


def _rwkv_scan(a_t, b_t, k_t, r_t, pt, v_t):
    s, nk, _ = a_t.shape
    nv = v_t.shape[1]
    steps = RECURRENCE_CHUNK
    n_blocks = s // steps
    kin = pl.BlockSpec((steps, nk, LANES), lambda i: (i, 0, 0))
    knext = pl.BlockSpec((1, nk, LANES), lambda i: (jnp.minimum((i + 1) * steps, s - 1), 0, 0))
    pin = pl.BlockSpec((1, nk, LANES), lambda i: (i, 0, 0))
    vin = pl.BlockSpec((steps, nv, LANES), lambda i: (i, 0, 0))
    return pl.pallas_call(
        _rwkv_scan_kernel,
        grid=(n_blocks,),
        in_specs=[kin, knext] + [kin] * 3 + [pin, vin],
        out_specs=vin,
        out_shape=jax.ShapeDtypeStruct((s, nv, LANES), F32),
        scratch_shapes=[pltpu.VMEM((nk, nv, LANES), F32), pltpu.VMEM((nv, LANES), F32)],
        compiler_params=_params("arbitrary"),
        name="rwkv_scan",
    )(a_t, a_t, b_t, k_t, r_t, pt, v_t)


def _layer_norm(z, g, b):
    mu = jnp.mean(z, axis=-1, keepdims=True)
    zc = z - mu
    var = jnp.mean(zc * zc, axis=-1, keepdims=True)
    return zc * lax.rsqrt(var + LN_EPS) * g + b


def _outproj_kernel(att_ref, y_ref, bonus_ref, g_ref, x_ref, wa_ref, wr_ref, vec_ref, ln_ref,
                    seg_ref, out_ref, outb_ref, att_rows, *, alpha, batch):
    n_tiles = att_rows.shape[0]
    for bi in range(batch):
        for j in range(n_tiles):
            att_rows[j, pl.ds(bi, att_ref.shape[1], stride=batch), :] = att_ref[bi, :, j * LANES:(j + 1) * LANES]
    att = jnp.concatenate([att_rows[j] for j in range(n_tiles)], axis=1)

    heads = bonus_ref.shape[0] // RWKV_HEAD_DIM
    steps = LANES // batch
    inv_n = 1.0 / RWKV_HEAD_DIM
    n_groups = y_ref.shape[0] // steps
    wide = lambda col: jnp.concatenate([col] * n_groups, axis=1)
    y = jnp.concatenate(
        [_values_from_lanes([y_ref[gi * steps + t] for t in range(steps)], heads, batch)
         for gi in range(n_groups)], axis=1)
    mu = _segment_sum_t(y, seg_ref) * inv_n
    yc = y - mu
    var = _segment_sum_t(yc * yc, seg_ref) * inv_n
    yn = yc * lax.rsqrt(var + GN_EPS) * wide(vec_ref[0]) + wide(vec_ref[1])
    rw = ((yn + bonus_ref[...]) * g_ref[...]).T

    mix = (jnp.dot(att.astype(BF16), wa_ref[...], preferred_element_type=F32)
           + jnp.dot(rw.astype(BF16), wr_ref[...], preferred_element_type=F32))
    out = _layer_norm(alpha * x_ref[...] + mix, ln_ref[0:1, :], ln_ref[1:2, :])
    out_ref[...] = out
    outb_ref[...] = out.astype(BF16)


def _outproj(att_b, y_lanes, bonus, g, x2, wa, wr, gn_vec, ln_vec, seg, alpha, tm=512):
    m, d = x2.shape
    batch, _, a = att_b.shape
    tm = min(tm, m)
    assert tm % LANES == 0
    rows = lambda n: pl.BlockSpec((tm, n), lambda i: (i, 0))
    cols = lambda arr: pl.BlockSpec((arr.shape[0], tm), lambda i: (0, i))
    full = lambda arr: pl.BlockSpec(arr.shape, lambda i: (0,) * arr.ndim)
    att_spec = pl.BlockSpec((batch, tm // batch, a), lambda i: (0, i, 0))
    y_spec = pl.BlockSpec((tm // batch,) + y_lanes.shape[1:], lambda i: (i, 0, 0))
    return pl.pallas_call(
        functools.partial(_outproj_kernel, alpha=alpha, batch=batch),
        grid=(m // tm,),
        in_specs=[att_spec, y_spec, cols(bonus), cols(g)]
                 + [rows(d), full(wa), full(wr), full(gn_vec), full(ln_vec), full(seg)],
        out_specs=[rows(d), rows(d)],
        out_shape=[jax.ShapeDtypeStruct((m, d), F32), jax.ShapeDtypeStruct((m, d), BF16)],
        scratch_shapes=[pltpu.VMEM((a // LANES, tm, LANES), F32)],
        compiler_params=_params("arbitrary"),
        name="outproj_ln",
    )(att_b, y_lanes, bonus, g, x2, wa, wr, gn_vec, ln_vec, seg)


def _swiglu_kernel(expert_ref, x_ref, wg_ref, wu_ref, wd_ref, *rest, alpha):
    if alpha is None:
        out_ref, acc_ref = rest
    else:
        res_ref, ln_ref, out_ref, acc_ref = rest
    j = pl.program_id(1)

    @pl.when(j == 0)
    def _():
        acc_ref[...] = jnp.zeros_like(acc_ref)

    x = x_ref[...]
    gate = jnp.dot(x, wg_ref[0], preferred_element_type=F32)
    up = jnp.dot(x, wu_ref[0], preferred_element_type=F32)
    h = (gate * jax.nn.sigmoid(gate) * up).astype(BF16)
    acc_ref[...] += jnp.dot(h, wd_ref[0], preferred_element_type=F32)

    @pl.when(j == pl.num_programs(1) - 1)
    def _():
        if alpha is None:
            out_ref[...] = acc_ref[...]
        else:
            out_ref[...] = _layer_norm(alpha * res_ref[...] + acc_ref[...], ln_ref[0:1, :], ln_ref[1:2, :])


def _swiglu(tile_expert, xb, wg, wu, wd, tm, fc, residual=None, ln_vec=None, alpha=None):
    m, d = xb.shape
    f = wg.shape[2]
    rows = pl.BlockSpec((tm, d), lambda i, j, e: (i, 0))
    extra_specs, extra_args = [], []
    if residual is not None:
        extra_specs = [rows, pl.BlockSpec(ln_vec.shape, lambda i, j, e: (0, 0))]
        extra_args = [residual, ln_vec]
    return pl.pallas_call(
        functools.partial(_swiglu_kernel, alpha=alpha),
        grid_spec=pltpu.PrefetchScalarGridSpec(
            num_scalar_prefetch=1,
            grid=(m // tm, f // fc),
            in_specs=[rows,
                      pl.BlockSpec((1, d, fc), lambda i, j, e: (e[i], 0, j)),
                      pl.BlockSpec((1, d, fc), lambda i, j, e: (e[i], 0, j)),
                      pl.BlockSpec((1, fc, d), lambda i, j, e: (e[i], j, 0))] + extra_specs,
            out_specs=rows,
            scratch_shapes=[pltpu.VMEM((tm, d), F32)]),
        out_shape=jax.ShapeDtypeStruct((m, d), F32),
        compiler_params=_params("arbitrary", "arbitrary"),
        name="swiglu",
    )(tile_expert, xb, wg, wu, wd, *extra_args)


MAX_FFN_CHUNK = 2048


def _ffn_chunk(f):
    if f % LANES:
        raise ValueError(f"feed-forward width {f} is not a multiple of {LANES}")
    tiles = f // LANES
    return LANES * max(n for n in range(1, tiles + 1) if tiles % n == 0 and n * LANES <= MAX_FFN_CHUNK)


def _router_kernel(x_ref, wr_ref, out_ref):
    logits = jnp.dot(x_ref[...], wr_ref[...], preferred_element_type=F32,
                     precision=lax.Precision.HIGHEST)
    lane = lax.broadcasted_iota(jnp.int32, logits.shape, 1)
    logits = jnp.where(lane < N_EXPERTS, logits, NEG_BIG)
    v1 = jnp.max(logits, axis=-1, keepdims=True)
    i1 = jnp.min(jnp.where(logits == v1, lane, LANES), axis=-1, keepdims=True)
    rest = jnp.where(lane == i1, NEG_BIG, logits)
    v2 = jnp.max(rest, axis=-1, keepdims=True)
    i2 = jnp.min(jnp.where(rest == v2, lane, LANES), axis=-1, keepdims=True)
    e2 = jnp.exp(v2 - v1)
    g1 = 1.0 / (1.0 + e2)
    g2 = e2 / (1.0 + e2)
    out_ref[...] = jnp.where(lane == 0, i1.astype(F32),
                             jnp.where(lane == 1, i2.astype(F32),
                                       jnp.where(lane == 2, g1, jnp.where(lane == 3, g2, 0.0))))


def _router(x2, wr_pad, tm=512):
    m, d = x2.shape
    tm = min(tm, m)
    return pl.pallas_call(
        _router_kernel,
        grid=(m // tm,),
        in_specs=[pl.BlockSpec((tm, d), lambda i: (i, 0)), pl.BlockSpec(wr_pad.shape, lambda i: (0, 0))],
        out_specs=pl.BlockSpec((tm, LANES), lambda i: (i, 0)),
        out_shape=jax.ShapeDtypeStruct((m, LANES), F32),
        compiler_params=_params("arbitrary"),
        name="router_top2",
    )(x2, wr_pad)


def _moe_add_ln_kernel(x_ref, y1_ref, y2_ref, route_ref, ln_ref, out_ref, *scratch, alpha, out_batch):
    g1 = route_ref[:, 2:3]
    g2 = route_ref[:, 3:4]
    f = g1 * y1_ref[...] + g2 * y2_ref[...]
    res = _layer_norm(alpha * x_ref[...] + f, ln_ref[0:1, :], ln_ref[1:2, :])
    if out_batch is None:
        out_ref[...] = res
    else:
        rows_scr, = scratch
        n_tiles, rows, _ = rows_scr.shape
        for j in range(n_tiles):
            rows_scr[j] = res[:, j * LANES:(j + 1) * LANES]
        for bi in range(out_batch):
            for j in range(n_tiles):
                out_ref[bi, :, j * LANES:(j + 1) * LANES] = rows_scr[
                    j, pl.ds(bi, rows // out_batch, stride=out_batch), :]


def _moe_add_ln(x2, y12, route, ln_vec, alpha, out_batch=None, tm=512):
    m, d = x2.shape
    tm = min(tm, m)
    rows = pl.BlockSpec((tm, d), lambda i: (i, 0))
    second = pl.BlockSpec((tm, d), lambda i: (i + m // tm, 0))
    if out_batch is None:
        out_spec, out_shape, scratch = rows, jax.ShapeDtypeStruct((m, d), F32), []
    else:
        out_spec = pl.BlockSpec((out_batch, tm // out_batch, d), lambda i: (0, i, 0))
        out_shape = jax.ShapeDtypeStruct((out_batch, m // out_batch, d), F32)
        scratch = [pltpu.VMEM((d // LANES, tm, LANES), F32)]
    return pl.pallas_call(
        functools.partial(_moe_add_ln_kernel, alpha=alpha, out_batch=out_batch),
        grid=(m // tm,),
        in_specs=[rows, rows, second, pl.BlockSpec((tm, LANES), lambda i: (i, 0)),
                  pl.BlockSpec(ln_vec.shape, lambda i: (0, 0))],
        out_specs=out_spec,
        out_shape=out_shape,
        scratch_shapes=scratch,
        compiler_params=_params("arbitrary"),
        name="moe_add_ln",
    )(x2, y12, y12, route, ln_vec)


def _moe(x2, xb, router_w, wg, wu, wd, first_expert, n_exp, ln_vec, alpha, tm, out_batch=None):
    m, d = x2.shape
    wr_pad = jnp.pad(router_w, ((0, 0), (0, LANES - n_exp)))
    route = _router(x2, wr_pad)
    idx = route[:, 0:2].astype(jnp.int32).reshape(-1)
    experts = jnp.arange(n_exp, dtype=jnp.int32)
    counts = jnp.sum((idx[:, None] == experts[None, :]).astype(jnp.int32), axis=0)
    padded = ((counts + tm - 1) // tm) * tm
    pad_end = jnp.cumsum(padded)
    pad_start = pad_end - padded
    start = jnp.cumsum(counts) - counts
    order = jnp.argsort(idx, stable=True)
    rank = jnp.argsort(order)
    slot_pos = pad_start[idx] + (rank - start[idx])
    rows_total = 2 * m + n_exp * tm
    tile_start = jnp.arange(rows_total // tm, dtype=jnp.int32) * tm
    tile_expert = jnp.minimum(jnp.sum((tile_start[:, None] >= pad_end[None, :]).astype(jnp.int32), axis=1),
                              n_exp - 1)
    row = jnp.arange(rows_total, dtype=jnp.int32)
    row_expert = jnp.repeat(tile_expert, tm)
    sorted_pos = jnp.clip(row - pad_start[row_expert] + start[row_expert], 0, 2 * m - 1)
    src_token = order[sorted_pos] // 2
    xg = jnp.take(xb, src_token, axis=0, mode="clip")
    yg = _swiglu(tile_expert + first_expert, xg, wg, wu, wd, tm, _ffn_chunk(wg.shape[2]))
    pos = slot_pos.reshape(m, 2).T.reshape(-1)
    y12 = jnp.take(yg, pos, axis=0, mode="clip")
    return _moe_add_ln(x2, y12, route, ln_vec, alpha, out_batch=out_batch)


def kernel(x, w_in, w_in_vres, shift_mu, shift_mu_vres, decay_up, decay_base, aaa_up, aaa_base, vres_up, vres_base, gate_up, k_k, k_a, r_k, lnx_g, lnx_b, w_out, ln1_g, ln1_b, ln2_g, ln2_b, ffn_w_gate, ffn_w_up, ffn_w_down, router, moe_w_gate, moe_w_up, moe_w_down):
    b, s, d = x.shape
    depth = w_in.shape[0]
    wdt = decay_up.shape[2]
    att = w_out.shape[1] - wdt
    alpha = float((2 * depth) ** 0.25)
    m = b * s
    ffn_tm = min(512, m)

    seg_head = jnp.arange(SEG_ROWS) // RWKV_HEAD_DIM
    seg = (seg_head[:, None] == seg_head[None, :]).astype(BF16)
    q_scale = jnp.concatenate([jnp.full((att,), ATT_HEAD_DIM ** -0.5, F32), jnp.ones((2 * att,), F32)])
    zero_tiles = jnp.zeros((m // ffn_tm,), jnp.int32)
    ffn_w = [w.astype(BF16) for w in (ffn_w_gate, ffn_w_up, ffn_w_down)]
    n_exp = moe_w_gate.shape[1]
    moe_w = [w.astype(BF16).reshape((-1,) + w.shape[2:]) for w in (moe_w_gate, moe_w_up, moe_w_down)]

    x2 = None
    v_first = None
    for l in range(depth):
        has_vres = l > 0
        w_l = w_in[l]
        c0 = 3 * att
        wqkv = (w_l[:, :c0] * q_scale).astype(BF16)
        wrkv_t = w_l[:, c0:c0 + 3 * wdt].T.astype(BF16)
        lora_w = w_l[:, c0 + 3 * wdt:]
        lora_mu = shift_mu[l, 3 * wdt:]
        if has_vres:
            lora_w = jnp.concatenate([lora_w, w_in_vres[l - 1]], axis=1)
            lora_mu = jnp.concatenate([lora_mu, shift_mu_vres[l - 1]])
        n_lora = lora_w.shape[1]
        wlora_t = jnp.pad(lora_w, ((0, 0), (0, LORA_PAD - n_lora))).T.astype(BF16)
        columns = lambda vec: jnp.broadcast_to(vec[..., None], vec.shape + (LANES,))
        mu_lora = columns(jnp.pad(lora_mu, (0, LORA_PAD - n_lora)))
        mu_rkv = columns(shift_mu[l, :3 * wdt])

        if l == 0:
            qkv, rkv_t, lora_t, x2 = _inproj(x, wqkv, wrkv_t, wlora_t, mu_rkv, mu_lora, b)
        else:
            qkv, rkv_t, lora_t = _inproj(x2, wqkv, wrkv_t, wlora_t, mu_rkv, mu_lora, b)

        att_b = _attention(qkv)

        zpad_t = lambda w, rows_before: jnp.pad(
            w, ((rows_before, LANES - rows_before - w.shape[0]), (0, 0))).T.astype(BF16)
        decay_up_p = zpad_t(decay_up[l], 0)
        aaa_up_p = zpad_t(aaa_up[l], DECAY_LORA)
        gate_up_p = gate_up[l].T.astype(BF16)
        if has_vres:
            vres_up_p = zpad_t(vres_up[l - 1], 0)
            vres_b = vres_base[l - 1]
            vf_in = v_first
        else:
            vres_up_p = jnp.zeros((wdt, LANES), BF16)
            vres_b = jnp.zeros((wdt,), F32)
            vf_in = rkv_t
        vecs = columns(jnp.stack([decay_base[l], aaa_base[l], vres_b, k_k[l], k_a[l], r_k[l]]))
        r_, k_, a_, b_, pt_, vl_, v_, g_, bonus = _rwkv_prep(
            rkv_t, lora_t, vf_in, decay_up_p, aaa_up_p, gate_up_p, vres_up_p, vecs, seg, has_vres, b)
        if not has_vres:
            v_first = v_
        y_lanes = _rwkv_scan(a_, b_, k_, r_, pt_, vl_)

        gn_vec = columns(jnp.stack([lnx_g[l], lnx_b[l]]))
        ln1 = jnp.stack([ln1_g[l], ln1_b[l]] + [jnp.zeros((d,), F32)] * 6)
        ln2 = jnp.stack([ln2_g[l], ln2_b[l]] + [jnp.zeros((d,), F32)] * 6)
        wo = w_out[l].astype(BF16)
        x2, xb = _outproj(att_b, y_lanes, bonus, g_, x2, wo[:att], wo[att:], gn_vec, ln1, seg, alpha)

        i = l // 2
        if l % 2 == 0:
            x2 = _swiglu(zero_tiles + i, xb, *ffn_w, ffn_tm, _ffn_chunk(ffn_w[0].shape[2]),
                         residual=x2, ln_vec=ln2, alpha=alpha)
        elif l + 1 < depth:
            x2 = _moe(x2, xb, router[i], *moe_w, i * n_exp, n_exp, ln2, alpha, ffn_tm)
        else:
            return _moe(x2, xb, router[i], *moe_w, i * n_exp, n_exp, ln2, alpha, ffn_tm, out_batch=b)
    return x2.reshape(s, b, d).transpose(1, 0, 2)
```

```python
import functools

import jax
import jax.numpy as jnp
from jax import lax
from jax.experimental import pallas as pl
from jax.experimental.pallas import tpu as pltpu

ATT_HEAD_DIM = 64
RWKV_HEAD_DIM = 64
DILATIONS = (1, 4, 16)
ATT_BLOCK = 128
DECAY_LORA = 64
N_EXPERTS = 8
LN_EPS = 1e-5
GN_EPS = 64e-5

LANES = 128
LORA_PAD = 384
VMEM_LIMIT = 56 * 1024 * 1024
NEG_BIG = -1e30

F32 = jnp.float32
BF16 = jnp.bfloat16


def _params(*sem):
    return pltpu.CompilerParams(dimension_semantics=sem, vmem_limit_bytes=VMEM_LIMIT)


def _inproj_kernel(x_ref, wqkv_ref, wrkv_ref, wlora_ref, mu_rkv_ref, mu_lora_ref,
                   qkv_ref, rkv_ref, lora_ref, *rest, batch, from_batch_major):
    if from_batch_major:
        x_rows_ref, carry_rkv, carry_lora, qkv_rows, x_rows = rest
        for bi in range(batch):
            for j in range(x_rows.shape[0]):
                x_rows[j, pl.ds(bi, x_ref.shape[1], stride=batch), :] = x_ref[bi, :, j * LANES:(j + 1) * LANES]
        x = jnp.concatenate([x_rows[j] for j in range(x_rows.shape[0])], axis=1)
        x_rows_ref[...] = x
    else:
        carry_rkv, carry_lora, qkv_rows = rest
        x = x_ref[...]
    xb = x.astype(BF16)
    qkv = jnp.dot(xb, wqkv_ref[...], preferred_element_type=F32)
    n_tiles, rows, _ = qkv_rows.shape
    for j in range(n_tiles):
        qkv_rows[j] = qkv[:, j * LANES:(j + 1) * LANES]
    for bi in range(batch):
        for j in range(n_tiles):
            qkv_ref[bi, :, j * LANES:(j + 1) * LANES] = qkv_rows[j, pl.ds(bi, rows // batch, stride=batch), :]

    @pl.when(pl.program_id(0) == 0)
    def _():
        carry_rkv[...] = jnp.zeros_like(carry_rkv)
        carry_lora[...] = jnp.zeros_like(carry_lora)

    nt = (((1,), (1,)), ((), ()))
    lane = lax.broadcasted_iota(jnp.int32, (1, LANES), 1)

    def shifted(wt_ref, mu_ref, carry_ref, out_ref):
        p = lax.dot_general(wt_ref[...], xb, nt, preferred_element_type=F32)
        tm = p.shape[1]
        mu = mu_ref[...]
        prev = pltpu.roll(p, batch, axis=1)
        from_carry = pltpu.roll(carry_ref[...], batch, axis=1)
        for j in range(tm // LANES):
            cols = slice(j * LANES, (j + 1) * LANES)
            prev_j = prev[:, cols]
            if j == 0:
                prev_j = jnp.where(lane < batch, from_carry, prev_j)
            out_ref[:, cols] = p[:, cols] + (prev_j - p[:, cols]) * mu
        carry_ref[...] = p[:, tm - LANES:tm]

    shifted(wrkv_ref, mu_rkv_ref, carry_rkv, rkv_ref)
    shifted(wlora_ref, mu_lora_ref, carry_lora, lora_ref)


def _inproj(x, wqkv, wrkv_t, wlora_t, mu_rkv, mu_lora, batch, tm=512):
    from_batch_major = x.ndim == 3
    d = x.shape[-1]
    m = x.size // d
    nq, nr, nl = wqkv.shape[1], wrkv_t.shape[0], wlora_t.shape[0]
    tm = min(tm, m)
    full = lambda shape: pl.BlockSpec(shape, lambda i: (0, 0))
    rows = pl.BlockSpec((tm, d), lambda i: (i, 0))
    x_spec = pl.BlockSpec((batch, tm // batch, d), lambda i: (0, i, 0)) if from_batch_major else rows
    return pl.pallas_call(
        functools.partial(_inproj_kernel, batch=batch, from_batch_major=from_batch_major),
        grid=(m // tm,),
        in_specs=[x_spec,
                  full((d, nq)), full((nr, d)), full((nl, d)), full((nr, LANES)), full((nl, LANES))],
        out_specs=[pl.BlockSpec((batch, tm // batch, nq), lambda i: (0, i, 0)),
                   pl.BlockSpec((nr, tm), lambda i: (0, i)),
                   pl.BlockSpec((nl, tm), lambda i: (0, i))] + [rows] * from_batch_major,
        out_shape=[jax.ShapeDtypeStruct((batch, m // batch, nq), F32),
                   jax.ShapeDtypeStruct((nr, m), F32),
                   jax.ShapeDtypeStruct((nl, m), F32)]
                  + [jax.ShapeDtypeStruct((m, d), F32)] * from_batch_major,
        scratch_shapes=[pltpu.VMEM((nr, LANES), F32), pltpu.VMEM((nl, LANES), F32),
                        pltpu.VMEM((nq // LANES, tm, LANES), F32)]
                       + [pltpu.VMEM((d // LANES, tm, LANES), F32)] * from_batch_major,
        compiler_params=_params("arbitrary"),
        name="inproj",
    )(x, wqkv, wrkv_t, wlora_t, mu_rkv, mu_lora)


CLASS_COPY_ROWS = 256
ATT_WINDOW = 2 * ATT_BLOCK
ATT_HEADS_PER_TILE = LANES // ATT_HEAD_DIM


def _attn_kernel(q_ref, k_ref, v_ref, out_ref, acc_ref, m_ref, l_ref, qc_ref, kc_ref, vc_ref,
                 q4_ref, k4_ref, v4_ref, tmp_ref,
                 s0_scr, s1_scr, p0_scr, p1_scr, ml0_scr, ml1_scr, bias_scr):
    seq = q_ref.shape[0]
    n_blocks = seq // ATT_BLOCK
    base = DILATIONS[1]
    sub = DILATIONS[2] // base
    qlen = seq // base
    lane = lax.broadcasted_iota(jnp.int32, (ATT_BLOCK, LANES), 1)
    in_head = [(lane >= h * ATT_HEAD_DIM) & (lane < (h + 1) * ATT_HEAD_DIM)
               for h in range(ATT_HEADS_PER_TILE)]
    rel = (lax.broadcasted_iota(jnp.int32, (ATT_BLOCK, ATT_WINDOW), 0)
           - lax.broadcasted_iota(jnp.int32, (ATT_BLOCK, ATT_WINDOW), 1))
    nt = (((1,), (1,)), ((), ()))
    sources = ((q_ref, q4_ref, qc_ref), (k_ref, k4_ref, kc_ref), (v_ref, v4_ref, vc_ref))

    s_scr, p_scr, ml_scr = (s0_scr, s1_scr), (p0_scr, p1_scr), (ml0_scr, ml1_scr)
    for ref in s_scr + p_scr + ml_scr:
        ref[...] = jnp.zeros_like(ref)
    for first_window in range(2):
        dist = rel + first_window * ATT_BLOCK
        bias_scr[first_window] = jnp.where((dist >= 0) & (dist <= ATT_BLOCK), 0.0, NEG_BIG)

    def split_by_base(c, carry):
        for j in range(qlen // CLASS_COPY_ROWS):
            src = pl.ds(c + j * CLASS_COPY_ROWS * base, CLASS_COPY_ROWS, stride=base)
            dst = pl.ds(pl.multiple_of(c * qlen, CLASS_COPY_ROWS) + j * CLASS_COPY_ROWS, CLASS_COPY_ROWS)
            for full_ref, split_ref, _ in sources:
                split_ref[dst, :] = full_ref[src, :]
        return carry

    lax.fori_loop(0, base, split_by_base, 0)

    def run_pattern(per_class, commit):
        def locate(g):
            g = jnp.clip(g, 0, n_blocks - 1)
            first = (g % per_class) == 0
            q_start = pl.multiple_of(g * ATT_BLOCK, ATT_BLOCK)
            k_start = pl.multiple_of(jnp.where(first, g, g - 1) * ATT_BLOCK, ATT_BLOCK)
            return g, q_start, k_start

        def scores(g, slot):
            _, q_start, k_start = locate(g)
            q = qc_ref[pl.ds(q_start, ATT_BLOCK), :]
            kw = kc_ref[pl.ds(k_start, ATT_WINDOW), :]
            bias = bias_scr[(q_start - k_start) // ATT_BLOCK]
            for h in range(ATT_HEADS_PER_TILE):
                qh = jnp.where(in_head[h], q, jnp.zeros_like(q))
                s_scr[slot][h] = lax.dot_general(qh, kw, nt, preferred_element_type=F32) + bias

        def softmax(slot):
            m_acc = l_acc = jnp.zeros((ATT_BLOCK, LANES), F32)
            for h in range(ATT_HEADS_PER_TILE):
                sc = s_scr[slot][h]
                m = jnp.max(sc, axis=-1, keepdims=True)
                p = jnp.exp(sc - m)
                p_scr[slot][h] = p.astype(BF16)
                m_acc = jnp.where(in_head[h], m, m_acc)
                l_acc = jnp.where(in_head[h], jnp.sum(p, axis=-1, keepdims=True), l_acc)
            ml_scr[slot][0] = m_acc
            ml_scr[slot][1] = l_acc

        def accumulate(g, slot):
            valid = g >= 0
            g, _, k_start = locate(g)
            vw = vc_ref[pl.ds(k_start, ATT_WINDOW), :]
            o = jnp.zeros((ATT_BLOCK, LANES), F32)
            for h in range(ATT_HEADS_PER_TILE):
                o = jnp.where(in_head[h], jnp.dot(p_scr[slot][h], vw, preferred_element_type=F32), o)
            commit(g, valid, o, ml_scr[slot][0], ml_scr[slot][1])

        def pipeline_pair(it, carry):
            for parity in range(2):
                i = 2 * it + parity
                accumulate(i - 4, parity)
                softmax(parity)
                scores(i, parity)
            return carry

        lax.fori_loop(0, (n_blocks + 4) // 2, pipeline_pair, 0, unroll=3)

    def merge(rows, valid, o, m_blk, l_blk):
        m_blk = jnp.where(valid, m_blk, NEG_BIG)
        m_old = m_ref[rows, :]
        m_new = jnp.maximum(m_old, m_blk)
        w_old = jnp.exp(m_old - m_new)
        w_new = jnp.where(valid, jnp.exp(m_blk - m_new), 0.0)
        acc_ref[rows, :] = acc_ref[rows, :] * w_old + o * w_new
        l_ref[rows, :] = l_ref[rows, :] * w_old + l_blk * w_new
        m_ref[rows, :] = m_new

    for j in range(seq // CLASS_COPY_ROWS):
        rows = pl.ds(j * CLASS_COPY_ROWS, CLASS_COPY_ROWS)
        for full_ref, _, class_ref in sources:
            class_ref[rows, :] = full_ref[rows, :].astype(BF16)

    def commit_dense(g, valid, o, m_blk, l_blk):
        del valid
        per_base = ATT_BLOCK // base
        for idx, val in enumerate((o, m_blk, l_blk)):
            tmp_ref[idx] = val
        for c in range(base):
            dst = pl.ds(pl.multiple_of(c * qlen + g * per_base, per_base), per_base)
            for idx, state_ref in enumerate((acc_ref, m_ref, l_ref)):
                state_ref[dst, :] = tmp_ref[idx, pl.ds(c, per_base, stride=base), :]

    run_pattern(n_blocks, commit_dense)

    for j in range(seq // CLASS_COPY_ROWS):
        rows = pl.ds(j * CLASS_COPY_ROWS, CLASS_COPY_ROWS)
        for _, split_ref, class_ref in sources:
            class_ref[rows, :] = split_ref[rows, :].astype(BF16)

    def commit_base(g, valid, o, m_blk, l_blk):
        merge(pl.ds(pl.multiple_of(g * ATT_BLOCK, ATT_BLOCK), ATT_BLOCK), valid, o, m_blk, l_blk)

    run_pattern(qlen // ATT_BLOCK, commit_base)

    sub_len = qlen // sub
    per_sub = sub_len // ATT_BLOCK

    def split_again(cls, carry):
        src = pl.ds((cls // sub) * qlen + cls % sub, sub_len, stride=sub)
        dst = pl.ds(pl.multiple_of(cls * sub_len, CLASS_COPY_ROWS), sub_len)
        for _, split_ref, class_ref in sources:
            class_ref[dst, :] = split_ref[src, :].astype(BF16)
        return carry

    lax.fori_loop(0, base * sub, split_again, 0)

    def commit_sub(g, valid, o, m_blk, l_blk):
        cls = g // per_sub
        start = (cls // sub) * qlen + cls % sub + (g % per_sub) * (ATT_BLOCK * sub)
        merge(pl.ds(start, ATT_BLOCK, stride=sub), valid, o, m_blk, l_blk)

    run_pattern(per_sub, commit_sub)

    for c in range(base):
        for j in range(qlen // CLASS_COPY_ROWS):
            rows = pl.ds(c * qlen + j * CLASS_COPY_ROWS, CLASS_COPY_ROWS)
            out_ref[pl.ds(c + j * CLASS_COPY_ROWS * base, CLASS_COPY_ROWS, stride=base), :] = (
                acc_ref[rows, :] / l_ref[rows, :])


def _attention(qkv_b):
    b, s, c3 = qkv_b.shape
    a = c3 // 3
    npair = a // LANES
    assert s % (DILATIONS[-1] * ATT_WINDOW) == 0 and (s // DILATIONS[-1]) % CLASS_COPY_ROWS == 0
    blk = lambda off: pl.BlockSpec((None, s, LANES), lambda bi, hp: (bi, 0, off + hp))
    pipe = (ATT_HEADS_PER_TILE, ATT_BLOCK, ATT_WINDOW)
    return pl.pallas_call(
        _attn_kernel,
        grid=(b, npair),
        in_specs=[blk(0), blk(npair), blk(2 * npair)],
        out_specs=blk(0),
        out_shape=jax.ShapeDtypeStruct((b, s, a), F32),
        scratch_shapes=[pltpu.VMEM((s, LANES), F32)] * 3 + [pltpu.VMEM((s, LANES), BF16)] * 3
                       + [pltpu.VMEM((s, LANES), F32)] * 3 + [pltpu.VMEM((3, ATT_BLOCK, LANES), F32)]
                       + [pltpu.VMEM(pipe, F32)] * 2 + [pltpu.VMEM(pipe, BF16)] * 2
                       + [pltpu.VMEM((2, ATT_BLOCK, LANES), F32)] * 2
                       + [pltpu.VMEM((2, ATT_BLOCK, ATT_WINDOW), F32)],
        compiler_params=_params("arbitrary", "arbitrary"),
        name="dilated_attn",
    )(qkv_b, qkv_b, qkv_b)


KEY_HALVES = 2
SEG_ROWS = 256
RECURRENCE_CHUNK = 32


def _chunk_transpose(parts, chunk):
    n = len(parts)
    lane = lax.broadcasted_iota(jnp.int32, parts[0].shape, 1)
    s = n // 2
    while s >= 1:
        swap = ((lane // chunk) & s) != 0
        new = list(parts)
        for j in range(n):
            if j & s == 0:
                lo, hi = parts[j], parts[j + s]
                if 2 * s * chunk == LANES:
                    moved = pltpu.roll(jnp.where(swap, lo, hi), s * chunk, axis=1)
                    new[j] = jnp.where(swap, moved, lo)
                    new[j + s] = jnp.where(swap, hi, moved)
                else:
                    new[j] = jnp.where(swap, pltpu.roll(hi, s * chunk, axis=1), lo)
                    new[j + s] = jnp.where(swap, hi, pltpu.roll(lo, LANES - s * chunk, axis=1))
        parts = new
        s //= 2
    return parts


def _keys_to_lanes(q, heads, batch):
    half = RWKV_HEAD_DIM // KEY_HALVES
    parts = [q[(h * KEY_HALVES + kh) * half:(h * KEY_HALVES + kh + 1) * half, :]
             for kh in range(KEY_HALVES) for h in range(heads)]
    return _chunk_transpose(parts, batch)


def _values_to_lanes(q, heads, batch):
    parts = [q[h * RWKV_HEAD_DIM:(h + 1) * RWKV_HEAD_DIM, :]
             for _ in range(KEY_HALVES) for h in range(heads)]
    return _chunk_transpose(parts, batch)


def _values_from_lanes(parts, heads, batch):
    per_half = _chunk_transpose(parts, batch)
    return jnp.concatenate([per_half[h] + per_half[heads + h] for h in range(heads)], axis=0)


def _segment_sum_t(x, seg_ref):
    seg = seg_ref[...]
    n = seg.shape[0]
    pieces = _bf16_pieces(x)
    slabs = []
    for s0 in range(0, x.shape[0], n):
        rows = slice(s0, s0 + n)
        slabs.append(sum(jnp.dot(seg, piece[rows], preferred_element_type=F32) for piece in pieces))
    return jnp.concatenate(slabs, axis=0)


def _bf16_pieces(x):
    hi = x.astype(BF16)
    rest = x - hi.astype(F32)
    mid = rest.astype(BF16)
    lo = (rest - mid.astype(F32)).astype(BF16)
    return hi, mid, lo


def _times_01_matrix(x, mat_ref):
    mat = mat_ref[...]
    return sum(jnp.dot(piece, mat, preferred_element_type=F32) for piece in _bf16_pieces(x))


def _last_step_to_lanes(q, heads, batch):
    half = RWKV_HEAD_DIM // KEY_HALVES
    n = LANES // batch
    lane = lax.broadcasted_iota(jnp.int32, (half, LANES), 1)
    out = jnp.zeros((half, LANES), F32)
    for kh in range(KEY_HALVES):
        for h in range(heads):
            j = kh * heads + h
            slab = q[(h * KEY_HALVES + kh) * half:(h * KEY_HALVES + kh + 1) * half, :]
            moved = pltpu.roll(slab, ((j - (n - 1)) * batch) % LANES, axis=1)
            out = jnp.where(lane // batch == j, moved, out)
    return out


def _rwkv_prep_kernel(rkv_ref, lora_ref, vfirst_ref, decay_up_ref, aaa_up_ref, gate_up_ref,
                      vres_up_ref, vec_ref, seg_ref, upto_ref, whole_ref,
                      r_ref, k_ref, a_ref, b_ref, pt_ref, vl_ref, v_ref, g_ref, bonus_ref,
                      *, has_vres, batch):
    wdt = rkv_ref.shape[0] // 3
    heads = wdt // RWKV_HEAD_DIM
    steps = LANES // batch
    n_groups = rkv_ref.shape[1] // LANES
    decay_base, aaa_base, vres_base = vec_ref[0], vec_ref[1], vec_ref[2]
    k_k, k_a, r_k = vec_ref[3], vec_ref[4], vec_ref[5]
    log_p_before = 0.0
    for gi in range(n_groups):
        cols = slice(gi * LANES, (gi + 1) * LANES)
        r = rkv_ref[0:wdt, cols]
        k = rkv_ref[wdt:2 * wdt, cols]
        v = rkv_ref[2 * wdt:3 * wdt, cols]
        wa = lora_ref[0:LANES, cols]
        gd = lora_ref[LANES:2 * LANES, cols]

        z = -(decay_base + jnp.dot(decay_up_ref[...], jnp.tanh(wa).astype(BF16),
                                   preferred_element_type=F32))
        softplus = jnp.maximum(z, 0.0) + jnp.log(1.0 + jnp.exp(-jnp.abs(z)))
        log_w = -jnp.exp(-softplus - 0.5)
        log_p = log_p_before + _times_01_matrix(log_w, upto_ref)
        if gi + 1 < n_groups:
            log_p_before = log_p_before + _times_01_matrix(log_w, whole_ref)
        p_t = jnp.exp(log_p)
        inv_p_t = jnp.exp(-log_p)
        p_prev = jnp.exp(log_p - log_w)
        a = jax.nn.sigmoid(aaa_base + jnp.dot(aaa_up_ref[...], wa.astype(BF16),
                                              preferred_element_type=F32))
        g_ref[:, cols] = jnp.dot(gate_up_ref[...], jax.nn.sigmoid(gd).astype(BF16),
                                 preferred_element_type=F32)
        if has_vres:
            vd = lora_ref[2 * LANES:3 * LANES, cols]
            mix = jax.nn.sigmoid(vres_base + jnp.dot(vres_up_ref[...], vd.astype(BF16),
                                                     preferred_element_type=F32))
            v = v + (vfirst_ref[:, cols] - v) * mix
        kk = k * k_k
        norm = jnp.sqrt(_segment_sum_t(kk * kk, seg_ref))
        kk = kk / jnp.maximum(norm, 1e-12)
        k2 = k * (1.0 + (a - 1.0) * k_a)
        v_ref[:, cols] = v
        bonus_ref[:, cols] = _segment_sum_t(r * k2 * r_k, seg_ref) * v

        for ref, val in ((r_ref, r * p_t), (k_ref, k2 * inv_p_t), (a_ref, -kk * p_prev),
                         (b_ref, kk * a * inv_p_t)):
            for t, part in enumerate(_keys_to_lanes(val, heads, batch)):
                ref[gi * steps + t] = part
        for t, part in enumerate(_values_to_lanes(v, heads, batch)):
            vl_ref[gi * steps + t] = part
        if gi + 1 == n_groups:
            pt_ref[0] = _last_step_to_lanes(p_t, heads, batch)


def _rwkv_prep(rkv_t, lora_t, vfirst_t, decay_up_t, aaa_up_t, gate_up_t, vres_up_t, vecs, seg,
               has_vres, batch):
    m = rkv_t.shape[1]
    wdt = rkv_t.shape[0] // 3
    heads = wdt // RWKV_HEAD_DIM
    assert KEY_HALVES * heads * batch == LANES, "recurrence layout needs 2*heads*batch == 128 lanes"
    steps = RECURRENCE_CHUNK
    tm = steps * batch
    seq = m // batch
    assert tm % LANES == 0 and seq % steps == 0
    t_idx, b_idx = jnp.arange(LANES) // batch, jnp.arange(LANES) % batch
    same_batch = b_idx[:, None] == b_idx[None, :]
    upto = (same_batch & (t_idx[:, None] <= t_idx[None, :])).astype(BF16)
    whole = same_batch.astype(BF16)
    half = RWKV_HEAD_DIM // KEY_HALVES
    cols = lambda n: pl.BlockSpec((n, tm), lambda i: (0, i))
    full = lambda arr: pl.BlockSpec(arr.shape, lambda i: (0,) * arr.ndim)
    key_spec = pl.BlockSpec((steps, half, LANES), lambda i: (i, 0, 0))
    val_spec = pl.BlockSpec((steps, RWKV_HEAD_DIM, LANES), lambda i: (i, 0, 0))
    key_out = jax.ShapeDtypeStruct((seq, half, LANES), F32)
    val_out = jax.ShapeDtypeStruct((seq, RWKV_HEAD_DIM, LANES), F32)
    nat_out = jax.ShapeDtypeStruct((wdt, m), F32)
    return pl.pallas_call(
        functools.partial(_rwkv_prep_kernel, has_vres=has_vres, batch=batch),
        grid=(m // tm,),
        in_specs=[cols(3 * wdt), cols(LORA_PAD), cols(wdt), full(decay_up_t), full(aaa_up_t),
                  full(gate_up_t), full(vres_up_t), full(vecs), full(seg), full(upto), full(whole)],
        out_specs=[key_spec] * 4 + [pl.BlockSpec((1, half, LANES), lambda i: (i, 0, 0))]
                  + [val_spec] + [cols(wdt)] * 3,
        out_shape=[key_out] * 4 + [jax.ShapeDtypeStruct((seq // steps, half, LANES), F32)]
                  + [val_out] + [nat_out] * 3,
        compiler_params=_params("arbitrary"),
        name="rwkv_prep",
    )(rkv_t, lora_t, vfirst_t, decay_up_t, aaa_up_t, gate_up_t, vres_up_t, vecs, seg, upto, whole)


def _rwkv_scan_kernel(a_ref, a_next_ref, b_ref, k_ref, r_ref, pt_ref, v_ref, y_ref, state_ref, sa_ref):
    @pl.when(pl.program_id(0) == 0)
    def _():
        state_ref[...] = jnp.zeros_like(state_ref)
        sa_ref[...] = jnp.zeros_like(sa_ref)

    steps, nk, _ = a_ref.shape
    nv = v_ref.shape[1]

    def step(t, sa_half, next_a_row, chunk_end):
        sa = sa_half + pltpu.roll(sa_half, LANES // KEY_HALVES, axis=1)
        row = lambda ref, kx: jnp.broadcast_to(ref[t, pl.ds(kx, 1), :], (nv, LANES))
        v_t = v_ref[t]
        acc_y = None
        acc_sa = None
        for kx in range(nk):
            new = state_ref[kx] + v_t * row(k_ref, kx) + sa * row(b_ref, kx)
            term_y = new * row(r_ref, kx)
            if chunk_end:
                new = new * jnp.broadcast_to(pt_ref[0, pl.ds(kx, 1), :], (nv, LANES))
            state_ref[kx] = new
            term_sa = new * jnp.broadcast_to(next_a_row(kx), (nv, LANES))
            acc_y = term_y if acc_y is None else acc_y + term_y
            acc_sa = term_sa if acc_sa is None else acc_sa + term_sa
        y_ref[t] = acc_y
        return acc_sa

    def body(t, sa_half):
        return step(t, sa_half, lambda kx: a_ref[t + 1, pl.ds(kx, 1), :], False)

    sa_half = lax.fori_loop(0, steps - 1, body, sa_ref[...], unroll=2)
    sa_ref[...] = step(steps - 1, sa_half, lambda kx: a_next_ref[0, pl.ds(kx, 1), :], True)


def _rwkv_scan(a_t, b_t, k_t, r_t, pt, v_t):
    s, nk, _ = a_t.shape
    nv = v_t.shape[1]
    steps = RECURRENCE_CHUNK
    n_blocks = s // steps
    kin = pl.BlockSpec((steps, nk, LANES), lambda i: (i, 0, 0))
    knext = pl.BlockSpec((1, nk, LANES), lambda i: (jnp.minimum((i + 1) * steps, s - 1), 0, 0))
    pin = pl.BlockSpec((1, nk, LANES), lambda i: (i, 0, 0))
    vin = pl.BlockSpec((steps, nv, LANES), lambda i: (i, 0, 0))
    return pl.pallas_call(
        _rwkv_scan_kernel,
        grid=(n_blocks,),
        in_specs=[kin, knext] + [kin] * 3 + [pin, vin],
        out_specs=vin,
        out_shape=jax.ShapeDtypeStruct((s, nv, LANES), F32),
        scratch_shapes=[pltpu.VMEM((nk, nv, LANES), F32), pltpu.VMEM((nv, LANES), F32)],
        compiler_params=_params("arbitrary"),
        name="rwkv_scan",
    )(a_t, a_t, b_t, k_t, r_t, pt, v_t)


def _layer_norm(z, g, b):
    mu = jnp.mean(z, axis=-1, keepdims=True)
    zc = z - mu
    var = jnp.mean(zc * zc, axis=-1, keepdims=True)
    return zc * lax.rsqrt(var + LN_EPS) * g + b


def _outproj_kernel(att_ref, y_ref, bonus_ref, g_ref, x_ref, wa_ref, wr_ref, vec_ref, ln_ref,
                    seg_ref, out_ref, outb_ref, att_rows, *, alpha, batch):
    n_tiles = att_rows.shape[0]
    for bi in range(batch):
        for j in range(n_tiles):
            att_rows[j, pl.ds(bi, att_ref.shape[1], stride=batch), :] = att_ref[bi, :, j * LANES:(j + 1) * LANES]
    att = jnp.concatenate([att_rows[j] for j in range(n_tiles)], axis=1)

    heads = bonus_ref.shape[0] // RWKV_HEAD_DIM
    steps = LANES // batch
    inv_n = 1.0 / RWKV_HEAD_DIM
    n_groups = y_ref.shape[0] // steps
    def lanes_to_channels(gi):
        return _values_from_lanes([y_ref[gi * steps + t] for t in range(steps)], heads, batch)

    def finish(gi, y):
        cols = slice(gi * LANES, (gi + 1) * LANES)
        mu = _segment_sum_t(y, seg_ref) * inv_n
        yc = y - mu
        var = _segment_sum_t(yc * yc, seg_ref) * inv_n
        yn = yc * lax.rsqrt(var + GN_EPS) * vec_ref[0] + vec_ref[1]
        return ((yn + bonus_ref[:, cols]) * g_ref[:, cols]).T

    rw_groups = []
    y_prev = lanes_to_channels(0)
    for gi in range(1, n_groups + 1):
        y_next = lanes_to_channels(gi) if gi < n_groups else None
        rw_groups.append(finish(gi - 1, y_prev))
        y_prev = y_next
    rw = jnp.concatenate(rw_groups, axis=0)

    mix = (jnp.dot(att.astype(BF16), wa_ref[...], preferred_element_type=F32)
           + jnp.dot(rw.astype(BF16), wr_ref[...], preferred_element_type=F32))
    out = _layer_norm(alpha * x_ref[...] + mix, ln_ref[0:1, :], ln_ref[1:2, :])
    out_ref[...] = out
    outb_ref[...] = out.astype(BF16)


def _outproj(att_b, y_lanes, bonus, g, x2, wa, wr, gn_vec, ln_vec, seg, alpha, tm=512):
    m, d = x2.shape
    batch, _, a = att_b.shape
    tm = min(tm, m)
    assert tm % LANES == 0
    rows = lambda n: pl.BlockSpec((tm, n), lambda i: (i, 0))
    cols = lambda arr: pl.BlockSpec((arr.shape[0], tm), lambda i: (0, i))
    full = lambda arr: pl.BlockSpec(arr.shape, lambda i: (0,) * arr.ndim)
    att_spec = pl.BlockSpec((batch, tm // batch, a), lambda i: (0, i, 0))
    y_spec = pl.BlockSpec((tm // batch,) + y_lanes.shape[1:], lambda i: (i, 0, 0))
    return pl.pallas_call(
        functools.partial(_outproj_kernel, alpha=alpha, batch=batch),
        grid=(m // tm,),
        in_specs=[att_spec, y_spec, cols(bonus), cols(g)]
                 + [rows(d), full(wa), full(wr), full(gn_vec), full(ln_vec), full(seg)],
        out_specs=[rows(d), rows(d)],
        out_shape=[jax.ShapeDtypeStruct((m, d), F32), jax.ShapeDtypeStruct((m, d), BF16)],
        scratch_shapes=[pltpu.VMEM((a // LANES, tm, LANES), F32)],
        compiler_params=_params("arbitrary"),
        name="outproj_ln",
    )(att_b, y_lanes, bonus, g, x2, wa, wr, gn_vec, ln_vec, seg)


def _swiglu_kernel(expert_ref, x_ref, wg_ref, wu_ref, wd_ref, *rest, alpha):
    if alpha is None:
        out_ref, acc_ref = rest
    else:
        res_ref, ln_ref, out_ref, acc_ref = rest
    j = pl.program_id(1)

    @pl.when(j == 0)
    def _():
        acc_ref[...] = jnp.zeros_like(acc_ref)

    x = x_ref[...]
    gate = jnp.dot(x, wg_ref[0], preferred_element_type=F32)
    up = jnp.dot(x, wu_ref[0], preferred_element_type=F32)
    h = (gate * jax.nn.sigmoid(gate) * up).astype(BF16)
    acc_ref[...] += jnp.dot(h, wd_ref[0], preferred_element_type=F32)

    @pl.when(j == pl.num_programs(1) - 1)
    def _():
        if alpha is None:
            out_ref[...] = acc_ref[...]
        else:
            out_ref[...] = _layer_norm(alpha * res_ref[...] + acc_ref[...], ln_ref[0:1, :], ln_ref[1:2, :])


def _swiglu(tile_expert, xb, wg, wu, wd, tm, fc, residual=None, ln_vec=None, alpha=None):
    m, d = xb.shape
    f = wg.shape[2]
    rows = pl.BlockSpec((tm, d), lambda i, j, e: (i, 0))
    extra_specs, extra_args = [], []
    if residual is not None:
        extra_specs = [rows, pl.BlockSpec(ln_vec.shape, lambda i, j, e: (0, 0))]
        extra_args = [residual, ln_vec]
    return pl.pallas_call(
        functools.partial(_swiglu_kernel, alpha=alpha),
        grid_spec=pltpu.PrefetchScalarGridSpec(
            num_scalar_prefetch=1,
            grid=(m // tm, f // fc),
            in_specs=[rows,
                      pl.BlockSpec((1, d, fc), lambda i, j, e: (e[i], 0, j)),
                      pl.BlockSpec((1, d, fc), lambda i, j, e: (e[i], 0, j)),
                      pl.BlockSpec((1, fc, d), lambda i, j, e: (e[i], j, 0))] + extra_specs,
            out_specs=rows,
            scratch_shapes=[pltpu.VMEM((tm, d), F32)]),
        out_shape=jax.ShapeDtypeStruct((m, d), F32),
        compiler_params=_params("arbitrary", "arbitrary"),
        name="swiglu",
    )(tile_expert, xb, wg, wu, wd, *extra_args)


MAX_FFN_CHUNK = 2048


def _ffn_chunk(f):
    if f % LANES:
        raise ValueError(f"feed-forward width {f} is not a multiple of {LANES}")
    tiles = f // LANES
    return LANES * max(n for n in range(1, tiles + 1) if tiles % n == 0 and n * LANES <= MAX_FFN_CHUNK)


def _router_kernel(x_ref, wr_ref, out_ref):
    logits = jnp.dot(x_ref[...], wr_ref[...], preferred_element_type=F32,
                     precision=lax.Precision.HIGHEST)
    lane = lax.broadcasted_iota(jnp.int32, logits.shape, 1)
    logits = jnp.where(lane < N_EXPERTS, logits, NEG_BIG)
    v1 = jnp.max(logits, axis=-1, keepdims=True)
    i1 = jnp.min(jnp.where(logits == v1, lane, LANES), axis=-1, keepdims=True)
    rest = jnp.where(lane == i1, NEG_BIG, logits)
    v2 = jnp.max(rest, axis=-1, keepdims=True)
    i2 = jnp.min(jnp.where(rest == v2, lane, LANES), axis=-1, keepdims=True)
    e2 = jnp.exp(v2 - v1)
    g1 = 1.0 / (1.0 + e2)
    g2 = e2 / (1.0 + e2)
    out_ref[...] = jnp.where(lane == 0, i1.astype(F32),
                             jnp.where(lane == 1, i2.astype(F32),
                                       jnp.where(lane == 2, g1, jnp.where(lane == 3, g2, 0.0))))


def _router(x2, wr_pad, tm=512):
    m, d = x2.shape
    tm = min(tm, m)
    return pl.pallas_call(
        _router_kernel,
        grid=(m // tm,),
        in_specs=[pl.BlockSpec((tm, d), lambda i: (i, 0)), pl.BlockSpec(wr_pad.shape, lambda i: (0, 0))],
        out_specs=pl.BlockSpec((tm, LANES), lambda i: (i, 0)),
        out_shape=jax.ShapeDtypeStruct((m, LANES), F32),
        compiler_params=_params("arbitrary"),
        name="router_top2",
    )(x2, wr_pad)


def _moe_add_ln_kernel(x_ref, y1_ref, y2_ref, route_ref, ln_ref, out_ref, *scratch, alpha, out_batch):
    g1 = route_ref[:, 2:3]
    g2 = route_ref[:, 3:4]
    f = g1 * y1_ref[...] + g2 * y2_ref[...]
    res = _layer_norm(alpha * x_ref[...] + f, ln_ref[0:1, :], ln_ref[1:2, :])
    if out_batch is None:
        out_ref[...] = res
    else:
        rows_scr, = scratch
        n_tiles, rows, _ = rows_scr.shape
        for j in range(n_tiles):
            rows_scr[j] = res[:, j * LANES:(j + 1) * LANES]
        for bi in range(out_batch):
            for j in range(n_tiles):
                out_ref[bi, :, j * LANES:(j + 1) * LANES] = rows_scr[
                    j, pl.ds(bi, rows // out_batch, stride=out_batch), :]


def _moe_add_ln(x2, y12, route, ln_vec, alpha, out_batch=None, tm=512):
    m, d = x2.shape
    tm = min(tm, m)
    rows = pl.BlockSpec((tm, d), lambda i: (i, 0))
    second = pl.BlockSpec((tm, d), lambda i: (i + m // tm, 0))
    if out_batch is None:
        out_spec, out_shape, scratch = rows, jax.ShapeDtypeStruct((m, d), F32), []
    else:
        out_spec = pl.BlockSpec((out_batch, tm // out_batch, d), lambda i: (0, i, 0))
        out_shape = jax.ShapeDtypeStruct((out_batch, m // out_batch, d), F32)
        scratch = [pltpu.VMEM((d // LANES, tm, LANES), F32)]
    return pl.pallas_call(
        functools.partial(_moe_add_ln_kernel, alpha=alpha, out_batch=out_batch),
        grid=(m // tm,),
        in_specs=[rows, rows, second, pl.BlockSpec((tm, LANES), lambda i: (i, 0)),
                  pl.BlockSpec(ln_vec.shape, lambda i: (0, 0))],
        out_specs=out_spec,
        out_shape=out_shape,
        scratch_shapes=scratch,
        compiler_params=_params("arbitrary"),
        name="moe_add_ln",
    )(x2, y12, y12, route, ln_vec)


def _moe(x2, xb, router_w, wg, wu, wd, first_expert, n_exp, ln_vec, alpha, tm, out_batch=None):
    m, d = x2.shape
    wr_pad = jnp.pad(router_w, ((0, 0), (0, LANES - n_exp)))
    route = _router(x2, wr_pad)
    idx = route[:, 0:2].astype(jnp.int32).reshape(-1)
    experts = jnp.arange(n_exp, dtype=jnp.int32)
    counts = jnp.sum((idx[:, None] == experts[None, :]).astype(jnp.int32), axis=0)
    padded = ((counts + tm - 1) // tm) * tm
    pad_end = jnp.cumsum(padded)
    pad_start = pad_end - padded
    start = jnp.cumsum(counts) - counts
    order = jnp.argsort(idx, stable=True)
    rank = jnp.argsort(order)
    slot_pos = pad_start[idx] + (rank - start[idx])
    rows_total = 2 * m + n_exp * tm
    tile_start = jnp.arange(rows_total // tm, dtype=jnp.int32) * tm
    tile_expert = jnp.minimum(jnp.sum((tile_start[:, None] >= pad_end[None, :]).astype(jnp.int32), axis=1),
                              n_exp - 1)
    row = jnp.arange(rows_total, dtype=jnp.int32)
    row_expert = jnp.repeat(tile_expert, tm)
    sorted_pos = jnp.clip(row - pad_start[row_expert] + start[row_expert], 0, 2 * m - 1)
    src_token = order[sorted_pos] // 2
    xg = jnp.take(xb, src_token, axis=0, mode="clip")
    yg = _swiglu(tile_expert + first_expert, xg, wg, wu, wd, tm, _ffn_chunk(wg.shape[2]))
    pos = slot_pos.reshape(m, 2).T.reshape(-1)
    y12 = jnp.take(yg, pos, axis=0, mode="clip")
    return _moe_add_ln(x2, y12, route, ln_vec, alpha, out_batch=out_batch)


def kernel(x, w_in, w_in_vres, shift_mu, shift_mu_vres, decay_up, decay_base, aaa_up, aaa_base, vres_up, vres_base, gate_up, k_k, k_a, r_k, lnx_g, lnx_b, w_out, ln1_g, ln1_b, ln2_g, ln2_b, ffn_w_gate, ffn_w_up, ffn_w_down, router, moe_w_gate, moe_w_up, moe_w_down):
    b, s, d = x.shape
    depth = w_in.shape[0]
    wdt = decay_up.shape[2]
    att = w_out.shape[1] - wdt
    alpha = float((2 * depth) ** 0.25)
    m = b * s
    ffn_tm = min(512, m)

    seg_head = jnp.arange(SEG_ROWS) // RWKV_HEAD_DIM
    seg = (seg_head[:, None] == seg_head[None, :]).astype(BF16)
    q_scale = jnp.concatenate([jnp.full((att,), ATT_HEAD_DIM ** -0.5, F32), jnp.ones((2 * att,), F32)])
    zero_tiles = jnp.zeros((m // ffn_tm,), jnp.int32)
    ffn_w = [w.astype(BF16) for w in (ffn_w_gate, ffn_w_up, ffn_w_down)]
    n_exp = moe_w_gate.shape[1]
    moe_w = [w.astype(BF16).reshape((-1,) + w.shape[2:]) for w in (moe_w_gate, moe_w_up, moe_w_down)]

    x2 = None
    v_first = None
    for l in range(depth):
        has_vres = l > 0
        w_l = w_in[l]
        c0 = 3 * att
        wqkv = (w_l[:, :c0] * q_scale).astype(BF16)
        wrkv_t = w_l[:, c0:c0 + 3 * wdt].T.astype(BF16)
        lora_w = w_l[:, c0 + 3 * wdt:]
        lora_mu = shift_mu[l, 3 * wdt:]
        if has_vres:
            lora_w = jnp.concatenate([lora_w, w_in_vres[l - 1]], axis=1)
            lora_mu = jnp.concatenate([lora_mu, shift_mu_vres[l - 1]])
        n_lora = lora_w.shape[1]
        wlora_t = jnp.pad(lora_w, ((0, 0), (0, LORA_PAD - n_lora))).T.astype(BF16)
        columns = lambda vec: jnp.broadcast_to(vec[..., None], vec.shape + (LANES,))
        mu_lora = columns(jnp.pad(lora_mu, (0, LORA_PAD - n_lora)))
        mu_rkv = columns(shift_mu[l, :3 * wdt])

        if l == 0:
            qkv, rkv_t, lora_t, x2 = _inproj(x, wqkv, wrkv_t, wlora_t, mu_rkv, mu_lora, b)
        else:
            qkv, rkv_t, lora_t = _inproj(x2, wqkv, wrkv_t, wlora_t, mu_rkv, mu_lora, b)

        att_b = _attention(qkv)

        zpad_t = lambda w, rows_before: jnp.pad(
            w, ((rows_before, LANES - rows_before - w.shape[0]), (0, 0))).T.astype(BF16)
        decay_up_p = zpad_t(decay_up[l], 0)
        aaa_up_p = zpad_t(aaa_up[l], DECAY_LORA)
        gate_up_p = gate_up[l].T.astype(BF16)
        if has_vres:
            vres_up_p = zpad_t(vres_up[l - 1], 0)
            vres_b = vres_base[l - 1]
            vf_in = v_first
        else:
            vres_up_p = jnp.zeros((wdt, LANES), BF16)
            vres_b = jnp.zeros((wdt,), F32)
            vf_in = rkv_t
        vecs = columns(jnp.stack([decay_base[l], aaa_base[l], vres_b, k_k[l], k_a[l], r_k[l]]))
        r_, k_, a_, b_, pt_, vl_, v_, g_, bonus = _rwkv_prep(
            rkv_t, lora_t, vf_in, decay_up_p, aaa_up_p, gate_up_p, vres_up_p, vecs, seg, has_vres, b)
        if not has_vres:
            v_first = v_
        y_lanes = _rwkv_scan(a_, b_, k_, r_, pt_, vl_)

        gn_vec = columns(jnp.stack([lnx_g[l], lnx_b[l]]))
        ln1 = jnp.stack([ln1_g[l], ln1_b[l]] + [jnp.zeros((d,), F32)] * 6)
        ln2 = jnp.stack([ln2_g[l], ln2_b[l]] + [jnp.zeros((d,), F32)] * 6)
        wo = w_out[l].astype(BF16)
        x2, xb = _outproj(att_b, y_lanes, bonus, g_, x2, wo[:att], wo[att:], gn_vec, ln1, seg, alpha)

        i = l // 2
        if l % 2 == 0:
            x2 = _swiglu(zero_tiles + i, xb, *ffn_w, ffn_tm, _ffn_chunk(ffn_w[0].shape[2]),
                         residual=x2, ln_vec=ln2, alpha=alpha)
        elif l + 1 < depth:
            x2 = _moe(x2, xb, router[i], *moe_w, i * n_exp, n_exp, ln2, alpha, ffn_tm)
        else:
            return _moe(x2, xb, router[i], *moe_w, i * n_exp, n_exp, ln2, alpha, ffn_tm, out_batch=b)
    return x2.reshape(s, b, d).transpose(1, 0, 2)
```
